```python
import math
import jax, jax.numpy as jnp
from jax import lax
import numpy as np

D_MODEL = 1024
BATCH = 4
SEQ = 4096
DEPTH = 4
DEC_BATCH = 32
DEC_SEQ = 4
PAST_LEN = 8192
PAGE_SIZE = 128

N_MIXERS = 3
N_A = (DEPTH + 2) // 3
N_B = (DEPTH + 1) // 3
N_C = DEPTH // 3
DN_ALPHA = (2.0 * DEPTH) ** 0.25
DN_BETA = (8.0 * DEPTH) ** -0.25
EPS = 1e-5
CONV_W = 4
LRU_W = D_MODEL
LRU_BLOCKS = 16
LRU_BS = LRU_W // LRU_BLOCKS
LRU_C = 8.0
ATT_HEADS = 8
ATT_DH = D_MODEL // (2 * ATT_HEADS)
ATT_W = ATT_HEADS * 2 * ATT_DH
Q_BLOCK = 128
SSD_INNER = 2 * D_MODEL
SSD_HEADDIM = 64
SSD_HEADS = SSD_INNER // SSD_HEADDIM
SSD_GROUPS = 8
SSD_HPG = SSD_HEADS // SSD_GROUPS
SSD_STATE = 128
SSD_CONV_DIM = SSD_INNER + 2 * SSD_GROUPS * SSD_STATE
SSD_PROJ = SSD_INNER + SSD_CONV_DIM + SSD_HEADS
SSD_CHUNK = 128
POOL_NUM, POOL_DEN = 5, 4

kernel_name = 'hybrid_rglru_diffattn_ssd_decode_step'


def layer_norm(x, g, b):
    xf = x.astype(jnp.float32)
    mu = jnp.mean(xf, -1, keepdims=True)
    var = jnp.mean(jnp.square(xf - mu), -1, keepdims=True)
    return ((xf - mu) * lax.rsqrt(var + EPS) * g.astype(jnp.float32) + b.astype(jnp.float32)).astype(x.dtype)


def rms_norm(x, w):
    xf = x.astype(jnp.float32)
    return (xf * lax.rsqrt(jnp.mean(xf * xf, -1, keepdims=True) + EPS) * w.astype(jnp.float32)).astype(x.dtype)


def causal_dwconv(u, buf, w, b):
    t = u.shape[1]
    up = jnp.concatenate([buf.astype(u.dtype), u], axis=1)
    y = up[:, 0:t] * w[0] + b
    for k in range(1, CONV_W):
        y = y + up[:, k:k + t] * w[k]
    return y, up[:, t:]


def rglru_mixer(x, conv_buf, h0, in_w, conv_w, conv_b, r_w, r_b, i_w, i_b, lam, out_w):
    f32 = jnp.float32
    gate, u = jnp.split(x @ in_w, 2, axis=-1)
    u, new_buf = causal_dwconv(u, conv_buf, conv_w, conv_b)
    bsz, t, _ = u.shape
    ub = u.reshape(bsz, t, LRU_BLOCKS, LRU_BS)
    r = jax.nn.sigmoid(jnp.einsum('btnk,nkj->btnj', ub, r_w).reshape(bsz, t, LRU_W) + r_b)
    ig = jax.nn.sigmoid(jnp.einsum('btnk,nkj->btnj', ub, i_w).reshape(bsz, t, LRU_W) + i_b)
    log_a = -LRU_C * r.astype(f32) * jax.nn.softplus(-lam.astype(f32))
    a = jnp.exp(log_a)
    bterm = jnp.sqrt(-jnp.expm1(2.0 * log_a)) * (ig * u).astype(f32)
    bterm = bterm.at[:, 0].add(a[:, 0] * h0.astype(f32))

    def combine(left, right):
        a_l, b_l = left
        a_r, b_r = right
        return a_l * a_r, a_r * b_l + b_r

    _, h = lax.associative_scan(combine, (a, bterm), axis=1)
    y = (h.astype(x.dtype) * jax.nn.silu(gate)) @ out_w
    return y, new_buf, h[:, -1].astype(x.dtype)


def diff_lambda(lam_p, lam_init):
    lp = lam_p.astype(jnp.float32)
    return jnp.exp(jnp.sum(lp[0] * lp[1])) - jnp.exp(jnp.sum(lp[2] * lp[3])) + lam_init


def diff_attn_project(x, in_w):
    bsz, t, _ = x.shape
    q, k, v, gate = jnp.split(x @ in_w, 4, axis=-1)
    q = q.reshape(bsz, t, ATT_HEADS, 2, ATT_DH)
    k = k.reshape(bsz, t, ATT_HEADS, 2, ATT_DH)
    v = v.reshape(bsz, t, ATT_HEADS, 2 * ATT_DH)
    return q, k, v, gate


def diff_attn_prompt(q, k, v, lam):
    bsz, t = q.shape[:2]
    nblk = t // Q_BLOCK
    scale = ATT_DH ** -0.5
    qb = jnp.moveaxis(q.reshape(bsz, nblk, Q_BLOCK, ATT_HEADS, 2, ATT_DH), 1, 0)
    kpos = jnp.arange(t)

    def block(args):
        qblk, start = args
        s = jnp.einsum('bqhcd,bkhcd->bhcqk', qblk, k).astype(jnp.float32) * scale
        qpos = start + jnp.arange(Q_BLOCK)
        mask = kpos[None, :] <= qpos[:, None]
        p = jax.nn.softmax(jnp.where(mask, s, -jnp.inf), axis=-1)
        w = p[:, :, 0] - lam * p[:, :, 1]
        return jnp.einsum('bhqk,bkhe->bqhe', w.astype(v.dtype), v)

    o = lax.map(block, (qb, jnp.arange(nblk) * Q_BLOCK))
    return jnp.moveaxis(o, 0, 1).reshape(bsz, t, ATT_HEADS, 2 * ATT_DH)


def diff_attn_sample(q, k, v, k_past, v_past, lam):
    t = q.shape[1]
    n_past = k_past.shape[1]
    scale = ATT_DH ** -0.5
    s_past = jnp.einsum('bqhcd,bkhcd->bhcqk', q, k_past).astype(jnp.float32) * scale
    s_new = jnp.einsum('bqhcd,bkhcd->bhcqk', q, k).astype(jnp.float32) * scale
    s_new = jnp.where(jnp.tril(jnp.ones((t, t), bool)), s_new, -jnp.inf)
    p = jax.nn.softmax(jnp.concatenate([s_past, s_new], axis=-1), axis=-1)
    w = (p[:, :, 0] - lam * p[:, :, 1]).astype(v.dtype)
    return (jnp.einsum('bhqk,bkhe->bqhe', w[..., :n_past], v_past)
            + jnp.einsum('bhqk,bkhe->bqhe', w[..., n_past:], v))


def diff_attn_output(o, gate, lam_init, subln_w, out_w):
    bsz, t = o.shape[:2]
    o = rms_norm(o, subln_w) * (1.0 - lam_init)
    return (o.reshape(bsz, t, ATT_W) * jax.nn.silu(gate)) @ out_w


def ssd_scan(xh, dt, a, bm, cm, h0):
    f32 = jnp.float32
    bsz, t = xh.shape[:2]
    q = SSD_CHUNK if t % SSD_CHUNK == 0 else t
    nc = t // q
    xdt = (xh.astype(f32) * dt[..., None]).reshape(bsz, nc, q, SSD_GROUPS, SSD_HPG, SSD_HEADDIM)
    adt = jnp.moveaxis((dt * a).reshape(bsz, nc, q, SSD_GROUPS, SSD_HPG), 2, -1)
    bc = bm.astype(f32).reshape(bsz, nc, q, SSD_GROUPS, SSD_STATE)
    cc = cm.astype(f32).reshape(bsz, nc, q, SSD_GROUPS, SSD_STATE)
    cs = lax.cumsum(adt, axis=adt.ndim - 1)
    causal = jnp.tril(jnp.ones((q, q), bool))
    lmat = jnp.exp(jnp.where(causal, cs[..., :, None] - cs[..., None, :], -jnp.inf))
    cb = jnp.einsum('bclgn,bcsgn->bcgls', cc, bc)
    y_diag = jnp.einsum('bcgls,bcgrls,bcsgrp->bclgrp', cb, lmat, xdt)
    states = jnp.einsum('bclgn,bcgrl,bclgrp->bcgrpn', bc, jnp.exp(cs[..., -1:] - cs), xdt)

    def step(h, inp):
        decay, st = inp
        return h * decay[..., None, None] + st, h

    h_last, h_prev = lax.scan(step, h0.astype(f32),
                              (jnp.moveaxis(jnp.exp(cs[..., -1]), 1, 0), jnp.moveaxis(states, 1, 0)))
    h_prev = jnp.moveaxis(h_prev, 0, 1)
    y_off = jnp.einsum('bclgn,bcgrpn,bcgrl->bclgrp', cc, h_prev, jnp.exp(cs))
    return (y_diag + y_off).reshape(bsz, t, SSD_GROUPS, SSD_HPG, SSD_HEADDIM), h_last


def ssd_mixer(x, conv_buf, h0, in_w, conv_w, conv_b, dt_bias, a_log, d_skip, norm_w, out_w):
    f32 = jnp.float32
    bsz, t, _ = x.shape
    proj = x @ in_w
    z = proj[..., :SSD_INNER]
    xbc = proj[..., SSD_INNER:SSD_INNER + SSD_CONV_DIM]
    dt_raw = proj[..., SSD_INNER + SSD_CONV_DIM:]
    xbc, new_buf = causal_dwconv(xbc, conv_buf, conv_w, conv_b)
    xbc = jax.nn.silu(xbc)
    gn = SSD_GROUPS * SSD_STATE
    xh = xbc[..., :SSD_INNER].reshape(bsz, t, SSD_GROUPS, SSD_HPG, SSD_HEADDIM)
    bm = xbc[..., SSD_INNER:SSD_INNER + gn].reshape(bsz, t, SSD_GROUPS, SSD_STATE)
    cm = xbc[..., SSD_INNER + gn:].reshape(bsz, t, SSD_GROUPS, SSD_STATE)
    dt = jax.nn.softplus(dt_raw.astype(f32) + dt_bias.astype(f32)).reshape(bsz, t, SSD_GROUPS, SSD_HPG)
    a = -jnp.exp(a_log.astype(f32)).reshape(SSD_GROUPS, SSD_HPG)
    h0g = h0.reshape(bsz, SSD_GROUPS, SSD_HPG, SSD_HEADDIM, SSD_STATE)
    y, h_last = ssd_scan(xh, dt, a, bm, cm, h0g)
    y = y + d_skip.astype(f32).reshape(SSD_GROUPS, SSD_HPG)[..., None] * xh.astype(f32)
    y = y.reshape(bsz, t, SSD_INNER).astype(x.dtype)
    y = rms_norm(y * jax.nn.silu(z), norm_w)
    return y @ out_w, new_buf, h_last.reshape(bsz, SSD_HEADS, SSD_HEADDIM, SSD_STATE).astype(x.dtype)


def setup_inputs(seed: int = 0) -> dict:
    key = jax.random.key(seed)
    ks = iter(jax.random.split(key, 48))
    f32 = jnp.float32

    def nrm(shape, scale):
        return jax.random.normal(next(ks), shape, f32) * scale

    n_pages = PAST_LEN // PAGE_SIZE
    n_used = DEC_BATCH * n_pages
    n_pool = (POOL_NUM * n_used + POOL_DEN - 1) // POOL_DEN
    page_table = jax.random.permutation(next(ks), n_pool)[:n_used].reshape(DEC_BATCH, n_pages).astype(jnp.int32)
    u = jax.random.uniform(next(ks), (N_A, LRU_W), f32, 0.9, 0.999)
    a_base = jnp.exp(jnp.log(u) / LRU_C)
    a_lambda = jnp.log(a_base) - jnp.log1p(-a_base)
    dt0 = jnp.exp(jax.random.uniform(next(ks), (N_C, SSD_HEADS), f32, math.log(1e-3), math.log(1e-1)))
    c_dt_bias = dt0 + jnp.log(-jnp.expm1(-dt0))
    c_a_log = jnp.log(jax.random.uniform(next(ks), (N_C, SSD_HEADS), f32, 1.0, 16.0))
    return dict(
        x_prompt=nrm((BATCH, SEQ, D_MODEL), 1.0),
        x_sample=nrm((DEC_BATCH, DEC_SEQ, D_MODEL), 1.0),
        cache_k=nrm((N_B, n_pool, PAGE_SIZE, ATT_HEADS, 2, ATT_DH), 1.0),
        cache_v=nrm((N_B, n_pool, PAGE_SIZE, ATT_HEADS, 2 * ATT_DH), 1.0),
        page_table=page_table,
        state_lru_conv=nrm((N_A, DEC_BATCH, CONV_W - 1, LRU_W), 1.0),
        state_lru_h=nrm((N_A, DEC_BATCH, LRU_W), 0.5),
        state_ssd_conv=nrm((N_C, DEC_BATCH, CONV_W - 1, SSD_CONV_DIM), 1.0),
        state_ssd_h=nrm((N_C, DEC_BATCH, SSD_HEADS, SSD_HEADDIM, SSD_STATE), 0.1),
        ln_g=1.0 + nrm((DEPTH, D_MODEL), 0.01),
        ln_b=nrm((DEPTH, D_MODEL), 0.01),
        a_in_w=nrm((N_A, D_MODEL, 2 * LRU_W), D_MODEL ** -0.5),
        a_conv_w=nrm((N_A, CONV_W, LRU_W), CONV_W ** -0.5),
        a_conv_b=nrm((N_A, LRU_W), 0.01),
        a_gate_r_w=nrm((N_A, LRU_BLOCKS, LRU_BS, LRU_BS), LRU_BS ** -0.5),
        a_gate_r_b=nrm((N_A, LRU_W), 0.01),
        a_gate_i_w=nrm((N_A, LRU_BLOCKS, LRU_BS, LRU_BS), LRU_BS ** -0.5),
        a_gate_i_b=nrm((N_A, LRU_W), 0.01),
        a_lambda=a_lambda,
        a_out_w=nrm((N_A, LRU_W, D_MODEL), LRU_W ** -0.5 * DN_BETA),
        b_in_w=nrm((N_B, D_MODEL, 4 * ATT_W), D_MODEL ** -0.5),
        b_lambda=nrm((N_B, 4, ATT_DH), 0.1),
        b_subln_w=1.0 + nrm((N_B, 2 * ATT_DH), 0.01),
        b_out_w=nrm((N_B, ATT_W, D_MODEL), ATT_W ** -0.5 * DN_BETA),
        c_in_w=nrm((N_C, D_MODEL, SSD_PROJ), D_MODEL ** -0.5),
        c_conv_w=nrm((N_C, CONV_W, SSD_CONV_DIM), CONV_W ** -0.5),
        c_conv_b=nrm((N_C, SSD_CONV_DIM), 0.01),
        c_dt_bias=c_dt_bias,
        c_a_log=c_a_log,
        c_d=1.0 + nrm((N_C, SSD_HEADS), 0.01),
        c_norm_w=1.0 + nrm((N_C, SSD_INNER), 0.01),
        c_out_w=nrm((N_C, SSD_INNER, D_MODEL), SSD_INNER ** -0.5 * DN_BETA),
    )


def reference(x_prompt, x_sample, cache_k, cache_v, page_table, state_lru_conv, state_lru_h,
              state_ssd_conv, state_ssd_h, ln_g, ln_b, a_in_w, a_conv_w, a_conv_b, a_gate_r_w,
              a_gate_r_b, a_gate_i_w, a_gate_i_b, a_lambda, a_out_w, b_in_w, b_lambda, b_subln_w,
              b_out_w, c_in_w, c_conv_w, c_conv_b, c_dt_bias, c_a_log, c_d, c_norm_w, c_out_w):
    xp, xs = x_prompt, x_sample
    bp, bs = xp.shape[0], xs.shape[0]
    k_p, v_p, k_s, v_s = [], [], [], []
    lc_p, lh_p, lc_s, lh_s = [], [], [], []
    sc_p, sh_p, sc_s, sh_s = [], [], [], []
    for i in range(DEPTH):
        j = i // N_MIXERS
        kind = i % N_MIXERS
        if kind == 0:
            w = (a_in_w[j], a_conv_w[j], a_conv_b[j], a_gate_r_w[j], a_gate_r_b[j],
                 a_gate_i_w[j], a_gate_i_b[j], a_lambda[j], a_out_w[j])
            zc = jnp.zeros((bp, CONV_W - 1, LRU_W), xp.dtype)
            zh = jnp.zeros((bp, LRU_W), xp.dtype)
            fp, c1, h1 = rglru_mixer(xp, zc, zh, *w)
            fs, c2, h2 = rglru_mixer(xs, state_lru_conv[j], state_lru_h[j], *w)
            lc_p.append(c1); lh_p.append(h1); lc_s.append(c2); lh_s.append(h2)
        elif kind == 1:
            lam_init = 0.8 - 0.6 * math.exp(-0.3 * i)
            lam = diff_lambda(b_lambda[j], lam_init)
            q, k, v, g = diff_attn_project(xp, b_in_w[j])
            fp = diff_attn_output(diff_attn_prompt(q, k, v, lam), g, lam_init, b_subln_w[j], b_out_w[j])
            k_p.append(k); v_p.append(v)
            q, k, v, g = diff_attn_project(xs, b_in_w[j])
            k_past = cache_k[j, page_table].reshape(bs, -1, ATT_HEADS, 2, ATT_DH)
            v_past = cache_v[j, page_table].reshape(bs, -1, ATT_HEADS, 2 * ATT_DH)
            o = diff_attn_sample(q, k, v, k_past.astype(q.dtype), v_past.astype(v.dtype), lam)
            fs = diff_attn_output(o, g, lam_init, b_subln_w[j], b_out_w[j])
            k_s.append(k); v_s.append(v)
        else:
            w = (c_in_w[j], c_conv_w[j], c_conv_b[j], c_dt_bias[j], c_a_log[j], c_d[j], c_norm_w[j], c_out_w[j])
            zc = jnp.zeros((bp, CONV_W - 1, SSD_CONV_DIM), xp.dtype)
            zh = jnp.zeros((bp, SSD_HEADS, SSD_HEADDIM, SSD_STATE), xp.dtype)
            fp, c1, h1 = ssd_mixer(xp, zc, zh, *w)
            fs, c2, h2 = ssd_mixer(xs, state_ssd_conv[j], state_ssd_h[j], *w)
            sc_p.append(c1); sh_p.append(h1); sc_s.append(c2); sh_s.append(h2)
        xp = layer_norm(DN_ALPHA * xp + fp, ln_g[i], ln_b[i])
        xs = layer_norm(DN_ALPHA * xs + fs, ln_g[i], ln_b[i])
    return (xp, xs, jnp.stack(k_p), jnp.stack(v_p), jnp.stack(k_s), jnp.stack(v_s),
            jnp.stack(lc_p), jnp.stack(lh_p), jnp.stack(lc_s), jnp.stack(lh_s),
            jnp.stack(sc_p), jnp.stack(sh_p), jnp.stack(sc_s), jnp.stack(sh_s))
```

```python
import functools
import math

import jax
import jax.numpy as jnp
from jax import lax
from jax.experimental import pallas as pl
from jax.experimental.pallas import tpu as pltpu

F32 = jnp.float32
BF16 = jnp.bfloat16

D_MODEL = 1024
DEPTH = 4
PAGE_SIZE = 128
N_MIXERS = 3
DN_ALPHA = (2.0 * DEPTH) ** 0.25
EPS = 1e-5
CONV_W = 4
LRU_W = D_MODEL
LRU_BLOCKS = 16
LRU_BS = LRU_W // LRU_BLOCKS
LRU_C = 8.0
ATT_HEADS = 8
ATT_DH = D_MODEL // (2 * ATT_HEADS)
ATT_W = ATT_HEADS * 2 * ATT_DH
SSD_INNER = 2 * D_MODEL
SSD_HEADDIM = 64
SSD_HEADS = SSD_INNER // SSD_HEADDIM
SSD_GROUPS = 8
SSD_HPG = SSD_HEADS // SSD_GROUPS
SSD_STATE = 128
SSD_CONV_DIM = SSD_INNER + 2 * SSD_GROUPS * SSD_STATE
SSD_CHUNK = 128

LANES = 128
SUBLANES = 8
MXU_DIM = 256
VMEM_LIMIT = 56 * 1024 * 1024
NEG_BIG = -1e30
HIST = SUBLANES


def _cparams(n_grid):
    return pltpu.CompilerParams(dimension_semantics=("arbitrary",) * n_grid,
                                vmem_limit_bytes=VMEM_LIMIT)


def _nt(a, b):
    return lax.dot_general(a, b, (((1,), (1,)), ((), ())), preferred_element_type=F32)


def _sigmoid(x):
    return 1.0 / (1.0 + jnp.exp(-x))


def _silu(x):
    return x * _sigmoid(x)


def _softplus(x):
    return jnp.maximum(x, 0.0) + jnp.log(1.0 + jnp.exp(-jnp.abs(x)))


def _proj_kernel(x_ref, w_ref, *out_refs, splits):
    x = x_ref[...].astype(BF16)
    for (off, width), o_ref in zip(splits, out_refs):
        o_ref[...] = jnp.dot(x, w_ref[:, off:off + width], preferred_element_type=F32)


def proj_split(x2d, w_bf16, splits, tm):
    rows, kdim = x2d.shape
    ndim = w_bf16.shape[1]
    return pl.pallas_call(
        functools.partial(_proj_kernel, splits=splits),
        grid=(rows // tm,),
        in_specs=[pl.BlockSpec((tm, kdim), lambda i: (i, 0)),
                  pl.BlockSpec((kdim, ndim), lambda i: (0, 0))],
        out_specs=[pl.BlockSpec((tm, wd), lambda i: (i, 0)) for _, wd in splits],
        out_shape=[jax.ShapeDtypeStruct((rows, wd), F32) for _, wd in splits],
        compiler_params=_cparams(1),
        name="proj_split",
    )(x2d, w_bf16)


def _outproj_ln_kernel(y_ref, w_ref, x_ref, g_ref, b_ref, o_ref):
    f = jnp.dot(y_ref[...].astype(BF16), w_ref[...], preferred_element_type=F32)
    v = DN_ALPHA * x_ref[...] + f
    mu = jnp.mean(v, axis=-1, keepdims=True)
    d = v - mu
    var = jnp.mean(d * d, axis=-1, keepdims=True)
    o_ref[...] = d * lax.rsqrt(var + EPS) * g_ref[...] + b_ref[...]


def outproj_ln(y2d, w_bf16, x2d, g, b, tm):
    rows, kdim = y2d.shape
    return pl.pallas_call(
        _outproj_ln_kernel,
        grid=(rows // tm,),
        in_specs=[pl.BlockSpec((tm, kdim), lambda i: (i, 0)),
                  pl.BlockSpec((kdim, D_MODEL), lambda i: (0, 0)),
                  pl.BlockSpec((tm, D_MODEL), lambda i: (i, 0)),
                  pl.BlockSpec((1, D_MODEL), lambda i: (0, 0)),
                  pl.BlockSpec((1, D_MODEL), lambda i: (0, 0))],
        out_specs=pl.BlockSpec((tm, D_MODEL), lambda i: (i, 0)),
        out_shape=jax.ShapeDtypeStruct((rows, D_MODEL), F32),
        compiler_params=_cparams(1),
        name="outproj_ln",
    )(y2d, w_bf16, x2d, g.reshape(1, D_MODEL), b.reshape(1, D_MODEL))


def _rglru_kernel(gate_ref, u_ref, buf_ref, h0_ref, cw_ref, cb_ref, wr_ref, rb_ref,
                  wi_ref, ib_ref, lam_ref, hg_ref, nbuf_ref, hlast_ref,
                  ubuf, a_s, b_s, hcar, *, tt, valid):
    t = pl.program_id(1)
    n_t = pl.num_programs(1)

    @pl.when(t == 0)
    def _():
        ubuf[HIST - 3:HIST, :] = buf_ref[0]
        hcar[...] = h0_ref[0]
        if valid < tt:
            ubuf[HIST + valid:HIST + tt, :] = jnp.zeros((tt - valid, LRU_W), F32)

    ubuf[HIST:HIST + valid, :] = u_ref[0]
    cw = cw_ref[...]
    y = cb_ref[...] + cw[3:4, :] * ubuf[HIST:HIST + tt, :]
    for k in range(1, CONV_W):
        y = y + cw[3 - k:4 - k, :] * ubuf[HIST - k:HIST - k + tt, :]

    sp = _softplus(-lam_ref[...])
    yb = y.astype(BF16)
    for j in range(LRU_W // MXU_DIM):
        sl = slice(j * MXU_DIM, (j + 1) * MXU_DIM)
        ys = yb[:, sl]
        r = _sigmoid(jnp.dot(ys, wr_ref[j], preferred_element_type=F32) + rb_ref[:, sl])
        ig = _sigmoid(jnp.dot(ys, wi_ref[j], preferred_element_type=F32) + ib_ref[:, sl])
        log_a = -LRU_C * r * sp[:, sl]
        a = jnp.exp(log_a)
        a_s[:, sl] = a
        b_s[:, sl] = jnp.sqrt(-jnp.tanh(log_a) * (a * a + 1.0)) * (ig * y[:, sl])

    row = lax.broadcasted_iota(jnp.int32, (SUBLANES, LRU_W), 0)

    def body(g, hc):
        r0 = pl.multiple_of(g * SUBLANES, SUBLANES)
        a = a_s[pl.ds(r0, SUBLANES), :]
        b = b_s[pl.ds(r0, SUBLANES), :]
        for d in (1, 2, 4):
            a_sh = pltpu.roll(a, d, 0)
            b_sh = pltpu.roll(b, d, 0)
            m = row >= d
            b = jnp.where(m, a * b_sh + b, b)
            a = jnp.where(m, a * a_sh, a)
        h = a * hc + b
        b_s[pl.ds(r0, SUBLANES), :] = h
        return h[SUBLANES - 1:SUBLANES, :]

    lax.fori_loop(0, tt // SUBLANES, body, hcar[...])
    h_end = b_s[valid - 1:valid, :]
    hcar[...] = h_end
    hg_ref[0] = b_s[0:valid, :] * _silu(gate_ref[0])

    @pl.when(t == n_t - 1)
    def _():
        nbuf_ref[0] = ubuf[HIST + valid - 3:HIST + valid, :]
        hlast_ref[0] = h_end

    ubuf[HIST - 3:HIST, :] = ubuf[HIST + tt - 3:HIST + tt, :]


def rglru_core(gate, u, conv_buf, h0, conv_w, conv_b, wr_bd, r_b, wi_bd, i_b, lam):
    bsz, t_len, _ = u.shape
    if t_len % 256 == 0:
        tt, valid = 256, 256
    else:
        tt, valid = 16, t_len
    n_t = max(t_len // tt, 1)
    row = lambda v: v.reshape(1, LRU_W)
    vec = lambda: pl.BlockSpec((1, LRU_W), lambda b, t: (0, 0))
    wspec = lambda: pl.BlockSpec((LRU_W // MXU_DIM, MXU_DIM, MXU_DIM), lambda b, t: (0, 0, 0))
    hg, nbuf, hlast = pl.pallas_call(
        functools.partial(_rglru_kernel, tt=tt, valid=valid),
        grid=(bsz, n_t),
        in_specs=[pl.BlockSpec((1, valid, LRU_W), lambda b, t: (b, t, 0)),
                  pl.BlockSpec((1, valid, LRU_W), lambda b, t: (b, t, 0)),
                  pl.BlockSpec((1, CONV_W - 1, LRU_W), lambda b, t: (b, 0, 0)),
                  pl.BlockSpec((1, 1, LRU_W), lambda b, t: (b, 0, 0)),
                  pl.BlockSpec((CONV_W, LRU_W), lambda b, t: (0, 0)),
                  vec(), wspec(), vec(), wspec(), vec(), vec()],
        out_specs=[pl.BlockSpec((1, valid, LRU_W), lambda b, t: (b, t, 0)),
                   pl.BlockSpec((1, CONV_W - 1, LRU_W), lambda b, t: (b, 0, 0)),
                   pl.BlockSpec((1, 1, LRU_W), lambda b, t: (b, 0, 0))],
        out_shape=[jax.ShapeDtypeStruct((bsz, t_len, LRU_W), F32),
                   jax.ShapeDtypeStruct((bsz, CONV_W - 1, LRU_W), F32),
                   jax.ShapeDtypeStruct((bsz, 1, LRU_W), F32)],
        scratch_shapes=[pltpu.VMEM((HIST + tt, LRU_W), F32),
                        pltpu.VMEM((tt, LRU_W), F32),
                        pltpu.VMEM((tt, LRU_W), F32),
                        pltpu.VMEM((1, LRU_W), F32)],
        compiler_params=_cparams(2),
        name="rglru_core",
    )(gate, u, conv_buf, h0.reshape(bsz, 1, LRU_W), conv_w, row(conv_b), wr_bd, row(r_b),
      wi_bd, row(i_b), row(lam))
    return hg, nbuf, hlast.reshape(bsz, LRU_W)


def _block_diag_tiles(w):
    per = MXU_DIM // LRU_BS
    w4 = w.reshape(LRU_W // MXU_DIM, per, LRU_BS, LRU_BS)
    eye = jnp.eye(per, dtype=w.dtype)
    t = jnp.einsum("jakc,ab->jakbc", w4, eye)
    return t.reshape(LRU_W // MXU_DIM, MXU_DIM, MXU_DIM).astype(BF16)


def _diff_lambda(lp, lam_init):
    s1 = jnp.sum(lp[0:1, :] * lp[1:2, :], axis=-1, keepdims=True)
    s2 = jnp.sum(lp[2:3, :] * lp[3:4, :], axis=-1, keepdims=True)
    return jnp.exp(s1) - jnp.exp(s2) + lam_init


def _subln_gate(o, sw, gate, lam_init):
    ms = jnp.mean(o * o, axis=-1, keepdims=True)
    return (o * lax.rsqrt(ms + EPS) * sw * (1.0 - lam_init)) * _silu(gate)


def _attn_prompt_kernel(lp_ref, q_ref, k_ref, v_ref, gate_ref, sw_ref, o_ref,
                        m_s, l_s, acc_s, *, tq, lam_init):
    qi = pl.program_id(2)
    lam = _diff_lambda(lp_ref[...], lam_init)
    q = q_ref[0] * (ATT_DH ** -0.5)
    lane = lax.broadcasted_iota(jnp.int32, (tq, 2 * ATT_DH), 1)
    q_maps = (jnp.where(lane < ATT_DH, q, 0.0).astype(BF16),
              jnp.where(lane >= ATT_DH, q, 0.0).astype(BF16))
    m_s[...] = jnp.full(m_s.shape, NEG_BIG, F32)
    l_s[...] = jnp.zeros(l_s.shape, F32)
    acc_s[...] = jnp.zeros(acc_s.shape, F32)
    rowi = lax.broadcasted_iota(jnp.int32, (tq, tq), 0)
    coli = lax.broadcasted_iota(jnp.int32, (tq, tq), 1)

    def step(ki, masked):
        ks = pl.multiple_of(ki * tq, tq)
        kb = k_ref[0, pl.ds(ks, tq), :].astype(BF16)
        vb = v_ref[0, pl.ds(ks, tq), :].astype(BF16)
        for c in range(2):
            s = _nt(q_maps[c], kb)
            if masked:
                s = jnp.where(coli <= rowi, s, NEG_BIG)
            m_prev = m_s[c]
            m_new = jnp.maximum(m_prev, jnp.max(s, axis=-1, keepdims=True))
            alpha = jnp.exp(m_prev - m_new)
            p = jnp.exp(s - m_new)
            l_s[c] = alpha * l_s[c] + jnp.sum(p, axis=-1, keepdims=True)
            acc_s[c] = alpha * acc_s[c] + jnp.dot(p.astype(BF16), vb, preferred_element_type=F32)
            m_s[c] = m_new

    def body(ki, carry):
        step(ki, False)
        return carry

    lax.fori_loop(0, qi, body, 0)
    step(qi, True)
    o = acc_s[0] / l_s[0] - lam * (acc_s[1] / l_s[1])
    o_ref[0] = _subln_gate(o, sw_ref[...], gate_ref[0], lam_init)


def attn_prompt(q, k, v, gate, lam_p, subln_w, lam_init, tq=256):
    bsz, t_len, _ = q.shape
    hd = 2 * ATT_DH
    qspec = lambda: pl.BlockSpec((1, tq, hd), lambda b, h, i: (b, i, h))
    kvspec = lambda: pl.BlockSpec((1, t_len, hd), lambda b, h, i: (b, 0, h))
    return pl.pallas_call(
        functools.partial(_attn_prompt_kernel, tq=tq, lam_init=lam_init),
        grid=(bsz, ATT_HEADS, t_len // tq),
        in_specs=[pl.BlockSpec((4, ATT_DH), lambda b, h, i: (0, 0)),
                  qspec(), kvspec(), kvspec(), qspec(),
                  pl.BlockSpec((1, hd), lambda b, h, i: (0, 0))],
        out_specs=qspec(),
        out_shape=jax.ShapeDtypeStruct((bsz, t_len, ATT_W), F32),
        scratch_shapes=[pltpu.VMEM((2, tq, 1), F32), pltpu.VMEM((2, tq, 1), F32),
                        pltpu.VMEM((2, tq, hd), F32)],
        compiler_params=_cparams(3),
        name="attn_prompt",
    )(lam_p, q, k, v, gate, subln_w.reshape(1, hd))


def _attn_sample_kernel(pt_ref, lp_ref, qbd_ref, *refs, n_pg, t_new, lam_init):
    k_refs = refs[:n_pg]
    v_refs = refs[n_pg:2 * n_pg]
    knew_ref, vnew_ref, gate_ref, sw_ref, o_ref, m_s, l_s, acc_s = refs[2 * n_pg:]
    j = pl.program_id(1)
    n_j = pl.num_programs(1)
    n_rows = ATT_HEADS * 2 * t_new
    hd = 2 * ATT_DH

    @pl.when(j == 0)
    def _():
        m_s[...] = jnp.full(m_s.shape, NEG_BIG, F32)
        l_s[...] = jnp.zeros(l_s.shape, F32)
        acc_s[...] = jnp.zeros(acc_s.shape, F32)

    qbd = qbd_ref[0]

    def attend(kb, vb, mask):
        s = _nt(qbd, kb.astype(BF16))
        if mask is not None:
            s = jnp.where(mask, s, NEG_BIG)
        m_prev = m_s[...]
        m_new = jnp.maximum(m_prev, jnp.max(s, axis=-1, keepdims=True))
        alpha = jnp.exp(m_prev - m_new)
        p = jnp.exp(s - m_new)
        l_s[...] = alpha * l_s[...] + jnp.sum(p, axis=-1, keepdims=True)
        acc_s[...] = alpha * acc_s[...] + jnp.dot(p.astype(BF16), vb.astype(BF16),
                                                  preferred_element_type=F32)
        m_s[...] = m_new

    for i in range(n_pg):
        attend(k_refs[i][0], v_refs[i][0], None)

    @pl.when(j == n_j - 1)
    def _():
        rowi = lax.broadcasted_iota(jnp.int32, (n_rows, PAGE_SIZE), 0)
        coli = lax.broadcasted_iota(jnp.int32, (n_rows, PAGE_SIZE), 1)
        attend(knew_ref[0], vnew_ref[0], coli <= (rowi % t_new))
        lam = _diff_lambda(lp_ref[...], lam_init)
        sw = sw_ref[...]
        for h in range(ATT_HEADS):
            r1 = h * 2 * t_new
            r2 = r1 + t_new
            cs = slice(h * hd, (h + 1) * hd)
            o1 = acc_s[r1:r1 + t_new, cs] / l_s[r1:r1 + t_new, :]
            o2 = acc_s[r2:r2 + t_new, cs] / l_s[r2:r2 + t_new, :]
            o_ref[0, :, cs] = _subln_gate(o1 - lam * o2, sw, gate_ref[0, :, cs], lam_init)


def attn_sample(q, k, v, gate, cache_k, cache_v, page_table, lam_p, subln_w, lam_init, n_pg=4):
    bsz, t_new, _ = q.shape
    n_pages = page_table.shape[1]
    n_pool = cache_k.shape[0]
    hd = 2 * ATT_DH
    n_rows = ATT_HEADS * 2 * t_new
    q4 = (q * (ATT_DH ** -0.5)).reshape(bsz, t_new, 2 * ATT_HEADS, ATT_DH)
    eye = jnp.eye(2 * ATT_HEADS, dtype=F32)
    qbd = jnp.einsum("bqhd,hg->bhqgd", q4, eye).reshape(bsz, n_rows, ATT_W).astype(BF16)
    pad = ((0, 0), (0, PAGE_SIZE - t_new), (0, 0))
    k_new = jnp.pad(k, pad)
    v_new = jnp.pad(v, pad)
    ck = cache_k.reshape(n_pool, PAGE_SIZE, ATT_W)
    cv = cache_v.reshape(n_pool, PAGE_SIZE, ATT_W)
    pt = page_table.reshape(-1)

    def page_spec(i):
        return pl.BlockSpec((1, PAGE_SIZE, ATT_W),
                            lambda b, j, pt_ref: (pt_ref[b * n_pages + j * n_pg + i], 0, 0))

    per_b = lambda r: pl.BlockSpec((1, r, ATT_W), lambda b, j, pt_ref: (b, 0, 0))
    grid_spec = pltpu.PrefetchScalarGridSpec(
        num_scalar_prefetch=1,
        grid=(bsz, n_pages // n_pg),
        in_specs=([pl.BlockSpec((4, ATT_DH), lambda b, j, pt_ref: (0, 0)), per_b(n_rows)]
                  + [page_spec(i) for i in range(n_pg)]
                  + [page_spec(i) for i in range(n_pg)]
                  + [per_b(PAGE_SIZE), per_b(PAGE_SIZE), per_b(t_new),
                     pl.BlockSpec((1, hd), lambda b, j, pt_ref: (0, 0))]),
        out_specs=per_b(t_new),
        scratch_shapes=[pltpu.VMEM((n_rows, 1), F32), pltpu.VMEM((n_rows, 1), F32),
                        pltpu.VMEM((n_rows, ATT_W), F32)],
    )
    return pl.pallas_call(
        functools.partial(_attn_sample_kernel, n_pg=n_pg, t_new=t_new, lam_init=lam_init),
        grid_spec=grid_spec,
        out_shape=jax.ShapeDtypeStruct((bsz, t_new, ATT_W), F32),
        compiler_params=_cparams(2),
        name="attn_sample",
    )(pt, lam_p, qbd, *([ck] * n_pg), *([cv] * n_pg), k_new, v_new, gate,
      subln_w.reshape(1, hd))


def _ssd_kernel(xbc_ref, z_ref, dtr_ref, buf_ref, h0_ref, cw_ref, cb_ref, dtb_ref, alog_ref,
                dexp_ref, nw_ref, yn_ref, nbuf_ref, hout_ref, ubuf, dt_s, y_s, *, vl):
    L = SSD_CHUNK
    c = pl.program_id(1)
    n_c = pl.num_programs(1)
    gw = SSD_HPG * SSD_HEADDIM

    @pl.when(c == 0)
    def _():
        ubuf[HIST - 3:HIST, :] = buf_ref[0]
        hout_ref[0] = h0_ref[0]
        if vl < L:
            ubuf[HIST + vl:HIST + L, :] = jnp.zeros((L - vl, SSD_CONV_DIM), F32)
            dt_s[...] = jnp.zeros((L, LANES), F32)

    ubuf[HIST:HIST + vl, :] = xbc_ref[0]
    dt_s[0:vl, :] = dtr_ref[0]

    rowL = lax.broadcasted_iota(jnp.int32, (L, LANES), 0)
    laneL = lax.broadcasted_iota(jnp.int32, (L, LANES), 1)
    dt = _softplus(dt_s[...] + dtb_ref[...])
    dt = jnp.where((rowL < vl) & (laneL < SSD_HEADS), dt, 0.0)
    adt = dt * (-jnp.exp(alog_ref[...]))
    tri = (lax.broadcasted_iota(jnp.int32, (L, L), 0)
           >= lax.broadcasted_iota(jnp.int32, (L, L), 1))
    cs = jnp.dot(tri.astype(F32), adt, preferred_element_type=F32,
                 precision=lax.Precision.HIGHEST)
    cs_last = cs[L - 1:L, :]
    w1 = dt * jnp.exp(cs_last - cs)
    cs_t = cs.T
    dt_t = dt.T
    w1_t = w1.T

    cw = cw_ref[...]
    lane_g = lax.broadcasted_iota(jnp.int32, (L, gw), 1)

    def conv_silu(lo, width):
        sl = slice(lo, lo + width)
        y = cb_ref[:, sl] + cw[3:4, sl] * ubuf[HIST:HIST + L, sl]
        for k in range(1, CONV_W):
            y = y + cw[3 - k:4 - k, sl] * ubuf[HIST - k:HIST - k + L, sl]
        return _silu(y)

    for g in range(SSD_GROUPS):
        xh = conv_silu(g * gw, gw)
        bm = conv_silu(SSD_INNER + g * SSD_STATE, SSD_STATE)
        cm = conv_silu(SSD_INNER + (SSD_GROUPS + g) * SSD_STATE, SSD_STATE)
        bmb = bm.astype(BF16)
        cmb = cm.astype(BF16)
        xhb = xh.astype(BF16)
        cbm = _nt(cmb, bmb)
        y_diag = jnp.zeros((L, gw), F32)
        e_cols = []
        w_rows = []
        d_rows = []
        for r in range(SSD_HPG):
            h = g * SSD_HPG + r
            cs_col = cs[:, h:h + 1]
            cs_row = cs_t[h:h + 1, :]
            lm = jnp.exp(jnp.where(tri, cs_col - cs_row, NEG_BIG))
            mat = (cbm * lm * dt_t[h:h + 1, :]).astype(BF16)
            in_head = (lane_g >= r * SSD_HEADDIM) & (lane_g < (r + 1) * SSD_HEADDIM)
            xr = jnp.where(in_head, xhb, jnp.zeros_like(xhb))
            y_diag = y_diag + jnp.dot(mat, xr, preferred_element_type=F32)
            e_cols.append(jnp.broadcast_to(jnp.exp(cs_col), (L, SSD_HEADDIM)))
            w_rows.append(jnp.broadcast_to(w1_t[h:h + 1, :], (SSD_HEADDIM, L)))
            d_rows.append(jnp.broadcast_to(jnp.exp(cs_t[h:h + 1, L - 1:L]),
                                           (SSD_HEADDIM, SSD_STATE)))
        hg = hout_ref[0, g]
        y_off = _nt(cmb, hg.astype(BF16)) * jnp.concatenate(e_cols, axis=1)
        y_s[:, g * gw:(g + 1) * gw] = y_diag + y_off + dexp_ref[:, g * gw:(g + 1) * gw] * xh
        xd_t = (xh.T * jnp.concatenate(w_rows, axis=0)).astype(BF16)
        states = jnp.dot(xd_t, bmb, preferred_element_type=F32)
        hout_ref[0, g] = hg * jnp.concatenate(d_rows, axis=0) + states

    gated = y_s[0:vl, :] * _silu(z_ref[0])
    ms = jnp.mean(gated * gated, axis=-1, keepdims=True)
    yn_ref[0] = gated * lax.rsqrt(ms + EPS) * nw_ref[...]

    @pl.when(c == n_c - 1)
    def _():
        nbuf_ref[0] = ubuf[HIST + vl - 3:HIST + vl, :]

    ubuf[HIST - 3:HIST, :] = ubuf[HIST + L - 3:HIST + L, :]


def ssd_core(xbc, z, dt_raw, conv_buf, h0, conv_w, conv_b, dt_bias, a_log, d_skip, norm_w):
    bsz, t_len, _ = xbc.shape
    L = SSD_CHUNK
    vl = L if t_len % L == 0 else t_len
    n_c = max(t_len // L, 1)
    gw = SSD_HPG * SSD_HEADDIM
    pad_lanes = lambda v: jnp.pad(v.reshape(1, SSD_HEADS), ((0, 0), (0, LANES - SSD_HEADS)))
    dexp = jnp.repeat(d_skip, SSD_HEADDIM).reshape(1, SSD_INNER)
    h0g = h0.reshape(bsz, SSD_GROUPS, gw, SSD_STATE)
    const = lambda shape: pl.BlockSpec(shape, lambda b, c: (0,) * len(shape))
    yn, nbuf, hout = pl.pallas_call(
        functools.partial(_ssd_kernel, vl=vl),
        grid=(bsz, n_c),
        in_specs=[pl.BlockSpec((1, vl, SSD_CONV_DIM), lambda b, c: (b, c, 0)),
                  pl.BlockSpec((1, vl, SSD_INNER), lambda b, c: (b, c, 0)),
                  pl.BlockSpec((1, vl, LANES), lambda b, c: (b, c, 0)),
                  pl.BlockSpec((1, CONV_W - 1, SSD_CONV_DIM), lambda b, c: (b, 0, 0)),
                  pl.BlockSpec((1, SSD_GROUPS, gw, SSD_STATE), lambda b, c: (b, 0, 0, 0)),
                  const((CONV_W, SSD_CONV_DIM)), const((1, SSD_CONV_DIM)),
                  const((1, LANES)), const((1, LANES)),
                  const((1, SSD_INNER)), const((1, SSD_INNER))],
        out_specs=[pl.BlockSpec((1, vl, SSD_INNER), lambda b, c: (b, c, 0)),
                   pl.BlockSpec((1, CONV_W - 1, SSD_CONV_DIM), lambda b, c: (b, 0, 0)),
                   pl.BlockSpec((1, SSD_GROUPS, gw, SSD_STATE), lambda b, c: (b, 0, 0, 0))],
        out_shape=[jax.ShapeDtypeStruct((bsz, t_len, SSD_INNER), F32),
                   jax.ShapeDtypeStruct((bsz, CONV_W - 1, SSD_CONV_DIM), F32),
                   jax.ShapeDtypeStruct((bsz, SSD_GROUPS, gw, SSD_STATE), F32)],
        scratch_shapes=[pltpu.VMEM((HIST + L, SSD_CONV_DIM), F32),
                        pltpu.VMEM((L, LANES), F32),
                        pltpu.VMEM((L, SSD_INNER), F32)],
        compiler_params=_cparams(2),
        name="ssd_core",
    )(xbc, z, dt_raw, conv_buf, h0g, conv_w, conv_b.reshape(1, SSD_CONV_DIM),
      pad_lanes(dt_bias), pad_lanes(a_log), dexp, norm_w.reshape(1, SSD_INNER))
    return yn, nbuf, hout.reshape(bsz, SSD_HEADS, SSD_HEADDIM, SSD_STATE)


def _row_tile(rows):
    return 256 if rows % 256 == 0 else rows


def _rglru_layer(x, conv_buf, h0, w, ln_g, ln_b):
    in_w, conv_w, conv_b, wr_bd, r_b, wi_bd, i_b, lam, out_w = w
    bsz, t_len, _ = x.shape
    x2 = x.reshape(bsz * t_len, D_MODEL)
    tm = _row_tile(bsz * t_len)
    gate, u = proj_split(x2, in_w, ((0, LRU_W), (LRU_W, LRU_W)), tm)
    hg, nbuf, hlast = rglru_core(gate.reshape(bsz, t_len, LRU_W), u.reshape(bsz, t_len, LRU_W),
                                 conv_buf, h0, conv_w, conv_b, wr_bd, r_b, wi_bd, i_b, lam)
    xn = outproj_ln(hg.reshape(bsz * t_len, LRU_W), out_w, x2, ln_g, ln_b, tm)
    return xn.reshape(bsz, t_len, D_MODEL), nbuf, hlast


def _attn_project(x, in_w):
    bsz, t_len, _ = x.shape
    x2 = x.reshape(bsz * t_len, D_MODEL)
    tm = _row_tile(bsz * t_len)
    splits = tuple((i * ATT_W, ATT_W) for i in range(4))
    q, k, v, gate = proj_split(x2, in_w, splits, tm)
    shp = (bsz, t_len, ATT_W)
    return x2, tm, q.reshape(shp), k.reshape(shp), v.reshape(shp), gate.reshape(shp)


def _ssd_layer(x, conv_buf, h0, w, ln_g, ln_b):
    in_w_pad, conv_w, conv_b, dt_bias, a_log, d_skip, norm_w, out_w = w
    bsz, t_len, _ = x.shape
    x2 = x.reshape(bsz * t_len, D_MODEL)
    tm = _row_tile(bsz * t_len)
    splits = ((0, SSD_INNER), (SSD_INNER, SSD_CONV_DIM), (SSD_INNER + SSD_CONV_DIM, LANES))
    z, xbc, dt_raw = proj_split(x2, in_w_pad, splits, tm)
    yn, nbuf, hlast = ssd_core(xbc.reshape(bsz, t_len, SSD_CONV_DIM),
                               z.reshape(bsz, t_len, SSD_INNER),
                               dt_raw.reshape(bsz, t_len, LANES),
                               conv_buf, h0, conv_w, conv_b, dt_bias, a_log, d_skip, norm_w)
    xn = outproj_ln(yn.reshape(bsz * t_len, SSD_INNER), out_w, x2, ln_g, ln_b, tm)
    return xn.reshape(bsz, t_len, D_MODEL), nbuf, hlast


def kernel(x_prompt, x_sample, cache_k, cache_v, page_table, state_lru_conv, state_lru_h, state_ssd_conv, state_ssd_h, ln_g, ln_b, a_in_w, a_conv_w, a_conv_b, a_gate_r_w, a_gate_r_b, a_gate_i_w, a_gate_i_b, a_lambda, a_out_w, b_in_w, b_lambda, b_subln_w, b_out_w, c_in_w, c_conv_w, c_conv_b, c_dt_bias, c_a_log, c_d, c_norm_w, c_out_w):
    xp, xs = x_prompt, x_sample
    bp, bs = xp.shape[0], xs.shape[0]
    tp, ts = xp.shape[1], xs.shape[1]
    k_p, v_p, k_s, v_s = [], [], [], []
    lc_p, lh_p, lc_s, lh_s = [], [], [], []
    sc_p, sh_p, sc_s, sh_s = [], [], [], []
    for i in range(DEPTH):
        j = i // N_MIXERS
        kind = i % N_MIXERS
        if kind == 0:
            w = (a_in_w[j].astype(BF16), a_conv_w[j], a_conv_b[j], _block_diag_tiles(a_gate_r_w[j]),
                 a_gate_r_b[j], _block_diag_tiles(a_gate_i_w[j]), a_gate_i_b[j], a_lambda[j],
                 a_out_w[j].astype(BF16))
            zc = jnp.zeros((bp, CONV_W - 1, LRU_W), F32)
            zh = jnp.zeros((bp, LRU_W), F32)
            xp, c1, h1 = _rglru_layer(xp, zc, zh, w, ln_g[i], ln_b[i])
            xs, c2, h2 = _rglru_layer(xs, state_lru_conv[j], state_lru_h[j], w, ln_g[i], ln_b[i])
            lc_p.append(c1); lh_p.append(h1); lc_s.append(c2); lh_s.append(h2)
        elif kind == 1:
            lam_init = 0.8 - 0.6 * math.exp(-0.3 * i)
            in_w = b_in_w[j].astype(BF16)
            out_w = b_out_w[j].astype(BF16)
            x2, tm, q, k, v, g = _attn_project(xp, in_w)
            og = attn_prompt(q, k, v, g, b_lambda[j], b_subln_w[j], lam_init)
            xp = outproj_ln(og.reshape(bp * tp, ATT_W), out_w, x2, ln_g[i], ln_b[i], tm
                            ).reshape(bp, tp, D_MODEL)
            k_p.append(k.reshape(bp, tp, ATT_HEADS, 2, ATT_DH))
            v_p.append(v.reshape(bp, tp, ATT_HEADS, 2 * ATT_DH))
            x2, tm, q, k, v, g = _attn_project(xs, in_w)
            og = attn_sample(q, k, v, g, cache_k[j], cache_v[j], page_table, b_lambda[j],
                             b_subln_w[j], lam_init)
            xs = outproj_ln(og.reshape(bs * ts, ATT_W), out_w, x2, ln_g[i], ln_b[i], tm
                            ).reshape(bs, ts, D_MODEL)
            k_s.append(k.reshape(bs, ts, ATT_HEADS, 2, ATT_DH))
            v_s.append(v.reshape(bs, ts, ATT_HEADS, 2 * ATT_DH))
        else:
            in_w_pad = jnp.pad(c_in_w[j], ((0, 0), (0, LANES - SSD_HEADS))).astype(BF16)
            w = (in_w_pad, c_conv_w[j], c_conv_b[j], c_dt_bias[j], c_a_log[j], c_d[j], c_norm_w[j],
                 c_out_w[j].astype(BF16))
            zc = jnp.zeros((bp, CONV_W - 1, SSD_CONV_DIM), F32)
            zh = jnp.zeros((bp, SSD_HEADS, SSD_HEADDIM, SSD_STATE), F32)
            xp, c1, h1 = _ssd_layer(xp, zc, zh, w, ln_g[i], ln_b[i])
            xs, c2, h2 = _ssd_layer(xs, state_ssd_conv[j], state_ssd_h[j], w, ln_g[i], ln_b[i])
            sc_p.append(c1); sh_p.append(h1); sc_s.append(c2); sh_s.append(h2)
    return (xp, xs, jnp.stack(k_p), jnp.stack(v_p), jnp.stack(k_s), jnp.stack(v_s),
            jnp.stack(lc_p), jnp.stack(lh_p), jnp.stack(lc_s), jnp.stack(lh_s),
            jnp.stack(sc_p), jnp.stack(sh_p), jnp.stack(sc_s), jnp.stack(sh_s))
```

```python
import functools
import math

import jax
import jax.numpy as jnp
from jax import lax
from jax.experimental import pallas as pl
from jax.experimental.pallas import tpu as pltpu

F32 = jnp.float32
BF16 = jnp.bfloat16

D_MODEL = 1024
DEPTH = 4
PAGE_SIZE = 128
N_MIXERS = 3
DN_ALPHA = (2.0 * DEPTH) ** 0.25
EPS = 1e-5
CONV_W = 4
LRU_W = D_MODEL
LRU_BLOCKS = 16
LRU_BS = LRU_W // LRU_BLOCKS
LRU_C = 8.0
ATT_HEADS = 8
ATT_DH = D_MODEL // (2 * ATT_HEADS)
ATT_W = ATT_HEADS * 2 * ATT_DH
SSD_INNER = 2 * D_MODEL
SSD_HEADDIM = 64
SSD_HEADS = SSD_INNER // SSD_HEADDIM
SSD_GROUPS = 8
SSD_HPG = SSD_HEADS // SSD_GROUPS
SSD_STATE = 128
SSD_CONV_DIM = SSD_INNER + 2 * SSD_GROUPS * SSD_STATE
SSD_CHUNK = 128

LANES = 128
SUBLANES = 8
MXU_DIM = 256
VMEM_LIMIT = 56 * 1024 * 1024
LOG2E = math.log2(math.e)
NEG_BIG = -1e30
HIST = SUBLANES


def _cparams(n_grid):
    return pltpu.CompilerParams(dimension_semantics=("arbitrary",) * n_grid,
                                vmem_limit_bytes=VMEM_LIMIT)


def _nt(a, b):
    return lax.dot_general(a, b, (((1,), (1,)), ((), ())), preferred_element_type=F32)


def _sigmoid(x):
    return 1.0 / (1.0 + jnp.exp(-x))


def _silu(x):
    return x * _sigmoid(x)


def _softplus(x):
    return jnp.maximum(x, 0.0) + jnp.log(1.0 + jnp.exp(-jnp.abs(x)))


def _proj_kernel(x_ref, w_ref, *out_refs, splits):
    x = x_ref[...].astype(BF16)
    for (off, width), o_ref in zip(splits, out_refs):
        o_ref[...] = jnp.dot(x, w_ref[:, off:off + width], preferred_element_type=F32)


def proj_split(x2d, w_bf16, splits, tm):
    rows, kdim = x2d.shape
    ndim = w_bf16.shape[1]
    return pl.pallas_call(
        functools.partial(_proj_kernel, splits=splits),
        grid=(rows // tm,),
        in_specs=[pl.BlockSpec((tm, kdim), lambda i: (i, 0)),
                  pl.BlockSpec((kdim, ndim), lambda i: (0, 0))],
        out_specs=[pl.BlockSpec((tm, wd), lambda i: (i, 0)) for _, wd in splits],
        out_shape=[jax.ShapeDtypeStruct((rows, wd), F32) for _, wd in splits],
        compiler_params=_cparams(1),
        name="proj_split",
    )(x2d, w_bf16)


def _outproj_ln_kernel(y_ref, w_ref, x_ref, g_ref, b_ref, o_ref):
    f = jnp.dot(y_ref[...].astype(BF16), w_ref[...], preferred_element_type=F32)
    v = DN_ALPHA * x_ref[...] + f
    mu = jnp.mean(v, axis=-1, keepdims=True)
    d = v - mu
    var = jnp.mean(d * d, axis=-1, keepdims=True)
    o_ref[...] = d * lax.rsqrt(var + EPS) * g_ref[...] + b_ref[...]


def outproj_ln(y2d, w_bf16, x2d, g, b, tm):
    rows, kdim = y2d.shape
    return pl.pallas_call(
        _outproj_ln_kernel,
        grid=(rows // tm,),
        in_specs=[pl.BlockSpec((tm, kdim), lambda i: (i, 0)),
                  pl.BlockSpec((kdim, D_MODEL), lambda i: (0, 0)),
                  pl.BlockSpec((tm, D_MODEL), lambda i: (i, 0)),
                  pl.BlockSpec((1, D_MODEL), lambda i: (0, 0)),
                  pl.BlockSpec((1, D_MODEL), lambda i: (0, 0))],
        out_specs=pl.BlockSpec((tm, D_MODEL), lambda i: (i, 0)),
        out_shape=jax.ShapeDtypeStruct((rows, D_MODEL), F32),
        compiler_params=_cparams(1),
        name="outproj_ln",
    )(y2d, w_bf16, x2d, g.reshape(1, D_MODEL), b.reshape(1, D_MODEL))


def _rglru_kernel(gate_ref, u_ref, buf_ref, h0_ref, cw_ref, cb_ref, wr_ref, rb_ref,
                  wi_ref, ib_ref, lam_ref, hg_ref, nbuf_ref, hlast_ref,
                  ubuf, a_s, b_s, hcar, *, tt, valid):
    t = pl.program_id(1)
    n_t = pl.num_programs(1)

    @pl.when(t == 0)
    def _():
        ubuf[HIST - 3:HIST, :] = buf_ref[0]
        hcar[...] = h0_ref[0]
        if valid < tt:
            ubuf[HIST + valid:HIST + tt, :] = jnp.zeros((tt - valid, LRU_W), F32)

    ubuf[HIST:HIST + valid, :] = u_ref[0]
    cw = cw_ref[...]
    y = cb_ref[...] + cw[3:4, :] * ubuf[HIST:HIST + tt, :]
    for k in range(1, CONV_W):
        y = y + cw[3 - k:4 - k, :] * ubuf[HIST - k:HIST - k + tt, :]

    sp = _softplus(-lam_ref[...])
    yb = y.astype(BF16)
    for j in range(LRU_W // MXU_DIM):
        sl = slice(j * MXU_DIM, (j + 1) * MXU_DIM)
        ys = yb[:, sl]
        r = _sigmoid(jnp.dot(ys, wr_ref[j], preferred_element_type=F32) + rb_ref[:, sl])
        ig = _sigmoid(jnp.dot(ys, wi_ref[j], preferred_element_type=F32) + ib_ref[:, sl])
        log_a = -LRU_C * r * sp[:, sl]
        a = jnp.exp(log_a)
        a_s[:, sl] = a
        b_s[:, sl] = jnp.sqrt(-jnp.tanh(log_a) * (a * a + 1.0)) * (ig * y[:, sl])

    row = lax.broadcasted_iota(jnp.int32, (SUBLANES, LRU_W), 0)

    def body(g, hc):
        r0 = pl.multiple_of(g * SUBLANES, SUBLANES)
        a = a_s[pl.ds(r0, SUBLANES), :]
        b = b_s[pl.ds(r0, SUBLANES), :]
        for d in (1, 2, 4):
            a_sh = pltpu.roll(a, d, 0)
            b_sh = pltpu.roll(b, d, 0)
            m = row >= d
            b = jnp.where(m, a * b_sh + b, b)
            a = jnp.where(m, a * a_sh, a)
        h = a * hc + b
        b_s[pl.ds(r0, SUBLANES), :] = h
        return h[SUBLANES - 1:SUBLANES, :]

    lax.fori_loop(0, tt // SUBLANES, body, hcar[...])
    h_end = b_s[valid - 1:valid, :]
    hcar[...] = h_end
    hg_ref[0] = b_s[0:valid, :] * _silu(gate_ref[0])

    @pl.when(t == n_t - 1)
    def _():
        nbuf_ref[0] = ubuf[HIST + valid - 3:HIST + valid, :]
        hlast_ref[0] = h_end

    ubuf[HIST - 3:HIST, :] = ubuf[HIST + tt - 3:HIST + tt, :]


def rglru_core(gate, u, conv_buf, h0, conv_w, conv_b, wr_bd, r_b, wi_bd, i_b, lam):
    bsz, t_len, _ = u.shape
    if t_len % 256 == 0:
        tt, valid = 256, 256
    else:
        tt, valid = 16, t_len
    n_t = max(t_len // tt, 1)
    row = lambda v: v.reshape(1, LRU_W)
    vec = lambda: pl.BlockSpec((1, LRU_W), lambda b, t: (0, 0))
    wspec = lambda: pl.BlockSpec((LRU_W // MXU_DIM, MXU_DIM, MXU_DIM), lambda b, t: (0, 0, 0))
    hg, nbuf, hlast = pl.pallas_call(
        functools.partial(_rglru_kernel, tt=tt, valid=valid),
        grid=(bsz, n_t),
        in_specs=[pl.BlockSpec((1, valid, LRU_W), lambda b, t: (b, t, 0)),
                  pl.BlockSpec((1, valid, LRU_W), lambda b, t: (b, t, 0)),
                  pl.BlockSpec((1, CONV_W - 1, LRU_W), lambda b, t: (b, 0, 0)),
                  pl.BlockSpec((1, 1, LRU_W), lambda b, t: (b, 0, 0)),
                  pl.BlockSpec((CONV_W, LRU_W), lambda b, t: (0, 0)),
                  vec(), wspec(), vec(), wspec(), vec(), vec()],
        out_specs=[pl.BlockSpec((1, valid, LRU_W), lambda b, t: (b, t, 0)),
                   pl.BlockSpec((1, CONV_W - 1, LRU_W), lambda b, t: (b, 0, 0)),
                   pl.BlockSpec((1, 1, LRU_W), lambda b, t: (b, 0, 0))],
        out_shape=[jax.ShapeDtypeStruct((bsz, t_len, LRU_W), F32),
                   jax.ShapeDtypeStruct((bsz, CONV_W - 1, LRU_W), F32),
                   jax.ShapeDtypeStruct((bsz, 1, LRU_W), F32)],
        scratch_shapes=[pltpu.VMEM((HIST + tt, LRU_W), F32),
                        pltpu.VMEM((tt, LRU_W), F32),
                        pltpu.VMEM((tt, LRU_W), F32),
                        pltpu.VMEM((1, LRU_W), F32)],
        compiler_params=_cparams(2),
        name="rglru_core",
    )(gate, u, conv_buf, h0.reshape(bsz, 1, LRU_W), conv_w, row(conv_b), wr_bd, row(r_b),
      wi_bd, row(i_b), row(lam))
    return hg, nbuf, hlast.reshape(bsz, LRU_W)


def _block_diag_tiles(w):
    per = MXU_DIM // LRU_BS
    w4 = w.reshape(LRU_W // MXU_DIM, per, LRU_BS, LRU_BS)
    eye = jnp.eye(per, dtype=w.dtype)
    t = jnp.einsum("jakc,ab->jakbc", w4, eye)
    return t.reshape(LRU_W // MXU_DIM, MXU_DIM, MXU_DIM).astype(BF16)


def _diff_lambda(lp, lam_init):
    s1 = jnp.sum(lp[0:1, :] * lp[1:2, :], axis=-1, keepdims=True)
    s2 = jnp.sum(lp[2:3, :] * lp[3:4, :], axis=-1, keepdims=True)
    return jnp.exp(s1) - jnp.exp(s2) + lam_init


def _subln_gate(o, sw, gate, lam_init):
    ms = jnp.mean(o * o, axis=-1, keepdims=True)
    return (o * lax.rsqrt(ms + EPS) * sw * (1.0 - lam_init)) * _silu(gate)


def _attn_prompt_kernel(lp_ref, q_ref, kt_ref, v_ref, gate_ref, sw_ref, o_ref,
                        kb_s, vx_s, m_s, accl_s, *, tq, lam_init):
    qi = pl.program_id(2)
    hd = 2 * ATT_DH
    t_len = v_ref.shape[1]

    @pl.when(qi == 0)
    def _():
        for i in range(t_len // tq):
            kb_s[i] = kt_ref[0, :, i * tq:(i + 1) * tq].astype(BF16)

        def cvt(i, carry):
            r = pl.multiple_of(i * tq, tq)
            vx_s[pl.ds(r, tq), 0:hd] = v_ref[0, pl.ds(r, tq), :].astype(BF16)
            vx_s[pl.ds(r, tq), hd:2 * hd] = jnp.ones((tq, hd), BF16)
            return carry
        lax.fori_loop(0, t_len // tq, cvt, 0)

    lam = _diff_lambda(lp_ref[...], lam_init)
    q = q_ref[0] * (ATT_DH ** -0.5 * LOG2E)
    lane = lax.broadcasted_iota(jnp.int32, (tq, hd), 1)
    q_maps = (jnp.where(lane < ATT_DH, q, 0.0).astype(BF16),
              jnp.where(lane >= ATT_DH, q, 0.0).astype(BF16))
    m_s[...] = jnp.full(m_s.shape, NEG_BIG, F32)
    accl_s[...] = jnp.zeros(accl_s.shape, F32)
    rowi = lax.broadcasted_iota(jnp.int32, (tq, tq), 0)
    coli = lax.broadcasted_iota(jnp.int32, (tq, tq), 1)

    def step(ki, masked):
        ks = pl.multiple_of(ki * tq, tq)
        kb = kb_s[ki]
        vx = vx_s[pl.ds(ks, tq), :]
        for c in range(2):
            s = jnp.dot(q_maps[c], kb, preferred_element_type=F32)
            if masked:
                s = jnp.where(coli <= rowi, s, NEG_BIG)
            m_prev = m_s[c]
            m_new = jnp.maximum(m_prev, jnp.max(s, axis=-1, keepdims=True))
            alpha = jnp.exp2(m_prev - m_new)
            p = jnp.exp2(s - jnp.concatenate([m_new] * (tq // hd), axis=1))
            pv = jnp.dot(p.astype(BF16), vx, preferred_element_type=F32)
            accl_s[c] = jnp.concatenate([alpha, alpha], axis=1) * accl_s[c] + pv
            m_s[c] = m_new

    def body(ki, carry):
        step(ki, False)
        return carry

    lax.fori_loop(0, qi, body, 0)
    step(qi, True)
    a1 = accl_s[0]
    a2 = accl_s[1]
    o = a1[:, 0:hd] / a1[:, hd:2 * hd] - lam * (a2[:, 0:hd] / a2[:, hd:2 * hd])
    o_ref[0] = _subln_gate(o, sw_ref[...], gate_ref[0], lam_init)


def attn_prompt(q, kt, v, gate, lam_p, subln_w, lam_init, tq=512):
    bsz, t_len, _ = q.shape
    hd = 2 * ATT_DH
    qspec = lambda: pl.BlockSpec((1, tq, hd), lambda b, h, i: (b, i, h))
    return pl.pallas_call(
        functools.partial(_attn_prompt_kernel, tq=tq, lam_init=lam_init),
        grid=(bsz, ATT_HEADS, t_len // tq),
        in_specs=[pl.BlockSpec((4, ATT_DH), lambda b, h, i: (0, 0)),
                  qspec(),
                  pl.BlockSpec((1, hd, t_len), lambda b, h, i: (b, h, 0)),
                  pl.BlockSpec((1, t_len, hd), lambda b, h, i: (b, 0, h)),
                  qspec(),
                  pl.BlockSpec((1, hd), lambda b, h, i: (0, 0))],
        out_specs=qspec(),
        out_shape=jax.ShapeDtypeStruct((bsz, t_len, ATT_W), F32),
        scratch_shapes=[pltpu.VMEM((t_len // tq, hd, tq), BF16),
                        pltpu.VMEM((t_len, 2 * hd), BF16),
                        pltpu.VMEM((2, tq, hd), F32), pltpu.VMEM((2, tq, 2 * hd), F32)],
        compiler_params=_cparams(3),
        name="attn_prompt",
    )(lam_p, q, kt, v, gate, subln_w.reshape(1, hd))


def _attn_sample_kernel(pt_ref, lp_ref, qbd_ref, *refs, n_pg, t_new, lam_init):
    k_refs = refs[:n_pg]
    v_refs = refs[n_pg:2 * n_pg]
    knew_ref, vnew_ref, gate_ref, sw_ref, o_ref, m_s, l_s, acc_s = refs[2 * n_pg:]
    j = pl.program_id(1)
    n_j = pl.num_programs(1)
    n_rows = ATT_HEADS * 2 * t_new
    hd = 2 * ATT_DH

    @pl.when(j == 0)
    def _():
        m_s[...] = jnp.full(m_s.shape, NEG_BIG, F32)
        l_s[...] = jnp.zeros(l_s.shape, F32)
        acc_s[...] = jnp.zeros(acc_s.shape, F32)

    qbd = qbd_ref[0]

    def attend(kts, vbs, mask):
        n = len(kts)
        s = jnp.concatenate([jnp.dot(qbd, kt.astype(BF16), preferred_element_type=F32)
                             for kt in kts], axis=1)
        if mask is not None:
            s = jnp.where(mask, s, NEG_BIG)
        m_prev = m_s[...]
        m_new = jnp.maximum(m_prev, jnp.max(s, axis=-1, keepdims=True))
        alpha = jnp.exp2(m_prev - m_new)
        p = jnp.exp2(s - jnp.concatenate([m_new] * n, axis=1))
        l_s[...] = alpha * l_s[...] + jnp.sum(p, axis=-1, keepdims=True)
        pb = p.astype(BF16)
        pv = jnp.dot(pb[:, 0:PAGE_SIZE], vbs[0].astype(BF16), preferred_element_type=F32)
        for i in range(1, n):
            pv = pv + jnp.dot(pb[:, i * PAGE_SIZE:(i + 1) * PAGE_SIZE], vbs[i].astype(BF16),
                              preferred_element_type=F32)
        acc_s[...] = jnp.concatenate([alpha] * (ATT_W // LANES), axis=1) * acc_s[...] + pv
        m_s[...] = m_new

    def page_v(vr):
        return jnp.concatenate([vr[0, pl.ds(h, PAGE_SIZE, stride=ATT_HEADS), :]
                                for h in range(ATT_HEADS)], axis=1)

    attend([r[0] for r in k_refs], [page_v(r) for r in v_refs], None)

    @pl.when(j == n_j - 1)
    def _():
        rowi = lax.broadcasted_iota(jnp.int32, (n_rows, PAGE_SIZE), 0)
        coli = lax.broadcasted_iota(jnp.int32, (n_rows, PAGE_SIZE), 1)
        attend([knew_ref[0]], [vnew_ref[0]], coli <= (rowi % t_new))
        lam = _diff_lambda(lp_ref[...], lam_init)
        sw = sw_ref[...]
        for h in range(ATT_HEADS):
            r1 = h * 2 * t_new
            r2 = r1 + t_new
            cs = slice(h * hd, (h + 1) * hd)
            o1 = acc_s[r1:r1 + t_new, cs] / l_s[r1:r1 + t_new, :]
            o2 = acc_s[r2:r2 + t_new, cs] / l_s[r2:r2 + t_new, :]
            o_ref[0, :, cs] = _subln_gate(o1 - lam * o2, sw, gate_ref[0, :, cs], lam_init)


def attn_sample(q, k, v, gate, cache_k, cache_v, layer, page_table, lam_p, subln_w, lam_init,
                n_pg=8):
    bsz, t_new, _ = q.shape
    n_pages = page_table.shape[1]
    n_pool = cache_k.shape[1]
    hd = 2 * ATT_DH
    n_rows = ATT_HEADS * 2 * t_new
    q4 = (q * (ATT_DH ** -0.5 * LOG2E)).reshape(bsz, t_new, 2 * ATT_HEADS, ATT_DH)
    eye = jnp.eye(2 * ATT_HEADS, dtype=F32)
    qbd = jnp.einsum("bqhd,hg->bhqgd", q4, eye).reshape(bsz, n_rows, ATT_W).astype(BF16)
    k_new = jnp.pad(jnp.swapaxes(k, 1, 2), ((0, 0), (0, 0), (0, PAGE_SIZE - t_new)))
    v_new = jnp.pad(v, ((0, 0), (0, PAGE_SIZE - t_new), (0, 0)))
    ck = jnp.transpose(cache_k, (0, 1, 3, 4, 5, 2)).reshape(-1, ATT_W, PAGE_SIZE)
    cv = cache_v.reshape(-1, PAGE_SIZE * ATT_HEADS, hd)
    pt = page_table.reshape(-1) + layer * n_pool

    def page_spec(i, rows, width):
        return pl.BlockSpec((1, rows, width),
                            lambda b, j, pt_ref: (pt_ref[b * n_pages + j * n_pg + i], 0, 0))

    per_b = lambda r: pl.BlockSpec((1, r, ATT_W), lambda b, j, pt_ref: (b, 0, 0))
    grid_spec = pltpu.PrefetchScalarGridSpec(
        num_scalar_prefetch=1,
        grid=(bsz, n_pages // n_pg),
        in_specs=([pl.BlockSpec((4, ATT_DH), lambda b, j, pt_ref: (0, 0)), per_b(n_rows)]
                  + [page_spec(i, ATT_W, PAGE_SIZE) for i in range(n_pg)]
                  + [page_spec(i, PAGE_SIZE * ATT_HEADS, hd) for i in range(n_pg)]
                  + [pl.BlockSpec((1, ATT_W, PAGE_SIZE), lambda b, j, pt_ref: (b, 0, 0)),
                     per_b(PAGE_SIZE), per_b(t_new),
                     pl.BlockSpec((1, hd), lambda b, j, pt_ref: (0, 0))]),
        out_specs=per_b(t_new),
        scratch_shapes=[pltpu.VMEM((n_rows, LANES), F32), pltpu.VMEM((n_rows, LANES), F32),
                        pltpu.VMEM((n_rows, ATT_W), F32)],
    )
    return pl.pallas_call(
        functools.partial(_attn_sample_kernel, n_pg=n_pg, t_new=t_new, lam_init=lam_init),
        grid_spec=grid_spec,
        out_shape=jax.ShapeDtypeStruct((bsz, t_new, ATT_W), F32),
        compiler_params=_cparams(2),
        name="attn_sample",
    )(pt, lam_p, qbd, *([ck] * n_pg), *([cv] * n_pg), k_new, v_new, gate,
      subln_w.reshape(1, hd))


def _ssd_kernel(xbc_ref, z_ref, dtr_ref, buf_ref, h0_ref, cw_ref, cb_ref, dtb_ref, alog_ref,
                dexp_ref, nw_ref, yn_ref, nbuf_ref, hout_ref, ubuf, dt_s, y_s, *, vl):
    L = SSD_CHUNK
    c = pl.program_id(1)
    n_c = pl.num_programs(1)
    gw = SSD_HPG * SSD_HEADDIM

    @pl.when(c == 0)
    def _():
        ubuf[HIST - 3:HIST, :] = buf_ref[0]
        hout_ref[0] = h0_ref[0]
        if vl < L:
            ubuf[HIST + vl:HIST + L, :] = jnp.zeros((L - vl, SSD_CONV_DIM), F32)
            dt_s[...] = jnp.zeros((L, LANES), F32)

    ubuf[HIST:HIST + vl, :] = xbc_ref[0]
    dt_s[0:vl, :] = dtr_ref[0]

    rowL = lax.broadcasted_iota(jnp.int32, (L, LANES), 0)
    laneL = lax.broadcasted_iota(jnp.int32, (L, LANES), 1)
    dt = _softplus(dt_s[...] + dtb_ref[...])
    dt = jnp.where((rowL < vl) & (laneL < SSD_HEADS), dt, 0.0)
    adt = dt * (-jnp.exp(alog_ref[...]))
    tri = (lax.broadcasted_iota(jnp.int32, (L, L), 0)
           >= lax.broadcasted_iota(jnp.int32, (L, L), 1))
    cs = jnp.dot(tri.astype(F32), adt, preferred_element_type=F32,
                 precision=lax.Precision.HIGHEST)
    cs_last = cs[L - 1:L, :]
    w1 = dt * jnp.exp(cs_last - cs)
    cs_t = cs.T
    dt_t = dt.T
    w1_t = w1.T

    cw = cw_ref[...]
    lane_g = lax.broadcasted_iota(jnp.int32, (L, gw), 1)

    def conv_silu(lo, width):
        sl = slice(lo, lo + width)
        y = cb_ref[:, sl] + cw[3:4, sl] * ubuf[HIST:HIST + L, sl]
        for k in range(1, CONV_W):
            y = y + cw[3 - k:4 - k, sl] * ubuf[HIST - k:HIST - k + L, sl]
        return _silu(y)

    for g in range(SSD_GROUPS):
        xh = conv_silu(g * gw, gw)
        bm = conv_silu(SSD_INNER + g * SSD_STATE, SSD_STATE)
        cm = conv_silu(SSD_INNER + (SSD_GROUPS + g) * SSD_STATE, SSD_STATE)
        bmb = bm.astype(BF16)
        cmb = cm.astype(BF16)
        xhb = xh.astype(BF16)
        cbm = _nt(cmb, bmb)
        y_diag = jnp.zeros((L, gw), F32)
        e_cols = []
        w_rows = []
        d_rows = []
        for r in range(SSD_HPG):
            h = g * SSD_HPG + r
            cs_col = cs[:, h:h + 1]
            cs_row = cs_t[h:h + 1, :]
            lm = jnp.exp(jnp.where(tri, cs_col - cs_row, NEG_BIG))
            mat = (cbm * lm * dt_t[h:h + 1, :]).astype(BF16)
            in_head = (lane_g >= r * SSD_HEADDIM) & (lane_g < (r + 1) * SSD_HEADDIM)
            xr = jnp.where(in_head, xhb, jnp.zeros_like(xhb))
            y_diag = y_diag + jnp.dot(mat, xr, preferred_element_type=F32)
            e_cols.append(jnp.broadcast_to(jnp.exp(cs_col), (L, SSD_HEADDIM)))
            w_rows.append(jnp.broadcast_to(w1_t[h:h + 1, :], (SSD_HEADDIM, L)))
            d_rows.append(jnp.broadcast_to(jnp.exp(cs_t[h:h + 1, L - 1:L]),
                                           (SSD_HEADDIM, SSD_STATE)))
        hg = hout_ref[0, g]
        y_off = _nt(cmb, hg.astype(BF16)) * jnp.concatenate(e_cols, axis=1)
        y_s[:, g * gw:(g + 1) * gw] = y_diag + y_off + dexp_ref[:, g * gw:(g + 1) * gw] * xh
        xd_t = (xh.T * jnp.concatenate(w_rows, axis=0)).astype(BF16)
        states = jnp.dot(xd_t, bmb, preferred_element_type=F32)
        hout_ref[0, g] = hg * jnp.concatenate(d_rows, axis=0) + states

    gated = y_s[0:vl, :] * _silu(z_ref[0])
    ms = jnp.mean(gated * gated, axis=-1, keepdims=True)
    yn_ref[0] = gated * lax.rsqrt(ms + EPS) * nw_ref[...]

    @pl.when(c == n_c - 1)
    def _():
        nbuf_ref[0] = ubuf[HIST + vl - 3:HIST + vl, :]

    ubuf[HIST - 3:HIST, :] = ubuf[HIST + L - 3:HIST + L, :]


def ssd_core(xbc, z, dt_raw, conv_buf, h0, conv_w, conv_b, dt_bias, a_log, d_skip, norm_w):
    bsz, t_len, _ = xbc.shape
    L = SSD_CHUNK
    vl = L if t_len % L == 0 else t_len
    n_c = max(t_len // L, 1)
    gw = SSD_HPG * SSD_HEADDIM
    pad_lanes = lambda v: jnp.pad(v.reshape(1, SSD_HEADS), ((0, 0), (0, LANES - SSD_HEADS)))
    dexp = jnp.repeat(d_skip, SSD_HEADDIM).reshape(1, SSD_INNER)
    h0g = h0.reshape(bsz, SSD_GROUPS, gw, SSD_STATE)
    const = lambda shape: pl.BlockSpec(shape, lambda b, c: (0,) * len(shape))
    yn, nbuf, hout = pl.pallas_call(
        functools.partial(_ssd_kernel, vl=vl),
        grid=(bsz, n_c),
        in_specs=[pl.BlockSpec((1, vl, SSD_CONV_DIM), lambda b, c: (b, c, 0)),
                  pl.BlockSpec((1, vl, SSD_INNER), lambda b, c: (b, c, 0)),
                  pl.BlockSpec((1, vl, LANES), lambda b, c: (b, c, 0)),
                  pl.BlockSpec((1, CONV_W - 1, SSD_CONV_DIM), lambda b, c: (b, 0, 0)),
                  pl.BlockSpec((1, SSD_GROUPS, gw, SSD_STATE), lambda b, c: (b, 0, 0, 0)),
                  const((CONV_W, SSD_CONV_DIM)), const((1, SSD_CONV_DIM)),
                  const((1, LANES)), const((1, LANES)),
                  const((1, SSD_INNER)), const((1, SSD_INNER))],
        out_specs=[pl.BlockSpec((1, vl, SSD_INNER), lambda b, c: (b, c, 0)),
                   pl.BlockSpec((1, CONV_W - 1, SSD_CONV_DIM), lambda b, c: (b, 0, 0)),
                   pl.BlockSpec((1, SSD_GROUPS, gw, SSD_STATE), lambda b, c: (b, 0, 0, 0))],
        out_shape=[jax.ShapeDtypeStruct((bsz, t_len, SSD_INNER), F32),
                   jax.ShapeDtypeStruct((bsz, CONV_W - 1, SSD_CONV_DIM), F32),
                   jax.ShapeDtypeStruct((bsz, SSD_GROUPS, gw, SSD_STATE), F32)],
        scratch_shapes=[pltpu.VMEM((HIST + L, SSD_CONV_DIM), F32),
                        pltpu.VMEM((L, LANES), F32),
                        pltpu.VMEM((L, SSD_INNER), F32)],
        compiler_params=_cparams(2),
        name="ssd_core",
    )(xbc, z, dt_raw, conv_buf, h0g, conv_w, conv_b.reshape(1, SSD_CONV_DIM),
      pad_lanes(dt_bias), pad_lanes(a_log), dexp, norm_w.reshape(1, SSD_INNER))
    return yn, nbuf, hout.reshape(bsz, SSD_HEADS, SSD_HEADDIM, SSD_STATE)


def _row_tile(rows):
    return 256 if rows % 256 == 0 else rows


def _rglru_layer(x, conv_buf, h0, w, ln_g, ln_b):
    in_w, conv_w, conv_b, wr_bd, r_b, wi_bd, i_b, lam, out_w = w
    bsz, t_len, _ = x.shape
    x2 = x.reshape(bsz * t_len, D_MODEL)
    tm = _row_tile(bsz * t_len)
    gate, u = proj_split(x2, in_w, ((0, LRU_W), (LRU_W, LRU_W)), tm)
    hg, nbuf, hlast = rglru_core(gate.reshape(bsz, t_len, LRU_W), u.reshape(bsz, t_len, LRU_W),
                                 conv_buf, h0, conv_w, conv_b, wr_bd, r_b, wi_bd, i_b, lam)
    xn = outproj_ln(hg.reshape(bsz * t_len, LRU_W), out_w, x2, ln_g, ln_b, tm)
    return xn.reshape(bsz, t_len, D_MODEL), nbuf, hlast


def _attn_proj_t_kernel(x_ref, w_ref, wkt_ref, q_ref, kt_ref, v_ref, g_ref):
    x = x_ref[0].astype(BF16)
    q_ref[0] = jnp.dot(x, w_ref[:, 0:ATT_W], preferred_element_type=F32)
    kt_ref[0] = _nt(wkt_ref[...], x)
    v_ref[0] = jnp.dot(x, w_ref[:, 2 * ATT_W:3 * ATT_W], preferred_element_type=F32)
    g_ref[0] = jnp.dot(x, w_ref[:, 3 * ATT_W:4 * ATT_W], preferred_element_type=F32)


def attn_project_t(x, in_w, wkt, tm=256):
    bsz, t_len, _ = x.shape
    rows = lambda: pl.BlockSpec((1, tm, ATT_W), lambda b, i: (b, i, 0))
    return pl.pallas_call(
        _attn_proj_t_kernel,
        grid=(bsz, t_len // tm),
        in_specs=[pl.BlockSpec((1, tm, D_MODEL), lambda b, i: (b, i, 0)),
                  pl.BlockSpec((D_MODEL, 4 * ATT_W), lambda b, i: (0, 0)),
                  pl.BlockSpec((ATT_W, D_MODEL), lambda b, i: (0, 0))],
        out_specs=[rows(), pl.BlockSpec((1, ATT_W, tm), lambda b, i: (b, 0, i)), rows(), rows()],
        out_shape=[jax.ShapeDtypeStruct((bsz, t_len, ATT_W), F32),
                   jax.ShapeDtypeStruct((bsz, ATT_W, t_len), F32),
                   jax.ShapeDtypeStruct((bsz, t_len, ATT_W), F32),
                   jax.ShapeDtypeStruct((bsz, t_len, ATT_W), F32)],
        compiler_params=_cparams(2),
        name="attn_project_t",
    )(x, in_w, wkt)


def _attn_project(x, in_w):
    bsz, t_len, _ = x.shape
    x2 = x.reshape(bsz * t_len, D_MODEL)
    tm = _row_tile(bsz * t_len)
    splits = tuple((i * ATT_W, ATT_W) for i in range(4))
    q, k, v, gate = proj_split(x2, in_w, splits, tm)
    shp = (bsz, t_len, ATT_W)
    return x2, tm, q.reshape(shp), k.reshape(shp), v.reshape(shp), gate.reshape(shp)


def _ssd_layer(x, conv_buf, h0, w, ln_g, ln_b):
    in_w_pad, conv_w, conv_b, dt_bias, a_log, d_skip, norm_w, out_w = w
    bsz, t_len, _ = x.shape
    x2 = x.reshape(bsz * t_len, D_MODEL)
    tm = _row_tile(bsz * t_len)
    splits = ((0, SSD_INNER), (SSD_INNER, SSD_CONV_DIM), (SSD_INNER + SSD_CONV_DIM, LANES))
    z, xbc, dt_raw = proj_split(x2, in_w_pad, splits, tm)
    yn, nbuf, hlast = ssd_core(xbc.reshape(bsz, t_len, SSD_CONV_DIM),
                               z.reshape(bsz, t_len, SSD_INNER),
                               dt_raw.reshape(bsz, t_len, LANES),
                               conv_buf, h0, conv_w, conv_b, dt_bias, a_log, d_skip, norm_w)
    xn = outproj_ln(yn.reshape(bsz * t_len, SSD_INNER), out_w, x2, ln_g, ln_b, tm)
    return xn.reshape(bsz, t_len, D_MODEL), nbuf, hlast


def kernel(x_prompt, x_sample, cache_k, cache_v, page_table, state_lru_conv, state_lru_h, state_ssd_conv, state_ssd_h, ln_g, ln_b, a_in_w, a_conv_w, a_conv_b, a_gate_r_w, a_gate_r_b, a_gate_i_w, a_gate_i_b, a_lambda, a_out_w, b_in_w, b_lambda, b_subln_w, b_out_w, c_in_w, c_conv_w, c_conv_b, c_dt_bias, c_a_log, c_d, c_norm_w, c_out_w):
    xp, xs = x_prompt, x_sample
    bp, bs = xp.shape[0], xs.shape[0]
    tp, ts = xp.shape[1], xs.shape[1]
    k_p, v_p, k_s, v_s = [], [], [], []
    lc_p, lh_p, lc_s, lh_s = [], [], [], []
    sc_p, sh_p, sc_s, sh_s = [], [], [], []
    for i in range(DEPTH):
        j = i // N_MIXERS
        kind = i % N_MIXERS
        if kind == 0:
            w = (a_in_w[j].astype(BF16), a_conv_w[j], a_conv_b[j], _block_diag_tiles(a_gate_r_w[j]),
                 a_gate_r_b[j], _block_diag_tiles(a_gate_i_w[j]), a_gate_i_b[j], a_lambda[j],
                 a_out_w[j].astype(BF16))
            zc = jnp.zeros((bp, CONV_W - 1, LRU_W), F32)
            zh = jnp.zeros((bp, LRU_W), F32)
            xp, c1, h1 = _rglru_layer(xp, zc, zh, w, ln_g[i], ln_b[i])
            xs, c2, h2 = _rglru_layer(xs, state_lru_conv[j], state_lru_h[j], w, ln_g[i], ln_b[i])
            lc_p.append(c1); lh_p.append(h1); lc_s.append(c2); lh_s.append(h2)
        elif kind == 1:
            lam_init = 0.8 - 0.6 * math.exp(-0.3 * i)
            in_w = b_in_w[j].astype(BF16)
            out_w = b_out_w[j].astype(BF16)
            wkt = b_in_w[j][:, ATT_W:2 * ATT_W].T.astype(BF16)
            q, kt, v, g = attn_project_t(xp, in_w, wkt)
            og = attn_prompt(q, kt, v, g, b_lambda[j], b_subln_w[j], lam_init)
            xp = outproj_ln(og.reshape(bp * tp, ATT_W), out_w, xp.reshape(bp * tp, D_MODEL),
                            ln_g[i], ln_b[i], _row_tile(bp * tp)).reshape(bp, tp, D_MODEL)
            k_p.append(jnp.transpose(kt.reshape(bp, ATT_HEADS, 2, ATT_DH, tp), (0, 4, 1, 2, 3)))
            v_p.append(v.reshape(bp, tp, ATT_HEADS, 2 * ATT_DH))
            x2, tm, q, k, v, g = _attn_project(xs, in_w)
            og = attn_sample(q, k, v, g, cache_k, cache_v, j, page_table, b_lambda[j],
                             b_subln_w[j], lam_init)
            xs = outproj_ln(og.reshape(bs * ts, ATT_W), out_w, x2, ln_g[i], ln_b[i], tm
                            ).reshape(bs, ts, D_MODEL)
            k_s.append(k.reshape(bs, ts, ATT_HEADS, 2, ATT_DH))
            v_s.append(v.reshape(bs, ts, ATT_HEADS, 2 * ATT_DH))
        else:
            in_w_pad = jnp.pad(c_in_w[j], ((0, 0), (0, LANES - SSD_HEADS))).astype(BF16)
            w = (in_w_pad, c_conv_w[j], c_conv_b[j], c_dt_bias[j], c_a_log[j], c_d[j], c_norm_w[j],
                 c_out_w[j].astype(BF16))
            zc = jnp.zeros((bp, CONV_W - 1, SSD_CONV_DIM), F32)
            zh = jnp.zeros((bp, SSD_HEADS, SSD_HEADDIM, SSD_STATE), F32)
            xp, c1, h1 = _ssd_layer(xp, zc, zh, w, ln_g[i], ln_b[i])
            xs, c2, h2 = _ssd_layer(xs, state_ssd_conv[j], state_ssd_h[j], w, ln_g[i], ln_b[i])
            sc_p.append(c1); sh_p.append(h1); sc_s.append(c2); sh_s.append(h2)
    return (xp, xs, jnp.stack(k_p), jnp.stack(v_p), jnp.stack(k_s), jnp.stack(v_s),
            jnp.stack(lc_p), jnp.stack(lh_p), jnp.stack(lc_s), jnp.stack(lh_s),
            jnp.stack(sc_p), jnp.stack(sh_p), jnp.stack(sc_s), jnp.stack(sh_s))
```

```python
import functools
import math

import jax
import jax.numpy as jnp
from jax import lax
from jax.experimental import pallas as pl
from jax.experimental.pallas import tpu as pltpu

F32 = jnp.float32
BF16 = jnp.bfloat16

D_MODEL = 1024
DEPTH = 4
PAGE_SIZE = 128
N_MIXERS = 3
DN_ALPHA = (2.0 * DEPTH) ** 0.25
EPS = 1e-5
CONV_W = 4
LRU_W = D_MODEL
LRU_BLOCKS = 16
LRU_BS = LRU_W // LRU_BLOCKS
LRU_C = 8.0
ATT_HEADS = 8
ATT_DH = D_MODEL // (2 * ATT_HEADS)
ATT_W = ATT_HEADS * 2 * ATT_DH
SSD_INNER = 2 * D_MODEL
SSD_HEADDIM = 64
SSD_HEADS = SSD_INNER // SSD_HEADDIM
SSD_GROUPS = 8
SSD_HPG = SSD_HEADS // SSD_GROUPS
SSD_STATE = 128
SSD_CONV_DIM = SSD_INNER + 2 * SSD_GROUPS * SSD_STATE
SSD_CHUNK = 128

LANES = 128
SUBLANES = 8
MXU_DIM = 256
VMEM_LIMIT = 56 * 1024 * 1024
LOG2E = math.log2(math.e)
F32_TINY = 1.1754944e-38
NEG_BIG = -1e30
HIST = SUBLANES


def _cparams(n_grid):
    return pltpu.CompilerParams(dimension_semantics=("arbitrary",) * n_grid,
                                vmem_limit_bytes=VMEM_LIMIT)


def _nt(a, b):
    return lax.dot_general(a, b, (((1,), (1,)), ((), ())), preferred_element_type=F32)


def _sigmoid(x):
    return 0.5 * jnp.tanh(0.5 * x) + 0.5


def _sqrt_nonneg(x):
    return x * lax.rsqrt(jnp.maximum(x, F32_TINY))


def _layer_norm(v, g, b):
    mu = jnp.mean(v, axis=-1, keepdims=True)
    d = v - mu
    var = jnp.mean(d * d, axis=-1, keepdims=True)
    return d * lax.rsqrt(var + EPS) * g + b


def _silu(x):
    return x * _sigmoid(x)


def _softplus(x):
    return jnp.maximum(x, 0.0) + jnp.log(1.0 + jnp.exp(-jnp.abs(x)))


def _proj_kernel(x_ref, w_ref, *out_refs, splits):
    x = x_ref[...].astype(BF16)
    for (off, width), o_ref in zip(splits, out_refs):
        o_ref[...] = jnp.dot(x, w_ref[:, off:off + width], preferred_element_type=F32)


def proj_split(x2d, w_bf16, splits, tm):
    rows, kdim = x2d.shape
    ndim = w_bf16.shape[1]
    return pl.pallas_call(
        functools.partial(_proj_kernel, splits=splits),
        grid=(rows // tm,),
        in_specs=[pl.BlockSpec((tm, kdim), lambda i: (i, 0)),
                  pl.BlockSpec((kdim, ndim), lambda i: (0, 0))],
        out_specs=[pl.BlockSpec((tm, wd), lambda i: (i, 0)) for _, wd in splits],
        out_shape=[jax.ShapeDtypeStruct((rows, wd), F32) for _, wd in splits],
        compiler_params=_cparams(1),
        name="proj_split",
    )(x2d, w_bf16)


def _outproj_ln_kernel(y_ref, w_ref, x_ref, g_ref, b_ref, o_ref):
    f = jnp.dot(y_ref[...].astype(BF16), w_ref[...], preferred_element_type=F32)
    v = DN_ALPHA * x_ref[...] + f
    mu = jnp.mean(v, axis=-1, keepdims=True)
    d = v - mu
    var = jnp.mean(d * d, axis=-1, keepdims=True)
    o_ref[...] = d * lax.rsqrt(var + EPS) * g_ref[...] + b_ref[...]


def outproj_ln(y2d, w_bf16, x2d, g, b, tm):
    rows, kdim = y2d.shape
    return pl.pallas_call(
        _outproj_ln_kernel,
        grid=(rows // tm,),
        in_specs=[pl.BlockSpec((tm, kdim), lambda i: (i, 0)),
                  pl.BlockSpec((kdim, D_MODEL), lambda i: (0, 0)),
                  pl.BlockSpec((tm, D_MODEL), lambda i: (i, 0)),
                  pl.BlockSpec((1, D_MODEL), lambda i: (0, 0)),
                  pl.BlockSpec((1, D_MODEL), lambda i: (0, 0))],
        out_specs=pl.BlockSpec((tm, D_MODEL), lambda i: (i, 0)),
        out_shape=jax.ShapeDtypeStruct((rows, D_MODEL), F32),
        compiler_params=_cparams(1),
        name="outproj_ln",
    )(y2d, w_bf16, x2d, g.reshape(1, D_MODEL), b.reshape(1, D_MODEL))


def _rglru_kernel(x_ref, buf_ref, h0_ref, win_ref, cw_ref, cb_ref, wr_ref, rb_ref,
                  wi_ref, ib_ref, lam_ref, wout_ref, lng_ref, lnb_ref,
                  xo_ref, nbuf_ref, hlast_ref,
                  xpad, ubuf, g_s, a_s, b_s, hcar, *, tt, valid):
    t = pl.program_id(1)
    n_t = pl.num_programs(1)

    @pl.when(t == 0)
    def _():
        ubuf[HIST - 3:HIST, :] = buf_ref[0]
        hcar[...] = h0_ref[0]
        if valid < tt:
            xpad[...] = jnp.zeros((tt, D_MODEL), F32)

    if valid < tt:
        xpad[0:valid, :] = x_ref[0]
        x = xpad[...]
    else:
        x = x_ref[0]
    xb = x.astype(BF16)
    g_s[...] = jnp.dot(xb, win_ref[:, 0:LRU_W], preferred_element_type=F32)
    ubuf[HIST:HIST + tt, :] = jnp.dot(xb, win_ref[:, LRU_W:2 * LRU_W], preferred_element_type=F32)
    cw = cw_ref[...]
    y = cb_ref[...] + cw[3:4, :] * ubuf[HIST:HIST + tt, :]
    for k in range(1, CONV_W):
        y = y + cw[3 - k:4 - k, :] * ubuf[HIST - k:HIST - k + tt, :]

    sp = _softplus(-lam_ref[...])
    yb = y.astype(BF16)
    for j in range(LRU_W // MXU_DIM):
        sl = slice(j * MXU_DIM, (j + 1) * MXU_DIM)
        ys = yb[:, sl]
        r = _sigmoid(jnp.dot(ys, wr_ref[j], preferred_element_type=F32) + rb_ref[:, sl])
        ig = _sigmoid(jnp.dot(ys, wi_ref[j], preferred_element_type=F32) + ib_ref[:, sl])
        log_a = -LRU_C * r * sp[:, sl]
        a = jnp.exp(log_a)
        a_s[:, sl] = a
        b_s[:, sl] = _sqrt_nonneg(-jnp.tanh(log_a) * (a * a + 1.0)) * (ig * y[:, sl])

    row = lax.broadcasted_iota(jnp.int32, (SUBLANES, LRU_W), 0)

    def body(g, hc):
        r0 = pl.multiple_of(g * SUBLANES, SUBLANES)
        a = a_s[pl.ds(r0, SUBLANES), :]
        b = b_s[pl.ds(r0, SUBLANES), :]
        for d in (1, 2, 4):
            a_sh = pltpu.roll(a, d, 0)
            b_sh = pltpu.roll(b, d, 0)
            m = row >= d
            b = jnp.where(m, a * b_sh + b, b)
            a = jnp.where(m, a * a_sh, a)
        h = a * hc + b
        b_s[pl.ds(r0, SUBLANES), :] = h
        return h[SUBLANES - 1:SUBLANES, :]

    lax.fori_loop(0, tt // SUBLANES, body, hcar[...])
    h_end = b_s[valid - 1:valid, :]
    hcar[...] = h_end
    hg = (b_s[...] * _silu(g_s[...])).astype(BF16)
    f = jnp.dot(hg, wout_ref[...], preferred_element_type=F32)
    xn = _layer_norm(DN_ALPHA * x + f, lng_ref[...], lnb_ref[...])
    xo_ref[0] = xn[0:valid, :]

    @pl.when(t == n_t - 1)
    def _():
        nbuf_ref[0] = ubuf[HIST + valid - 3:HIST + valid, :]
        hlast_ref[0] = h_end

    ubuf[HIST - 3:HIST, :] = ubuf[HIST + tt - 3:HIST + tt, :]


def rglru_layer(x, conv_buf, h0, w, ln_g, ln_b):
    in_w, conv_w, conv_b, wr_bd, r_b, wi_bd, i_b, lam, out_w = w
    bsz, t_len, _ = x.shape
    if t_len % 256 == 0:
        tt, valid = 256, 256
    else:
        tt, valid = 16, t_len
    n_t = max(t_len // tt, 1)
    row = lambda v: v.reshape(1, LRU_W)
    vec = lambda: pl.BlockSpec((1, LRU_W), lambda b, t: (0, 0))
    wspec = lambda: pl.BlockSpec((LRU_W // MXU_DIM, MXU_DIM, MXU_DIM), lambda b, t: (0, 0, 0))
    xn, nbuf, hlast = pl.pallas_call(
        functools.partial(_rglru_kernel, tt=tt, valid=valid),
        grid=(bsz, n_t),
        in_specs=[pl.BlockSpec((1, valid, D_MODEL), lambda b, t: (b, t, 0)),
                  pl.BlockSpec((1, CONV_W - 1, LRU_W), lambda b, t: (b, 0, 0)),
                  pl.BlockSpec((1, 1, LRU_W), lambda b, t: (b, 0, 0)),
                  pl.BlockSpec((D_MODEL, 2 * LRU_W), lambda b, t: (0, 0)),
                  pl.BlockSpec((CONV_W, LRU_W), lambda b, t: (0, 0)),
                  vec(), wspec(), vec(), wspec(), vec(), vec(),
                  pl.BlockSpec((LRU_W, D_MODEL), lambda b, t: (0, 0)),
                  vec(), vec()],
        out_specs=[pl.BlockSpec((1, valid, D_MODEL), lambda b, t: (b, t, 0)),
                   pl.BlockSpec((1, CONV_W - 1, LRU_W), lambda b, t: (b, 0, 0)),
                   pl.BlockSpec((1, 1, LRU_W), lambda b, t: (b, 0, 0))],
        out_shape=[jax.ShapeDtypeStruct((bsz, t_len, D_MODEL), F32),
                   jax.ShapeDtypeStruct((bsz, CONV_W - 1, LRU_W), F32),
                   jax.ShapeDtypeStruct((bsz, 1, LRU_W), F32)],
        scratch_shapes=[pltpu.VMEM((tt, D_MODEL), F32),
                        pltpu.VMEM((HIST + tt, LRU_W), F32),
                        pltpu.VMEM((tt, LRU_W), F32),
                        pltpu.VMEM((tt, LRU_W), F32),
                        pltpu.VMEM((tt, LRU_W), F32),
                        pltpu.VMEM((1, LRU_W), F32)],
        compiler_params=_cparams(2),
        name="rglru_layer",
    )(x, conv_buf, h0.reshape(bsz, 1, LRU_W), in_w, conv_w, row(conv_b), wr_bd, row(r_b),
      wi_bd, row(i_b), row(lam), out_w, row(ln_g), row(ln_b))
    return xn, nbuf, hlast.reshape(bsz, LRU_W)


def _block_diag_tiles(w):
    per = MXU_DIM // LRU_BS
    w4 = w.reshape(LRU_W // MXU_DIM, per, LRU_BS, LRU_BS)
    eye = jnp.eye(per, dtype=w.dtype)
    t = jnp.einsum("jakc,ab->jakbc", w4, eye)
    return t.reshape(LRU_W // MXU_DIM, MXU_DIM, MXU_DIM).astype(BF16)


def _diff_lambda(lp, lam_init):
    s1 = jnp.sum(lp[0:1, :] * lp[1:2, :], axis=-1, keepdims=True)
    s2 = jnp.sum(lp[2:3, :] * lp[3:4, :], axis=-1, keepdims=True)
    return jnp.exp(s1) - jnp.exp(s2) + lam_init


def _subln_gate(o, sw, gate, lam_init):
    ms = jnp.mean(o * o, axis=-1, keepdims=True)
    return (o * lax.rsqrt(ms + EPS) * sw * (1.0 - lam_init)) * _silu(gate)


def _attn_prompt_kernel(lp_ref, q_ref, kt_ref, v_ref, gate_ref, sw_ref, o_ref,
                        kb_s, vx_s, m_s, accl_s, *, tq, lam_init):
    qi = pl.program_id(2)
    hd = 2 * ATT_DH
    t_len = v_ref.shape[1]

    @pl.when(qi == 0)
    def _():
        for i in range(t_len // tq):
            kb_s[i] = kt_ref[0, :, i * tq:(i + 1) * tq].astype(BF16)

        def cvt(i, carry):
            r = pl.multiple_of(i * tq, tq)
            vx_s[pl.ds(r, tq), 0:hd] = v_ref[0, pl.ds(r, tq), :].astype(BF16)
            vx_s[pl.ds(r, tq), hd:2 * hd] = jnp.ones((tq, hd), BF16)
            return carry
        lax.fori_loop(0, t_len // tq, cvt, 0)

    lam = _diff_lambda(lp_ref[...], lam_init)
    q = q_ref[0] * (ATT_DH ** -0.5 * LOG2E)
    lane = lax.broadcasted_iota(jnp.int32, (tq, hd), 1)
    q_maps = (jnp.where(lane < ATT_DH, q, 0.0).astype(BF16),
              jnp.where(lane >= ATT_DH, q, 0.0).astype(BF16))
    m_s[...] = jnp.full(m_s.shape, NEG_BIG, F32)
    accl_s[...] = jnp.zeros(accl_s.shape, F32)
    rowi = lax.broadcasted_iota(jnp.int32, (tq, tq), 0)
    coli = lax.broadcasted_iota(jnp.int32, (tq, tq), 1)

    def step(ki, masked):
        ks = pl.multiple_of(ki * tq, tq)
        kb = kb_s[ki]
        vx = vx_s[pl.ds(ks, tq), :]
        for c in range(2):
            s = jnp.dot(q_maps[c], kb, preferred_element_type=F32)
            if masked:
                s = jnp.where(coli <= rowi, s, NEG_BIG)
            m_prev = m_s[c]
            m_new = jnp.maximum(m_prev, jnp.max(s, axis=-1, keepdims=True))
            alpha = jnp.exp2(m_prev - m_new)
            p = jnp.exp2(s - jnp.concatenate([m_new] * (tq // hd), axis=1))
            pv = jnp.dot(p.astype(BF16), vx, preferred_element_type=F32)
            accl_s[c] = jnp.concatenate([alpha, alpha], axis=1) * accl_s[c] + pv
            m_s[c] = m_new

    def body(ki, carry):
        step(ki, False)
        return carry

    lax.fori_loop(0, qi, body, 0)
    step(qi, True)
    a1 = accl_s[0]
    a2 = accl_s[1]
    o = a1[:, 0:hd] / a1[:, hd:2 * hd] - lam * (a2[:, 0:hd] / a2[:, hd:2 * hd])
    o_ref[0] = _subln_gate(o, sw_ref[...], gate_ref[0], lam_init)


def attn_prompt(q, kt, v, gate, lam_p, subln_w, lam_init, tq=512):
    bsz, t_len, _ = q.shape
    hd = 2 * ATT_DH
    qspec = lambda: pl.BlockSpec((1, tq, hd), lambda b, h, i: (b, i, h))
    return pl.pallas_call(
        functools.partial(_attn_prompt_kernel, tq=tq, lam_init=lam_init),
        grid=(bsz, ATT_HEADS, t_len // tq),
        in_specs=[pl.BlockSpec((4, ATT_DH), lambda b, h, i: (0, 0)),
                  qspec(),
                  pl.BlockSpec((1, hd, t_len), lambda b, h, i: (b, h, 0)),
                  pl.BlockSpec((1, t_len, hd), lambda b, h, i: (b, 0, h)),
                  qspec(),
                  pl.BlockSpec((1, hd), lambda b, h, i: (0, 0))],
        out_specs=qspec(),
        out_shape=jax.ShapeDtypeStruct((bsz, t_len, ATT_W), F32),
        scratch_shapes=[pltpu.VMEM((t_len // tq, hd, tq), BF16),
                        pltpu.VMEM((t_len, 2 * hd), BF16),
                        pltpu.VMEM((2, tq, hd), F32), pltpu.VMEM((2, tq, 2 * hd), F32)],
        compiler_params=_cparams(3),
        name="attn_prompt",
    )(lam_p, q, kt, v, gate, subln_w.reshape(1, hd))


def _attn_sample_kernel(pt_ref, lp_ref, qbd_ref, *refs, n_pg, t_new, lam_init):
    k_refs = refs[:n_pg]
    v_refs = refs[n_pg:2 * n_pg]
    knew_ref, vnew_ref, gate_ref, sw_ref, o_ref, m_s, l_s, acc_s = refs[2 * n_pg:]
    j = pl.program_id(1)
    n_j = pl.num_programs(1)
    n_rows = ATT_HEADS * 2 * t_new
    hd = 2 * ATT_DH

    @pl.when(j == 0)
    def _():
        m_s[...] = jnp.full(m_s.shape, NEG_BIG, F32)
        l_s[...] = jnp.zeros(l_s.shape, F32)
        acc_s[...] = jnp.zeros(acc_s.shape, F32)

    qbd = qbd_ref[0]

    def attend(kts, vbs, mask):
        n = len(kts)
        s = jnp.concatenate([jnp.dot(qbd, kt.astype(BF16), preferred_element_type=F32)
                             for kt in kts], axis=1)
        if mask is not None:
            s = jnp.where(mask, s, NEG_BIG)
        m_prev = m_s[...]
        m_new = jnp.maximum(m_prev, jnp.max(s, axis=-1, keepdims=True))
        alpha = jnp.exp2(m_prev - m_new)
        p = jnp.exp2(s - jnp.concatenate([m_new] * n, axis=1))
        l_s[...] = alpha * l_s[...] + jnp.sum(p, axis=-1, keepdims=True)
        pb = p.astype(BF16)
        pv = jnp.dot(pb[:, 0:PAGE_SIZE], vbs[0].astype(BF16), preferred_element_type=F32)
        for i in range(1, n):
            pv = pv + jnp.dot(pb[:, i * PAGE_SIZE:(i + 1) * PAGE_SIZE], vbs[i].astype(BF16),
                              preferred_element_type=F32)
        acc_s[...] = jnp.concatenate([alpha] * (ATT_W // LANES), axis=1) * acc_s[...] + pv
        m_s[...] = m_new

    def page_v(vr):
        return jnp.concatenate([vr[0, pl.ds(h, PAGE_SIZE, stride=ATT_HEADS), :]
                                for h in range(ATT_HEADS)], axis=1)

    attend([r[0] for r in k_refs], [page_v(r) for r in v_refs], None)

    @pl.when(j == n_j - 1)
    def _():
        rowi = lax.broadcasted_iota(jnp.int32, (n_rows, PAGE_SIZE), 0)
        coli = lax.broadcasted_iota(jnp.int32, (n_rows, PAGE_SIZE), 1)
        attend([knew_ref[0]], [vnew_ref[0]], coli <= (rowi % t_new))
        lam = _diff_lambda(lp_ref[...], lam_init)
        sw = sw_ref[...]
        for h in range(ATT_HEADS):
            r1 = h * 2 * t_new
            r2 = r1 + t_new
            cs = slice(h * hd, (h + 1) * hd)
            o1 = acc_s[r1:r1 + t_new, cs] / l_s[r1:r1 + t_new, :]
            o2 = acc_s[r2:r2 + t_new, cs] / l_s[r2:r2 + t_new, :]
            o_ref[0, :, cs] = _subln_gate(o1 - lam * o2, sw, gate_ref[0, :, cs], lam_init)


def attn_sample(q, k, v, gate, cache_k, cache_v, layer, page_table, lam_p, subln_w, lam_init,
                n_pg=8):
    bsz, t_new, _ = q.shape
    n_pages = page_table.shape[1]
    n_pool = cache_k.shape[1]
    hd = 2 * ATT_DH
    n_rows = ATT_HEADS * 2 * t_new
    q4 = (q * (ATT_DH ** -0.5 * LOG2E)).reshape(bsz, t_new, 2 * ATT_HEADS, ATT_DH)
    eye = jnp.eye(2 * ATT_HEADS, dtype=F32)
    qbd = jnp.einsum("bqhd,hg->bhqgd", q4, eye).reshape(bsz, n_rows, ATT_W).astype(BF16)
    k_new = jnp.pad(jnp.swapaxes(k, 1, 2), ((0, 0), (0, 0), (0, PAGE_SIZE - t_new)))
    v_new = jnp.pad(v, ((0, 0), (0, PAGE_SIZE - t_new), (0, 0)))
    ck = jnp.transpose(cache_k, (0, 1, 3, 4, 5, 2)).reshape(-1, ATT_W, PAGE_SIZE)
    cv = cache_v.reshape(-1, PAGE_SIZE * ATT_HEADS, hd)
    pt = page_table.reshape(-1) + layer * n_pool

    def page_spec(i, rows, width):
        return pl.BlockSpec((1, rows, width),
                            lambda b, j, pt_ref: (pt_ref[b * n_pages + j * n_pg + i], 0, 0))

    per_b = lambda r: pl.BlockSpec((1, r, ATT_W), lambda b, j, pt_ref: (b, 0, 0))
    grid_spec = pltpu.PrefetchScalarGridSpec(
        num_scalar_prefetch=1,
        grid=(bsz, n_pages // n_pg),
        in_specs=([pl.BlockSpec((4, ATT_DH), lambda b, j, pt_ref: (0, 0)), per_b(n_rows)]
                  + [page_spec(i, ATT_W, PAGE_SIZE) for i in range(n_pg)]
                  + [page_spec(i, PAGE_SIZE * ATT_HEADS, hd) for i in range(n_pg)]
                  + [pl.BlockSpec((1, ATT_W, PAGE_SIZE), lambda b, j, pt_ref: (b, 0, 0)),
                     per_b(PAGE_SIZE), per_b(t_new),
                     pl.BlockSpec((1, hd), lambda b, j, pt_ref: (0, 0))]),
        out_specs=per_b(t_new),
        scratch_shapes=[pltpu.VMEM((n_rows, LANES), F32), pltpu.VMEM((n_rows, LANES), F32),
                        pltpu.VMEM((n_rows, ATT_W), F32)],
    )
    return pl.pallas_call(
        functools.partial(_attn_sample_kernel, n_pg=n_pg, t_new=t_new, lam_init=lam_init),
        grid_spec=grid_spec,
        out_shape=jax.ShapeDtypeStruct((bsz, t_new, ATT_W), F32),
        compiler_params=_cparams(2),
        name="attn_sample",
    )(pt, lam_p, qbd, *([ck] * n_pg), *([cv] * n_pg), k_new, v_new, gate,
      subln_w.reshape(1, hd))


def _ssd_kernel(x_ref, buf_ref, h0_ref, win_ref, cw_ref, cb_ref, dtb_ref, alog_ref,
                dexp_ref, nw_ref, wout_ref, lng_ref, lnb_ref,
                xo_ref, nbuf_ref, hout_ref, xpad, ubuf, z_s, y_s, *, vl):
    L = SSD_CHUNK
    c = pl.program_id(1)
    n_c = pl.num_programs(1)
    gw = SSD_HPG * SSD_HEADDIM

    @pl.when(c == 0)
    def _():
        ubuf[HIST - 3:HIST, :] = buf_ref[0]
        hout_ref[0] = h0_ref[0]
        if vl < L:
            xpad[...] = jnp.zeros((L, D_MODEL), F32)

    if vl < L:
        xpad[0:vl, :] = x_ref[0]
        x = xpad[...]
    else:
        x = x_ref[0]
    xb = x.astype(BF16)
    xbc_lo = SSD_INNER
    dt_lo = SSD_INNER + SSD_CONV_DIM
    z_s[...] = jnp.dot(xb, win_ref[:, 0:xbc_lo], preferred_element_type=F32)
    ubuf[HIST:HIST + L, :] = jnp.dot(xb, win_ref[:, xbc_lo:dt_lo], preferred_element_type=F32)
    dt_raw = jnp.dot(xb, win_ref[:, dt_lo:dt_lo + LANES], preferred_element_type=F32)

    rowL = lax.broadcasted_iota(jnp.int32, (L, LANES), 0)
    laneL = lax.broadcasted_iota(jnp.int32, (L, LANES), 1)
    dt = _softplus(dt_raw + dtb_ref[...])
    dt = jnp.where((rowL < vl) & (laneL < SSD_HEADS), dt, 0.0)
    adt = dt * (-jnp.exp(alog_ref[...]))
    tri = (lax.broadcasted_iota(jnp.int32, (L, L), 0)
           >= lax.broadcasted_iota(jnp.int32, (L, L), 1))
    cs = jnp.dot(tri.astype(F32), adt, preferred_element_type=F32,
                 precision=lax.Precision.HIGHEST)
    cs_last = cs[L - 1:L, :]
    w1 = dt * jnp.exp(cs_last - cs)
    cs_t = cs.T
    dt_t = dt.T
    w1_t = w1.T

    cw = cw_ref[...]
    lane_g = lax.broadcasted_iota(jnp.int32, (L, gw), 1)

    def conv_silu(lo, width):
        sl = slice(lo, lo + width)
        y = cb_ref[:, sl] + cw[3:4, sl] * ubuf[HIST:HIST + L, sl]
        for k in range(1, CONV_W):
            y = y + cw[3 - k:4 - k, sl] * ubuf[HIST - k:HIST - k + L, sl]
        return _silu(y)

    for g in range(SSD_GROUPS):
        xh = conv_silu(g * gw, gw)
        bm = conv_silu(SSD_INNER + g * SSD_STATE, SSD_STATE)
        cm = conv_silu(SSD_INNER + (SSD_GROUPS + g) * SSD_STATE, SSD_STATE)
        bmb = bm.astype(BF16)
        cmb = cm.astype(BF16)
        xhb = xh.astype(BF16)
        cbm = _nt(cmb, bmb)
        y_diag = jnp.zeros((L, gw), F32)
        e_cols = []
        w_rows = []
        d_rows = []
        for r in range(SSD_HPG):
            h = g * SSD_HPG + r
            cs_col = cs[:, h:h + 1]
            cs_row = cs_t[h:h + 1, :]
            lm = jnp.exp(jnp.where(tri, cs_col - cs_row, NEG_BIG))
            mat = (cbm * lm * dt_t[h:h + 1, :]).astype(BF16)
            in_head = (lane_g >= r * SSD_HEADDIM) & (lane_g < (r + 1) * SSD_HEADDIM)
            xr = jnp.where(in_head, xhb, jnp.zeros_like(xhb))
            y_diag = y_diag + jnp.dot(mat, xr, preferred_element_type=F32)
            e_cols.append(jnp.broadcast_to(jnp.exp(cs_col), (L, SSD_HEADDIM)))
            w_rows.append(jnp.broadcast_to(w1_t[h:h + 1, :], (SSD_HEADDIM, L)))
            d_rows.append(jnp.broadcast_to(jnp.exp(cs_t[h:h + 1, L - 1:L]),
                                           (SSD_HEADDIM, SSD_STATE)))
        hg = hout_ref[0, g]
        y_off = _nt(cmb, hg.astype(BF16)) * jnp.concatenate(e_cols, axis=1)
        y_s[:, g * gw:(g + 1) * gw] = y_diag + y_off + dexp_ref[:, g * gw:(g + 1) * gw] * xh
        xd_t = (xh.T * jnp.concatenate(w_rows, axis=0)).astype(BF16)
        states = jnp.dot(xd_t, bmb, preferred_element_type=F32)
        hout_ref[0, g] = hg * jnp.concatenate(d_rows, axis=0) + states

    gated = y_s[...] * _silu(z_s[...])
    ms = jnp.mean(gated * gated, axis=-1, keepdims=True)
    yn = (gated * lax.rsqrt(ms + EPS) * nw_ref[...]).astype(BF16)
    f = jnp.dot(yn, wout_ref[...], preferred_element_type=F32)
    xn = _layer_norm(DN_ALPHA * x + f, lng_ref[...], lnb_ref[...])
    xo_ref[0] = xn[0:vl, :]

    @pl.when(c == n_c - 1)
    def _():
        nbuf_ref[0] = ubuf[HIST + vl - 3:HIST + vl, :]

    ubuf[HIST - 3:HIST, :] = ubuf[HIST + L - 3:HIST + L, :]


def ssd_layer(x, conv_buf, h0, w, ln_g, ln_b):
    in_w_pad, conv_w, conv_b, dt_bias, a_log, d_skip, norm_w, out_w = w
    bsz, t_len, _ = x.shape
    L = SSD_CHUNK
    vl = L if t_len % L == 0 else t_len
    n_c = max(t_len // L, 1)
    gw = SSD_HPG * SSD_HEADDIM
    n_proj = in_w_pad.shape[1]
    pad_lanes = lambda v: jnp.pad(v.reshape(1, SSD_HEADS), ((0, 0), (0, LANES - SSD_HEADS)))
    dexp = jnp.repeat(d_skip, SSD_HEADDIM).reshape(1, SSD_INNER)
    h0g = h0.reshape(bsz, SSD_GROUPS, gw, SSD_STATE)
    const = lambda shape: pl.BlockSpec(shape, lambda b, c: (0,) * len(shape))
    resident = lambda shape: pl.BlockSpec(shape, lambda b, c: (0,) * len(shape),
                                          pipeline_mode=pl.Buffered(1))
    xn, nbuf, hout = pl.pallas_call(
        functools.partial(_ssd_kernel, vl=vl),
        grid=(bsz, n_c),
        in_specs=[pl.BlockSpec((1, vl, D_MODEL), lambda b, c: (b, c, 0)),
                  pl.BlockSpec((1, CONV_W - 1, SSD_CONV_DIM), lambda b, c: (b, 0, 0)),
                  pl.BlockSpec((1, SSD_GROUPS, gw, SSD_STATE), lambda b, c: (b, 0, 0, 0)),
                  resident((D_MODEL, n_proj)),
                  const((CONV_W, SSD_CONV_DIM)), const((1, SSD_CONV_DIM)),
                  const((1, LANES)), const((1, LANES)),
                  const((1, SSD_INNER)), const((1, SSD_INNER)),
                  resident((SSD_INNER, D_MODEL)),
                  const((1, D_MODEL)), const((1, D_MODEL))],
        out_specs=[pl.BlockSpec((1, vl, D_MODEL), lambda b, c: (b, c, 0)),
                   pl.BlockSpec((1, CONV_W - 1, SSD_CONV_DIM), lambda b, c: (b, 0, 0)),
                   pl.BlockSpec((1, SSD_GROUPS, gw, SSD_STATE), lambda b, c: (b, 0, 0, 0))],
        out_shape=[jax.ShapeDtypeStruct((bsz, t_len, D_MODEL), F32),
                   jax.ShapeDtypeStruct((bsz, CONV_W - 1, SSD_CONV_DIM), F32),
                   jax.ShapeDtypeStruct((bsz, SSD_GROUPS, gw, SSD_STATE), F32)],
        scratch_shapes=[pltpu.VMEM((L, D_MODEL), F32),
                        pltpu.VMEM((HIST + L, SSD_CONV_DIM), F32),
                        pltpu.VMEM((L, SSD_INNER), F32),
                        pltpu.VMEM((L, SSD_INNER), F32)],
        compiler_params=_cparams(2),
        name="ssd_layer",
    )(x, conv_buf, h0g, in_w_pad, conv_w, conv_b.reshape(1, SSD_CONV_DIM),
      pad_lanes(dt_bias), pad_lanes(a_log), dexp, norm_w.reshape(1, SSD_INNER), out_w,
      ln_g.reshape(1, D_MODEL), ln_b.reshape(1, D_MODEL))
    return xn, nbuf, hout.reshape(bsz, SSD_HEADS, SSD_HEADDIM, SSD_STATE)


def _row_tile(rows):
    return 256 if rows % 256 == 0 else rows


def _attn_proj_t_kernel(x_ref, w_ref, wkt_ref, q_ref, kt_ref, v_ref, g_ref):
    x = x_ref[0].astype(BF16)
    q_ref[0] = jnp.dot(x, w_ref[:, 0:ATT_W], preferred_element_type=F32)
    kt_ref[0] = _nt(wkt_ref[...], x)
    v_ref[0] = jnp.dot(x, w_ref[:, 2 * ATT_W:3 * ATT_W], preferred_element_type=F32)
    g_ref[0] = jnp.dot(x, w_ref[:, 3 * ATT_W:4 * ATT_W], preferred_element_type=F32)


def attn_project_t(x, in_w, wkt, tm=256):
    bsz, t_len, _ = x.shape
    rows = lambda: pl.BlockSpec((1, tm, ATT_W), lambda b, i: (b, i, 0))
    return pl.pallas_call(
        _attn_proj_t_kernel,
        grid=(bsz, t_len // tm),
        in_specs=[pl.BlockSpec((1, tm, D_MODEL), lambda b, i: (b, i, 0)),
                  pl.BlockSpec((D_MODEL, 4 * ATT_W), lambda b, i: (0, 0)),
                  pl.BlockSpec((ATT_W, D_MODEL), lambda b, i: (0, 0))],
        out_specs=[rows(), pl.BlockSpec((1, ATT_W, tm), lambda b, i: (b, 0, i)), rows(), rows()],
        out_shape=[jax.ShapeDtypeStruct((bsz, t_len, ATT_W), F32),
                   jax.ShapeDtypeStruct((bsz, ATT_W, t_len), F32),
                   jax.ShapeDtypeStruct((bsz, t_len, ATT_W), F32),
                   jax.ShapeDtypeStruct((bsz, t_len, ATT_W), F32)],
        compiler_params=_cparams(2),
        name="attn_project_t",
    )(x, in_w, wkt)


def _attn_project(x, in_w):
    bsz, t_len, _ = x.shape
    x2 = x.reshape(bsz * t_len, D_MODEL)
    tm = _row_tile(bsz * t_len)
    splits = tuple((i * ATT_W, ATT_W) for i in range(4))
    q, k, v, gate = proj_split(x2, in_w, splits, tm)
    shp = (bsz, t_len, ATT_W)
    return x2, tm, q.reshape(shp), k.reshape(shp), v.reshape(shp), gate.reshape(shp)


def kernel(x_prompt, x_sample, cache_k, cache_v, page_table, state_lru_conv, state_lru_h, state_ssd_conv, state_ssd_h, ln_g, ln_b, a_in_w, a_conv_w, a_conv_b, a_gate_r_w, a_gate_r_b, a_gate_i_w, a_gate_i_b, a_lambda, a_out_w, b_in_w, b_lambda, b_subln_w, b_out_w, c_in_w, c_conv_w, c_conv_b, c_dt_bias, c_a_log, c_d, c_norm_w, c_out_w):
    xp, xs = x_prompt, x_sample
    bp, bs = xp.shape[0], xs.shape[0]
    tp, ts = xp.shape[1], xs.shape[1]
    k_p, v_p, k_s, v_s = [], [], [], []
    lc_p, lh_p, lc_s, lh_s = [], [], [], []
    sc_p, sh_p, sc_s, sh_s = [], [], [], []
    for i in range(DEPTH):
        j = i // N_MIXERS
        kind = i % N_MIXERS
        if kind == 0:
            w = (a_in_w[j].astype(BF16), a_conv_w[j], a_conv_b[j], _block_diag_tiles(a_gate_r_w[j]),
                 a_gate_r_b[j], _block_diag_tiles(a_gate_i_w[j]), a_gate_i_b[j], a_lambda[j],
                 a_out_w[j].astype(BF16))
            zc = jnp.zeros((bp, CONV_W - 1, LRU_W), F32)
            zh = jnp.zeros((bp, LRU_W), F32)
            xp, c1, h1 = rglru_layer(xp, zc, zh, w, ln_g[i], ln_b[i])
            xs, c2, h2 = rglru_layer(xs, state_lru_conv[j], state_lru_h[j], w, ln_g[i], ln_b[i])
            lc_p.append(c1); lh_p.append(h1); lc_s.append(c2); lh_s.append(h2)
        elif kind == 1:
            lam_init = 0.8 - 0.6 * math.exp(-0.3 * i)
            in_w = b_in_w[j].astype(BF16)
            out_w = b_out_w[j].astype(BF16)
            wkt = b_in_w[j][:, ATT_W:2 * ATT_W].T.astype(BF16)
            q, kt, v, g = attn_project_t(xp, in_w, wkt)
            og = attn_prompt(q, kt, v, g, b_lambda[j], b_subln_w[j], lam_init)
            xp = outproj_ln(og.reshape(bp * tp, ATT_W), out_w, xp.reshape(bp * tp, D_MODEL),
                            ln_g[i], ln_b[i], _row_tile(bp * tp)).reshape(bp, tp, D_MODEL)
            k_p.append(jnp.transpose(kt.reshape(bp, ATT_HEADS, 2, ATT_DH, tp), (0, 4, 1, 2, 3)))
            v_p.append(v.reshape(bp, tp, ATT_HEADS, 2 * ATT_DH))
            x2, tm, q, k, v, g = _attn_project(xs, in_w)
            og = attn_sample(q, k, v, g, cache_k, cache_v, j, page_table, b_lambda[j],
                             b_subln_w[j], lam_init)
            xs = outproj_ln(og.reshape(bs * ts, ATT_W), out_w, x2, ln_g[i], ln_b[i], tm
                            ).reshape(bs, ts, D_MODEL)
            k_s.append(k.reshape(bs, ts, ATT_HEADS, 2, ATT_DH))
            v_s.append(v.reshape(bs, ts, ATT_HEADS, 2 * ATT_DH))
        else:
            in_w_pad = jnp.pad(c_in_w[j], ((0, 0), (0, LANES - SSD_HEADS))).astype(BF16)
            w = (in_w_pad, c_conv_w[j], c_conv_b[j], c_dt_bias[j], c_a_log[j], c_d[j], c_norm_w[j],
                 c_out_w[j].astype(BF16))
            zc = jnp.zeros((bp, CONV_W - 1, SSD_CONV_DIM), F32)
            zh = jnp.zeros((bp, SSD_HEADS, SSD_HEADDIM, SSD_STATE), F32)
            xp, c1, h1 = ssd_layer(xp, zc, zh, w, ln_g[i], ln_b[i])
            xs, c2, h2 = ssd_layer(xs, state_ssd_conv[j], state_ssd_h[j], w, ln_g[i], ln_b[i])
            sc_p.append(c1); sh_p.append(h1); sc_s.append(c2); sh_s.append(h2)
    return (xp, xs, jnp.stack(k_p), jnp.stack(v_p), jnp.stack(k_s), jnp.stack(v_s),
            jnp.stack(lc_p), jnp.stack(lh_p), jnp.stack(lc_s), jnp.stack(lh_s),
            jnp.stack(sc_p), jnp.stack(sh_p), jnp.stack(sc_s), jnp.stack(sh_s))
```

```python
import functools
import math

import jax
import jax.numpy as jnp
from jax import lax
from jax.experimental import pallas as pl
from jax.experimental.pallas import tpu as pltpu

F32 = jnp.float32
BF16 = jnp.bfloat16

D_MODEL = 1024
DEPTH = 4
PAGE_SIZE = 128
N_MIXERS = 3
DN_ALPHA = (2.0 * DEPTH) ** 0.25
EPS = 1e-5
CONV_W = 4
LRU_W = D_MODEL
LRU_BLOCKS = 16
LRU_BS = LRU_W // LRU_BLOCKS
LRU_C = 8.0
ATT_HEADS = 8
ATT_DH = D_MODEL // (2 * ATT_HEADS)
ATT_W = ATT_HEADS * 2 * ATT_DH
SSD_INNER = 2 * D_MODEL
SSD_HEADDIM = 64
SSD_HEADS = SSD_INNER // SSD_HEADDIM
SSD_GROUPS = 8
SSD_HPG = SSD_HEADS // SSD_GROUPS
SSD_STATE = 128
SSD_CONV_DIM = SSD_INNER + 2 * SSD_GROUPS * SSD_STATE
SSD_CHUNK = 128

LANES = 128
SUBLANES = 8
MXU_DIM = 256
VMEM_LIMIT = 56 * 1024 * 1024
LOG2E = math.log2(math.e)
F32_TINY = 1.1754944e-38
NEG_BIG = -1e30
HIST = SUBLANES


def _cparams(n_grid):
    return pltpu.CompilerParams(dimension_semantics=("arbitrary",) * n_grid,
                                vmem_limit_bytes=VMEM_LIMIT)


def _nt(a, b):
    return lax.dot_general(a, b, (((1,), (1,)), ((), ())), preferred_element_type=F32)


def _sigmoid(x):
    return 0.5 * jnp.tanh(0.5 * x) + 0.5


def _sqrt_nonneg(x):
    return x * lax.rsqrt(jnp.maximum(x, F32_TINY))


def _layer_norm(v, g, b):
    mu = jnp.mean(v, axis=-1, keepdims=True)
    d = v - mu
    var = jnp.mean(d * d, axis=-1, keepdims=True)
    return d * lax.rsqrt(var + EPS) * g + b


def _silu(x):
    return x * _sigmoid(x)


def _softplus(x):
    return jnp.maximum(x, 0.0) + jnp.log(1.0 + jnp.exp(-jnp.abs(x)))


def _proj_kernel(x_ref, w_ref, *out_refs, splits):
    x = x_ref[...].astype(BF16)
    for (off, width), o_ref in zip(splits, out_refs):
        o_ref[...] = jnp.dot(x, w_ref[:, off:off + width], preferred_element_type=F32)


def proj_split(x2d, w_bf16, splits, tm):
    rows, kdim = x2d.shape
    ndim = w_bf16.shape[1]
    return pl.pallas_call(
        functools.partial(_proj_kernel, splits=splits),
        grid=(rows // tm,),
        in_specs=[pl.BlockSpec((tm, kdim), lambda i: (i, 0)),
                  pl.BlockSpec((kdim, ndim), lambda i: (0, 0))],
        out_specs=[pl.BlockSpec((tm, wd), lambda i: (i, 0)) for _, wd in splits],
        out_shape=[jax.ShapeDtypeStruct((rows, wd), F32) for _, wd in splits],
        compiler_params=_cparams(1),
        name="proj_split",
    )(x2d, w_bf16)


def _outproj_ln_kernel(y_ref, w_ref, x_ref, g_ref, b_ref, o_ref):
    f = jnp.dot(y_ref[...].astype(BF16), w_ref[...], preferred_element_type=F32)
    v = DN_ALPHA * x_ref[...] + f
    mu = jnp.mean(v, axis=-1, keepdims=True)
    d = v - mu
    var = jnp.mean(d * d, axis=-1, keepdims=True)
    o_ref[...] = d * lax.rsqrt(var + EPS) * g_ref[...] + b_ref[...]


def outproj_ln(y2d, w_bf16, x2d, g, b, tm):
    rows, kdim = y2d.shape
    return pl.pallas_call(
        _outproj_ln_kernel,
        grid=(rows // tm,),
        in_specs=[pl.BlockSpec((tm, kdim), lambda i: (i, 0)),
                  pl.BlockSpec((kdim, D_MODEL), lambda i: (0, 0)),
                  pl.BlockSpec((tm, D_MODEL), lambda i: (i, 0)),
                  pl.BlockSpec((1, D_MODEL), lambda i: (0, 0)),
                  pl.BlockSpec((1, D_MODEL), lambda i: (0, 0))],
        out_specs=pl.BlockSpec((tm, D_MODEL), lambda i: (i, 0)),
        out_shape=jax.ShapeDtypeStruct((rows, D_MODEL), F32),
        compiler_params=_cparams(1),
        name="outproj_ln",
    )(y2d, w_bf16, x2d, g.reshape(1, D_MODEL), b.reshape(1, D_MODEL))


def _rglru_kernel(x_ref, buf_ref, h0_ref, win_ref, cw_ref, cb_ref, wr_ref, rb_ref,
                  wi_ref, ib_ref, lam_ref, wout_ref, lng_ref, lnb_ref,
                  xo_ref, nbuf_ref, hlast_ref,
                  xpad, ubuf, g_s, a_s, b_s, hcar, *, tt, valid):
    t = pl.program_id(1)
    n_t = pl.num_programs(1)

    @pl.when(t == 0)
    def _():
        ubuf[HIST - 3:HIST, :] = buf_ref[0]
        hcar[...] = h0_ref[0]
        if valid < tt:
            xpad[...] = jnp.zeros((tt, D_MODEL), F32)

    if valid < tt:
        xpad[0:valid, :] = x_ref[0]
        x = xpad[...]
    else:
        x = x_ref[0]
    xb = x.astype(BF16)
    g_s[...] = jnp.dot(xb, win_ref[:, 0:LRU_W], preferred_element_type=F32)
    ubuf[HIST:HIST + tt, :] = jnp.dot(xb, win_ref[:, LRU_W:2 * LRU_W], preferred_element_type=F32)
    cw = cw_ref[...]
    y = cb_ref[...] + cw[3:4, :] * ubuf[HIST:HIST + tt, :]
    for k in range(1, CONV_W):
        y = y + cw[3 - k:4 - k, :] * ubuf[HIST - k:HIST - k + tt, :]

    sp = _softplus(-lam_ref[...])
    yb = y.astype(BF16)
    for j in range(LRU_W // MXU_DIM):
        sl = slice(j * MXU_DIM, (j + 1) * MXU_DIM)
        ys = yb[:, sl]
        r = _sigmoid(jnp.dot(ys, wr_ref[j], preferred_element_type=F32) + rb_ref[:, sl])
        ig = _sigmoid(jnp.dot(ys, wi_ref[j], preferred_element_type=F32) + ib_ref[:, sl])
        log_a = -LRU_C * r * sp[:, sl]
        a = jnp.exp(log_a)
        a_s[:, sl] = a
        b_s[:, sl] = _sqrt_nonneg(-jnp.tanh(log_a) * (a * a + 1.0)) * (ig * y[:, sl])

    row = lax.broadcasted_iota(jnp.int32, (SUBLANES, LRU_W), 0)

    def body(g, hc):
        r0 = pl.multiple_of(g * SUBLANES, SUBLANES)
        a = a_s[pl.ds(r0, SUBLANES), :]
        b = b_s[pl.ds(r0, SUBLANES), :]
        for d in (1, 2, 4):
            a_sh = pltpu.roll(a, d, 0)
            b_sh = pltpu.roll(b, d, 0)
            m = row >= d
            b = jnp.where(m, a * b_sh + b, b)
            a = jnp.where(m, a * a_sh, a)
        h = a * hc + b
        b_s[pl.ds(r0, SUBLANES), :] = h
        return h[SUBLANES - 1:SUBLANES, :]

    lax.fori_loop(0, tt // SUBLANES, body, hcar[...])
    h_end = b_s[valid - 1:valid, :]
    hcar[...] = h_end
    hg = (b_s[...] * _silu(g_s[...])).astype(BF16)
    f = jnp.dot(hg, wout_ref[...], preferred_element_type=F32)
    xn = _layer_norm(DN_ALPHA * x + f, lng_ref[...], lnb_ref[...])
    xo_ref[0] = xn[0:valid, :]

    @pl.when(t == n_t - 1)
    def _():
        nbuf_ref[0] = ubuf[HIST + valid - 3:HIST + valid, :]
        hlast_ref[0] = h_end

    ubuf[HIST - 3:HIST, :] = ubuf[HIST + tt - 3:HIST + tt, :]


def rglru_layer(x, conv_buf, h0, w, ln_g, ln_b):
    in_w, conv_w, conv_b, wr_bd, r_b, wi_bd, i_b, lam, out_w = w
    bsz, t_len, _ = x.shape
    if t_len % 256 == 0:
        tt, valid = 256, 256
    else:
        tt, valid = 16, t_len
    n_t = max(t_len // tt, 1)
    row = lambda v: v.reshape(1, LRU_W)
    vec = lambda: pl.BlockSpec((1, LRU_W), lambda b, t: (0, 0))
    wspec = lambda: pl.BlockSpec((LRU_W // MXU_DIM, MXU_DIM, MXU_DIM), lambda b, t: (0, 0, 0))
    xn, nbuf, hlast = pl.pallas_call(
        functools.partial(_rglru_kernel, tt=tt, valid=valid),
        grid=(bsz, n_t),
        in_specs=[pl.BlockSpec((1, valid, D_MODEL), lambda b, t: (b, t, 0)),
                  pl.BlockSpec((1, CONV_W - 1, LRU_W), lambda b, t: (b, 0, 0)),
                  pl.BlockSpec((1, 1, LRU_W), lambda b, t: (b, 0, 0)),
                  pl.BlockSpec((D_MODEL, 2 * LRU_W), lambda b, t: (0, 0)),
                  pl.BlockSpec((CONV_W, LRU_W), lambda b, t: (0, 0)),
                  vec(), wspec(), vec(), wspec(), vec(), vec(),
                  pl.BlockSpec((LRU_W, D_MODEL), lambda b, t: (0, 0)),
                  vec(), vec()],
        out_specs=[pl.BlockSpec((1, valid, D_MODEL), lambda b, t: (b, t, 0)),
                   pl.BlockSpec((1, CONV_W - 1, LRU_W), lambda b, t: (b, 0, 0)),
                   pl.BlockSpec((1, 1, LRU_W), lambda b, t: (b, 0, 0))],
        out_shape=[jax.ShapeDtypeStruct((bsz, t_len, D_MODEL), F32),
                   jax.ShapeDtypeStruct((bsz, CONV_W - 1, LRU_W), F32),
                   jax.ShapeDtypeStruct((bsz, 1, LRU_W), F32)],
        scratch_shapes=[pltpu.VMEM((tt, D_MODEL), F32),
                        pltpu.VMEM((HIST + tt, LRU_W), F32),
                        pltpu.VMEM((tt, LRU_W), F32),
                        pltpu.VMEM((tt, LRU_W), F32),
                        pltpu.VMEM((tt, LRU_W), F32),
                        pltpu.VMEM((1, LRU_W), F32)],
        compiler_params=_cparams(2),
        name="rglru_layer",
    )(x, conv_buf, h0.reshape(bsz, 1, LRU_W), in_w, conv_w, row(conv_b), wr_bd, row(r_b),
      wi_bd, row(i_b), row(lam), out_w, row(ln_g), row(ln_b))
    return xn, nbuf, hlast.reshape(bsz, LRU_W)


def _block_diag_tiles(w):
    per = MXU_DIM // LRU_BS
    w4 = w.reshape(LRU_W // MXU_DIM, per, LRU_BS, LRU_BS)
    eye = jnp.eye(per, dtype=w.dtype)
    t = jnp.einsum("jakc,ab->jakbc", w4, eye)
    return t.reshape(LRU_W // MXU_DIM, MXU_DIM, MXU_DIM).astype(BF16)


def _diff_lambda(lp, lam_init):
    s1 = jnp.sum(lp[0:1, :] * lp[1:2, :], axis=-1, keepdims=True)
    s2 = jnp.sum(lp[2:3, :] * lp[3:4, :], axis=-1, keepdims=True)
    return jnp.exp(s1) - jnp.exp(s2) + lam_init


def _subln_gate(o, sw, gate, lam_init):
    ms = jnp.mean(o * o, axis=-1, keepdims=True)
    return (o * lax.rsqrt(ms + EPS) * sw * (1.0 - lam_init)) * _silu(gate)


def _attn_prompt_kernel(lp_ref, q_ref, kt_ref, v_ref, gate_ref, sw_ref, o_ref,
                        kb_s, vx_s, m_s, accl_s, s_s, *, tq, lam_init):
    qi = pl.program_id(2)
    hd = 2 * ATT_DH
    t_len = v_ref.shape[1]

    @pl.when(qi == 0)
    def _():
        for i in range(t_len // tq):
            kb_s[i] = kt_ref[0, :, i * tq:(i + 1) * tq].astype(BF16)

        def cvt(i, carry):
            r = pl.multiple_of(i * tq, tq)
            vx_s[pl.ds(r, tq), 0:hd] = v_ref[0, pl.ds(r, tq), :].astype(BF16)
            vx_s[pl.ds(r, tq), hd:2 * hd] = jnp.ones((tq, hd), BF16)
            return carry
        lax.fori_loop(0, t_len // tq, cvt, 0)

    lam = _diff_lambda(lp_ref[...], lam_init)
    q = q_ref[0] * (ATT_DH ** -0.5 * LOG2E)
    lane = lax.broadcasted_iota(jnp.int32, (tq, hd), 1)
    q_maps = (jnp.where(lane < ATT_DH, q, 0.0).astype(BF16),
              jnp.where(lane >= ATT_DH, q, 0.0).astype(BF16))
    m_s[...] = jnp.full(m_s.shape, NEG_BIG, F32)
    accl_s[...] = jnp.zeros(accl_s.shape, F32)
    rowi = lax.broadcasted_iota(jnp.int32, (tq, tq), 0)
    coli = lax.broadcasted_iota(jnp.int32, (tq, tq), 1)

    def scores(ki, slot):
        kb = kb_s[ki]
        for c in range(2):
            s_s[slot, c] = jnp.dot(q_maps[c], kb, preferred_element_type=F32)

    def absorb(ki, slot, masked):
        ks = pl.multiple_of(ki * tq, tq)
        vx = vx_s[pl.ds(ks, tq), :]
        for c in range(2):
            s = s_s[slot, c]
            if masked:
                s = jnp.where(coli <= rowi, s, NEG_BIG)
            m_prev = m_s[c]
            m_new = jnp.maximum(m_prev, jnp.max(s, axis=-1, keepdims=True))
            alpha = jnp.exp2(m_prev - m_new)
            p = jnp.exp2(s - jnp.concatenate([m_new] * (tq // hd), axis=1))
            pv = jnp.dot(p.astype(BF16), vx, preferred_element_type=F32)
            accl_s[c] = jnp.concatenate([alpha, alpha], axis=1) * accl_s[c] + pv
            m_s[c] = m_new

    scores(0, 0)

    def pair(j, carry):
        k0 = 2 * j
        scores(k0 + 1, 1)
        absorb(k0, 0, False)
        scores(k0 + 2, 0)
        absorb(k0 + 1, 1, False)
        return carry

    lax.fori_loop(0, qi // 2, pair, 0)

    @pl.when(qi % 2 == 0)
    def _():
        absorb(qi, 0, True)

    @pl.when(qi % 2 == 1)
    def _():
        scores(qi, 1)
        absorb(qi - 1, 0, False)
        absorb(qi, 1, True)

    a1 = accl_s[0]
    a2 = accl_s[1]
    o = a1[:, 0:hd] / a1[:, hd:2 * hd] - lam * (a2[:, 0:hd] / a2[:, hd:2 * hd])
    o_ref[0] = _subln_gate(o, sw_ref[...], gate_ref[0], lam_init)


def attn_prompt(q, kt, v, gate, lam_p, subln_w, lam_init, tq=512):
    bsz, t_len, _ = q.shape
    hd = 2 * ATT_DH
    qspec = lambda: pl.BlockSpec((1, tq, hd), lambda b, h, i: (b, i, h))
    return pl.pallas_call(
        functools.partial(_attn_prompt_kernel, tq=tq, lam_init=lam_init),
        grid=(bsz, ATT_HEADS, t_len // tq),
        in_specs=[pl.BlockSpec((4, ATT_DH), lambda b, h, i: (0, 0)),
                  qspec(),
                  pl.BlockSpec((1, hd, t_len), lambda b, h, i: (b, h, 0)),
                  pl.BlockSpec((1, t_len, hd), lambda b, h, i: (b, 0, h)),
                  qspec(),
                  pl.BlockSpec((1, hd), lambda b, h, i: (0, 0))],
        out_specs=qspec(),
        out_shape=jax.ShapeDtypeStruct((bsz, t_len, ATT_W), F32),
        scratch_shapes=[pltpu.VMEM((t_len // tq, hd, tq), BF16),
                        pltpu.VMEM((t_len, 2 * hd), BF16),
                        pltpu.VMEM((2, tq, hd), F32), pltpu.VMEM((2, tq, 2 * hd), F32),
                        pltpu.VMEM((2, 2, tq, tq), F32)],
        compiler_params=_cparams(3),
        name="attn_prompt",
    )(lam_p, q, kt, v, gate, subln_w.reshape(1, hd))


def _attn_sample_kernel(pt_ref, lp_ref, qbd_ref, *refs, n_pg, t_new, lam_init):
    k_refs = refs[:n_pg]
    v_refs = refs[n_pg:2 * n_pg]
    knew_ref, vnew_ref, gate_ref, sw_ref, o_ref, m_s, l_s, acc_s = refs[2 * n_pg:]
    j = pl.program_id(1)
    n_j = pl.num_programs(1)
    n_rows = ATT_HEADS * 2 * t_new
    hd = 2 * ATT_DH

    @pl.when(j == 0)
    def _():
        m_s[...] = jnp.full(m_s.shape, NEG_BIG, F32)
        l_s[...] = jnp.zeros(l_s.shape, F32)
        acc_s[...] = jnp.zeros(acc_s.shape, F32)

    qbd = qbd_ref[0]

    def attend(kts, vbs, mask):
        n = len(kts)
        s = jnp.concatenate([jnp.dot(qbd, kt.astype(BF16), preferred_element_type=F32)
                             for kt in kts], axis=1)
        if mask is not None:
            s = jnp.where(mask, s, NEG_BIG)
        m_prev = m_s[...]
        m_new = jnp.maximum(m_prev, jnp.max(s, axis=-1, keepdims=True))
        alpha = jnp.exp2(m_prev - m_new)
        p = jnp.exp2(s - jnp.concatenate([m_new] * n, axis=1))
        l_s[...] = alpha * l_s[...] + jnp.sum(p, axis=-1, keepdims=True)
        pb = p.astype(BF16)
        pv = jnp.dot(pb[:, 0:PAGE_SIZE], vbs[0].astype(BF16), preferred_element_type=F32)
        for i in range(1, n):
            pv = pv + jnp.dot(pb[:, i * PAGE_SIZE:(i + 1) * PAGE_SIZE], vbs[i].astype(BF16),
                              preferred_element_type=F32)
        acc_s[...] = jnp.concatenate([alpha] * (ATT_W // LANES), axis=1) * acc_s[...] + pv
        m_s[...] = m_new

    def page_v(vr):
        return jnp.concatenate([vr[0, pl.ds(h, PAGE_SIZE, stride=ATT_HEADS), :]
                                for h in range(ATT_HEADS)], axis=1)

    attend([r[0] for r in k_refs], [page_v(r) for r in v_refs], None)

    @pl.when(j == n_j - 1)
    def _():
        rowi = lax.broadcasted_iota(jnp.int32, (n_rows, PAGE_SIZE), 0)
        coli = lax.broadcasted_iota(jnp.int32, (n_rows, PAGE_SIZE), 1)
        attend([knew_ref[0]], [vnew_ref[0]], coli <= (rowi % t_new))
        lam = _diff_lambda(lp_ref[...], lam_init)
        sw = sw_ref[...]
        for h in range(ATT_HEADS):
            r1 = h * 2 * t_new
            r2 = r1 + t_new
            cs = slice(h * hd, (h + 1) * hd)
            o1 = acc_s[r1:r1 + t_new, cs] / l_s[r1:r1 + t_new, :]
            o2 = acc_s[r2:r2 + t_new, cs] / l_s[r2:r2 + t_new, :]
            o_ref[0, :, cs] = _subln_gate(o1 - lam * o2, sw, gate_ref[0, :, cs], lam_init)


def attn_sample(q, k, v, gate, cache_k, cache_v, layer, page_table, lam_p, subln_w, lam_init,
                n_pg=8):
    bsz, t_new, _ = q.shape
    n_pages = page_table.shape[1]
    n_pool = cache_k.shape[1]
    hd = 2 * ATT_DH
    n_rows = ATT_HEADS * 2 * t_new
    q4 = (q * (ATT_DH ** -0.5 * LOG2E)).reshape(bsz, t_new, 2 * ATT_HEADS, ATT_DH)
    eye = jnp.eye(2 * ATT_HEADS, dtype=F32)
    qbd = jnp.einsum("bqhd,hg->bhqgd", q4, eye).reshape(bsz, n_rows, ATT_W).astype(BF16)
    k_new = jnp.pad(jnp.swapaxes(k, 1, 2), ((0, 0), (0, 0), (0, PAGE_SIZE - t_new)))
    v_new = jnp.pad(v, ((0, 0), (0, PAGE_SIZE - t_new), (0, 0)))
    ck = jnp.transpose(cache_k, (0, 1, 3, 4, 5, 2)).reshape(-1, ATT_W, PAGE_SIZE)
    cv = cache_v.reshape(-1, PAGE_SIZE * ATT_HEADS, hd)
    pt = page_table.reshape(-1) + layer * n_pool

    def page_spec(i, rows, width):
        return pl.BlockSpec((1, rows, width),
                            lambda b, j, pt_ref: (pt_ref[b * n_pages + j * n_pg + i], 0, 0))

    per_b = lambda r: pl.BlockSpec((1, r, ATT_W), lambda b, j, pt_ref: (b, 0, 0))
    grid_spec = pltpu.PrefetchScalarGridSpec(
        num_scalar_prefetch=1,
        grid=(bsz, n_pages // n_pg),
        in_specs=([pl.BlockSpec((4, ATT_DH), lambda b, j, pt_ref: (0, 0)), per_b(n_rows)]
                  + [page_spec(i, ATT_W, PAGE_SIZE) for i in range(n_pg)]
                  + [page_spec(i, PAGE_SIZE * ATT_HEADS, hd) for i in range(n_pg)]
                  + [pl.BlockSpec((1, ATT_W, PAGE_SIZE), lambda b, j, pt_ref: (b, 0, 0)),
                     per_b(PAGE_SIZE), per_b(t_new),
                     pl.BlockSpec((1, hd), lambda b, j, pt_ref: (0, 0))]),
        out_specs=per_b(t_new),
        scratch_shapes=[pltpu.VMEM((n_rows, LANES), F32), pltpu.VMEM((n_rows, LANES), F32),
                        pltpu.VMEM((n_rows, ATT_W), F32)],
    )
    return pl.pallas_call(
        functools.partial(_attn_sample_kernel, n_pg=n_pg, t_new=t_new, lam_init=lam_init),
        grid_spec=grid_spec,
        out_shape=jax.ShapeDtypeStruct((bsz, t_new, ATT_W), F32),
        compiler_params=_cparams(2),
        name="attn_sample",
    )(pt, lam_p, qbd, *([ck] * n_pg), *([cv] * n_pg), k_new, v_new, gate,
      subln_w.reshape(1, hd))


def _ssd_kernel(x_ref, buf_ref, h0_ref, win_ref, cw_ref, cb_ref, dtb_ref, alog_ref,
                dexp_ref, nw_ref, wout_ref, lng_ref, lnb_ref,
                xo_ref, nbuf_ref, hout_ref, xpad, ubuf, z_s, y_s, *, vl):
    L = SSD_CHUNK
    c = pl.program_id(1)
    n_c = pl.num_programs(1)
    gw = SSD_HPG * SSD_HEADDIM

    @pl.when(c == 0)
    def _():
        ubuf[HIST - 3:HIST, :] = buf_ref[0]
        hout_ref[0] = h0_ref[0]
        if vl < L:
            xpad[...] = jnp.zeros((L, D_MODEL), F32)

    if vl < L:
        xpad[0:vl, :] = x_ref[0]
        x = xpad[...]
    else:
        x = x_ref[0]
    xb = x.astype(BF16)
    xbc_lo = SSD_INNER
    dt_lo = SSD_INNER + SSD_CONV_DIM
    z_s[...] = jnp.dot(xb, win_ref[:, 0:xbc_lo], preferred_element_type=F32)
    ubuf[HIST:HIST + L, :] = jnp.dot(xb, win_ref[:, xbc_lo:dt_lo], preferred_element_type=F32)
    dt_raw = jnp.dot(xb, win_ref[:, dt_lo:dt_lo + LANES], preferred_element_type=F32)

    rowL = lax.broadcasted_iota(jnp.int32, (L, LANES), 0)
    laneL = lax.broadcasted_iota(jnp.int32, (L, LANES), 1)
    dt = _softplus(dt_raw + dtb_ref[...])
    dt = jnp.where((rowL < vl) & (laneL < SSD_HEADS), dt, 0.0)
    adt = dt * (-jnp.exp(alog_ref[...]))
    tri = (lax.broadcasted_iota(jnp.int32, (L, L), 0)
           >= lax.broadcasted_iota(jnp.int32, (L, L), 1))
    cs = jnp.dot(tri.astype(F32), adt, preferred_element_type=F32,
                 precision=lax.Precision.HIGHEST)
    cs_last = cs[L - 1:L, :]
    w1 = dt * jnp.exp(cs_last - cs)
    cs_t = cs.T
    dt_t = dt.T
    w1_t = w1.T

    cw = cw_ref[...]
    lane_g = lax.broadcasted_iota(jnp.int32, (L, gw), 1)

    def conv_silu(lo, width):
        sl = slice(lo, lo + width)
        y = cb_ref[:, sl] + cw[3:4, sl] * ubuf[HIST:HIST + L, sl]
        for k in range(1, CONV_W):
            y = y + cw[3 - k:4 - k, sl] * ubuf[HIST - k:HIST - k + L, sl]
        return _silu(y)

    for g in range(SSD_GROUPS):
        xh = conv_silu(g * gw, gw)
        bm = conv_silu(SSD_INNER + g * SSD_STATE, SSD_STATE)
        cm = conv_silu(SSD_INNER + (SSD_GROUPS + g) * SSD_STATE, SSD_STATE)
        bmb = bm.astype(BF16)
        cmb = cm.astype(BF16)
        xhb = xh.astype(BF16)
        cbm = _nt(cmb, bmb)
        y_diag = jnp.zeros((L, gw), F32)
        e_cols = []
        w_rows = []
        d_rows = []
        for r in range(SSD_HPG):
            h = g * SSD_HPG + r
            cs_col = cs[:, h:h + 1]
            cs_row = cs_t[h:h + 1, :]
            lm = jnp.exp(jnp.where(tri, cs_col - cs_row, NEG_BIG))
            mat = (cbm * lm * dt_t[h:h + 1, :]).astype(BF16)
            in_head = (lane_g >= r * SSD_HEADDIM) & (lane_g < (r + 1) * SSD_HEADDIM)
            xr = jnp.where(in_head, xhb, jnp.zeros_like(xhb))
            y_diag = y_diag + jnp.dot(mat, xr, preferred_element_type=F32)
            e_cols.append(jnp.broadcast_to(jnp.exp(cs_col), (L, SSD_HEADDIM)))
            w_rows.append(jnp.broadcast_to(w1_t[h:h + 1, :], (SSD_HEADDIM, L)))
            d_rows.append(jnp.broadcast_to(jnp.exp(cs_t[h:h + 1, L - 1:L]),
                                           (SSD_HEADDIM, SSD_STATE)))
        hg = hout_ref[0, g]
        y_off = _nt(cmb, hg.astype(BF16)) * jnp.concatenate(e_cols, axis=1)
        y_s[:, g * gw:(g + 1) * gw] = y_diag + y_off + dexp_ref[:, g * gw:(g + 1) * gw] * xh
        xd_t = (xh.T * jnp.concatenate(w_rows, axis=0)).astype(BF16)
        states = jnp.dot(xd_t, bmb, preferred_element_type=F32)
        hout_ref[0, g] = hg * jnp.concatenate(d_rows, axis=0) + states

    gated = y_s[...] * _silu(z_s[...])
    ms = jnp.mean(gated * gated, axis=-1, keepdims=True)
    yn = (gated * lax.rsqrt(ms + EPS) * nw_ref[...]).astype(BF16)
    f = jnp.dot(yn, wout_ref[...], preferred_element_type=F32)
    xn = _layer_norm(DN_ALPHA * x + f, lng_ref[...], lnb_ref[...])
    xo_ref[0] = xn[0:vl, :]

    @pl.when(c == n_c - 1)
    def _():
        nbuf_ref[0] = ubuf[HIST + vl - 3:HIST + vl, :]

    ubuf[HIST - 3:HIST, :] = ubuf[HIST + L - 3:HIST + L, :]


def ssd_layer(x, conv_buf, h0, w, ln_g, ln_b):
    in_w_pad, conv_w, conv_b, dt_bias, a_log, d_skip, norm_w, out_w = w
    bsz, t_len, _ = x.shape
    L = SSD_CHUNK
    vl = L if t_len % L == 0 else t_len
    n_c = max(t_len // L, 1)
    gw = SSD_HPG * SSD_HEADDIM
    n_proj = in_w_pad.shape[1]
    pad_lanes = lambda v: jnp.pad(v.reshape(1, SSD_HEADS), ((0, 0), (0, LANES - SSD_HEADS)))
    dexp = jnp.repeat(d_skip, SSD_HEADDIM).reshape(1, SSD_INNER)
    h0g = h0.reshape(bsz, SSD_GROUPS, gw, SSD_STATE)
    const = lambda shape: pl.BlockSpec(shape, lambda b, c: (0,) * len(shape))
    resident = lambda shape: pl.BlockSpec(shape, lambda b, c: (0,) * len(shape),
                                          pipeline_mode=pl.Buffered(1))
    xn, nbuf, hout = pl.pallas_call(
        functools.partial(_ssd_kernel, vl=vl),
        grid=(bsz, n_c),
        in_specs=[pl.BlockSpec((1, vl, D_MODEL), lambda b, c: (b, c, 0)),
                  pl.BlockSpec((1, CONV_W - 1, SSD_CONV_DIM), lambda b, c: (b, 0, 0)),
                  pl.BlockSpec((1, SSD_GROUPS, gw, SSD_STATE), lambda b, c: (b, 0, 0, 0)),
                  resident((D_MODEL, n_proj)),
                  const((CONV_W, SSD_CONV_DIM)), const((1, SSD_CONV_DIM)),
                  const((1, LANES)), const((1, LANES)),
                  const((1, SSD_INNER)), const((1, SSD_INNER)),
                  resident((SSD_INNER, D_MODEL)),
                  const((1, D_MODEL)), const((1, D_MODEL))],
        out_specs=[pl.BlockSpec((1, vl, D_MODEL), lambda b, c: (b, c, 0)),
                   pl.BlockSpec((1, CONV_W - 1, SSD_CONV_DIM), lambda b, c: (b, 0, 0)),
                   pl.BlockSpec((1, SSD_GROUPS, gw, SSD_STATE), lambda b, c: (b, 0, 0, 0))],
        out_shape=[jax.ShapeDtypeStruct((bsz, t_len, D_MODEL), F32),
                   jax.ShapeDtypeStruct((bsz, CONV_W - 1, SSD_CONV_DIM), F32),
                   jax.ShapeDtypeStruct((bsz, SSD_GROUPS, gw, SSD_STATE), F32)],
        scratch_shapes=[pltpu.VMEM((L, D_MODEL), F32),
                        pltpu.VMEM((HIST + L, SSD_CONV_DIM), F32),
                        pltpu.VMEM((L, SSD_INNER), F32),
                        pltpu.VMEM((L, SSD_INNER), F32)],
        compiler_params=_cparams(2),
        name="ssd_layer",
    )(x, conv_buf, h0g, in_w_pad, conv_w, conv_b.reshape(1, SSD_CONV_DIM),
      pad_lanes(dt_bias), pad_lanes(a_log), dexp, norm_w.reshape(1, SSD_INNER), out_w,
      ln_g.reshape(1, D_MODEL), ln_b.reshape(1, D_MODEL))
    return xn, nbuf, hout.reshape(bsz, SSD_HEADS, SSD_HEADDIM, SSD_STATE)


def _row_tile(rows):
    return 256 if rows % 256 == 0 else rows


def _attn_proj_t_kernel(x_ref, w_ref, wkt_ref, q_ref, kt_ref, v_ref, g_ref):
    x = x_ref[0].astype(BF16)
    q_ref[0] = jnp.dot(x, w_ref[:, 0:ATT_W], preferred_element_type=F32)
    kt_ref[0] = _nt(wkt_ref[...], x)
    v_ref[0] = jnp.dot(x, w_ref[:, 2 * ATT_W:3 * ATT_W], preferred_element_type=F32)
    g_ref[0] = jnp.dot(x, w_ref[:, 3 * ATT_W:4 * ATT_W], preferred_element_type=F32)


def attn_project_t(x, in_w, wkt, tm=256):
    bsz, t_len, _ = x.shape
    rows = lambda: pl.BlockSpec((1, tm, ATT_W), lambda b, i: (b, i, 0))
    return pl.pallas_call(
        _attn_proj_t_kernel,
        grid=(bsz, t_len // tm),
        in_specs=[pl.BlockSpec((1, tm, D_MODEL), lambda b, i: (b, i, 0)),
                  pl.BlockSpec((D_MODEL, 4 * ATT_W), lambda b, i: (0, 0)),
                  pl.BlockSpec((ATT_W, D_MODEL), lambda b, i: (0, 0))],
        out_specs=[rows(), pl.BlockSpec((1, ATT_W, tm), lambda b, i: (b, 0, i)), rows(), rows()],
        out_shape=[jax.ShapeDtypeStruct((bsz, t_len, ATT_W), F32),
                   jax.ShapeDtypeStruct((bsz, ATT_W, t_len), F32),
                   jax.ShapeDtypeStruct((bsz, t_len, ATT_W), F32),
                   jax.ShapeDtypeStruct((bsz, t_len, ATT_W), F32)],
        compiler_params=_cparams(2),
        name="attn_project_t",
    )(x, in_w, wkt)


def _attn_project(x, in_w):
    bsz, t_len, _ = x.shape
    x2 = x.reshape(bsz * t_len, D_MODEL)
    tm = _row_tile(bsz * t_len)
    splits = tuple((i * ATT_W, ATT_W) for i in range(4))
    q, k, v, gate = proj_split(x2, in_w, splits, tm)
    shp = (bsz, t_len, ATT_W)
    return x2, tm, q.reshape(shp), k.reshape(shp), v.reshape(shp), gate.reshape(shp)


def kernel(x_prompt, x_sample, cache_k, cache_v, page_table, state_lru_conv, state_lru_h, state_ssd_conv, state_ssd_h, ln_g, ln_b, a_in_w, a_conv_w, a_conv_b, a_gate_r_w, a_gate_r_b, a_gate_i_w, a_gate_i_b, a_lambda, a_out_w, b_in_w, b_lambda, b_subln_w, b_out_w, c_in_w, c_conv_w, c_conv_b, c_dt_bias, c_a_log, c_d, c_norm_w, c_out_w):
    xp, xs = x_prompt, x_sample
    bp, bs = xp.shape[0], xs.shape[0]
    tp, ts = xp.shape[1], xs.shape[1]
    k_p, v_p, k_s, v_s = [], [], [], []
    lc_p, lh_p, lc_s, lh_s = [], [], [], []
    sc_p, sh_p, sc_s, sh_s = [], [], [], []
    for i in range(DEPTH):
        j = i // N_MIXERS
        kind = i % N_MIXERS
        if kind == 0:
            w = (a_in_w[j].astype(BF16), a_conv_w[j], a_conv_b[j], _block_diag_tiles(a_gate_r_w[j]),
                 a_gate_r_b[j], _block_diag_tiles(a_gate_i_w[j]), a_gate_i_b[j], a_lambda[j],
                 a_out_w[j].astype(BF16))
            zc = jnp.zeros((bp, CONV_W - 1, LRU_W), F32)
            zh = jnp.zeros((bp, LRU_W), F32)
            xp, c1, h1 = rglru_layer(xp, zc, zh, w, ln_g[i], ln_b[i])
            xs, c2, h2 = rglru_layer(xs, state_lru_conv[j], state_lru_h[j], w, ln_g[i], ln_b[i])
            lc_p.append(c1); lh_p.append(h1); lc_s.append(c2); lh_s.append(h2)
        elif kind == 1:
            lam_init = 0.8 - 0.6 * math.exp(-0.3 * i)
            in_w = b_in_w[j].astype(BF16)
            out_w = b_out_w[j].astype(BF16)
            wkt = b_in_w[j][:, ATT_W:2 * ATT_W].T.astype(BF16)
            q, kt, v, g = attn_project_t(xp, in_w, wkt)
            og = attn_prompt(q, kt, v, g, b_lambda[j], b_subln_w[j], lam_init)
            xp = outproj_ln(og.reshape(bp * tp, ATT_W), out_w, xp.reshape(bp * tp, D_MODEL),
                            ln_g[i], ln_b[i], _row_tile(bp * tp)).reshape(bp, tp, D_MODEL)
            k_p.append(jnp.transpose(kt.reshape(bp, ATT_HEADS, 2, ATT_DH, tp), (0, 4, 1, 2, 3)))
            v_p.append(v.reshape(bp, tp, ATT_HEADS, 2 * ATT_DH))
            x2, tm, q, k, v, g = _attn_project(xs, in_w)
            og = attn_sample(q, k, v, g, cache_k, cache_v, j, page_table, b_lambda[j],
                             b_subln_w[j], lam_init)
            xs = outproj_ln(og.reshape(bs * ts, ATT_W), out_w, x2, ln_g[i], ln_b[i], tm
                            ).reshape(bs, ts, D_MODEL)
            k_s.append(k.reshape(bs, ts, ATT_HEADS, 2, ATT_DH))
            v_s.append(v.reshape(bs, ts, ATT_HEADS, 2 * ATT_DH))
        else:
            in_w_pad = jnp.pad(c_in_w[j], ((0, 0), (0, LANES - SSD_HEADS))).astype(BF16)
            w = (in_w_pad, c_conv_w[j], c_conv_b[j], c_dt_bias[j], c_a_log[j], c_d[j], c_norm_w[j],
                 c_out_w[j].astype(BF16))
            zc = jnp.zeros((bp, CONV_W - 1, SSD_CONV_DIM), F32)
            zh = jnp.zeros((bp, SSD_HEADS, SSD_HEADDIM, SSD_STATE), F32)
            xp, c1, h1 = ssd_layer(xp, zc, zh, w, ln_g[i], ln_b[i])
            xs, c2, h2 = ssd_layer(xs, state_ssd_conv[j], state_ssd_h[j], w, ln_g[i], ln_b[i])
            sc_p.append(c1); sh_p.append(h1); sc_s.append(c2); sh_s.append(h2)
    return (xp, xs, jnp.stack(k_p), jnp.stack(v_p), jnp.stack(k_s), jnp.stack(v_s),
            jnp.stack(lc_p), jnp.stack(lh_p), jnp.stack(lc_s), jnp.stack(lh_s),
            jnp.stack(sc_p), jnp.stack(sh_p), jnp.stack(sc_s), jnp.stack(sh_s))
```

```python
import functools
import math

import jax
import jax.numpy as jnp
from jax import lax
from jax.experimental import pallas as pl
from jax.experimental.pallas import tpu as pltpu

F32 = jnp.float32
BF16 = jnp.bfloat16

D_MODEL = 1024
DEPTH = 4
PAGE_SIZE = 128
N_MIXERS = 3
DN_ALPHA = (2.0 * DEPTH) ** 0.25
EPS = 1e-5
CONV_W = 4
LRU_W = D_MODEL
LRU_BLOCKS = 16
LRU_BS = LRU_W // LRU_BLOCKS
LRU_C = 8.0
ATT_HEADS = 8
ATT_DH = D_MODEL // (2 * ATT_HEADS)
ATT_W = ATT_HEADS * 2 * ATT_DH
SSD_INNER = 2 * D_MODEL
SSD_HEADDIM = 64
SSD_HEADS = SSD_INNER // SSD_HEADDIM
SSD_GROUPS = 8
SSD_HPG = SSD_HEADS // SSD_GROUPS
SSD_STATE = 128
SSD_CONV_DIM = SSD_INNER + 2 * SSD_GROUPS * SSD_STATE
SSD_CHUNK = 128

LANES = 128
SUBLANES = 8
MXU_DIM = 256
VMEM_LIMIT = 56 * 1024 * 1024
LOG2E = math.log2(math.e)
F32_TINY = 1.1754944e-38
NEG_BIG = -1e30
HIST = SUBLANES


def _cparams(n_grid):
    return pltpu.CompilerParams(dimension_semantics=("arbitrary",) * n_grid,
                                vmem_limit_bytes=VMEM_LIMIT)


def _nt(a, b):
    return lax.dot_general(a, b, (((1,), (1,)), ((), ())), preferred_element_type=F32)


def _sigmoid(x):
    return 0.5 * jnp.tanh(0.5 * x) + 0.5


def _sqrt_nonneg(x):
    return x * lax.rsqrt(jnp.maximum(x, F32_TINY))


def _layer_norm(v, g, b):
    mu = jnp.mean(v, axis=-1, keepdims=True)
    d = v - mu
    var = jnp.mean(d * d, axis=-1, keepdims=True)
    return d * lax.rsqrt(var + EPS) * g + b


def _silu(x):
    return x * _sigmoid(x)


def _softplus(x):
    return jnp.maximum(x, 0.0) + jnp.log(1.0 + jnp.exp(-jnp.abs(x)))


def _proj_kernel(x_ref, w_ref, *out_refs, splits):
    x = x_ref[...].astype(BF16)
    for (off, width), o_ref in zip(splits, out_refs):
        o_ref[...] = jnp.dot(x, w_ref[:, off:off + width], preferred_element_type=F32)


def proj_split(x2d, w_bf16, splits, tm):
    rows, kdim = x2d.shape
    ndim = w_bf16.shape[1]
    return pl.pallas_call(
        functools.partial(_proj_kernel, splits=splits),
        grid=(rows // tm,),
        in_specs=[pl.BlockSpec((tm, kdim), lambda i: (i, 0)),
                  pl.BlockSpec((kdim, ndim), lambda i: (0, 0))],
        out_specs=[pl.BlockSpec((tm, wd), lambda i: (i, 0)) for _, wd in splits],
        out_shape=[jax.ShapeDtypeStruct((rows, wd), F32) for _, wd in splits],
        compiler_params=_cparams(1),
        name="proj_split",
    )(x2d, w_bf16)


def _outproj_ln_kernel(y_ref, w_ref, x_ref, g_ref, b_ref, o_ref):
    f = jnp.dot(y_ref[...].astype(BF16), w_ref[...], preferred_element_type=F32)
    o_ref[...] = _layer_norm(DN_ALPHA * x_ref[...] + f, g_ref[...], b_ref[...])


def outproj_ln(y2d, w_bf16, x2d, g, b, tm):
    rows, kdim = y2d.shape
    return pl.pallas_call(
        _outproj_ln_kernel,
        grid=(rows // tm,),
        in_specs=[pl.BlockSpec((tm, kdim), lambda i: (i, 0)),
                  pl.BlockSpec((kdim, D_MODEL), lambda i: (0, 0)),
                  pl.BlockSpec((tm, D_MODEL), lambda i: (i, 0)),
                  pl.BlockSpec((1, D_MODEL), lambda i: (0, 0)),
                  pl.BlockSpec((1, D_MODEL), lambda i: (0, 0))],
        out_specs=pl.BlockSpec((tm, D_MODEL), lambda i: (i, 0)),
        out_shape=jax.ShapeDtypeStruct((rows, D_MODEL), F32),
        compiler_params=_cparams(1),
        name="outproj_ln",
    )(y2d, w_bf16, x2d, g.reshape(1, D_MODEL), b.reshape(1, D_MODEL))


def _rglru_core(ub, gate, a_s, b_s, h_in, gate_refs, *, tt, unroll):
    cw_ref, cb_ref, wr_ref, rb_ref, wi_ref, ib_ref, lam_ref = gate_refs
    cw = cw_ref[...]
    ext = ub[...]
    y = cb_ref[...] + cw[3:4, :] * ext[HIST:, :]
    for k in range(1, CONV_W):
        y = y + cw[3 - k:4 - k, :] * pltpu.roll(ext, k, 0)[HIST:, :]

    sp = _softplus(-lam_ref[...])
    yb = y.astype(BF16)
    for j in range(LRU_W // MXU_DIM):
        sl = slice(j * MXU_DIM, (j + 1) * MXU_DIM)
        ys = yb[:, sl]
        r = _sigmoid(jnp.dot(ys, wr_ref[j], preferred_element_type=F32) + rb_ref[:, sl])
        ig = _sigmoid(jnp.dot(ys, wi_ref[j], preferred_element_type=F32) + ib_ref[:, sl])
        log_a = -LRU_C * r * sp[:, sl]
        a = jnp.exp(log_a)
        a_s[:, sl] = a
        b_s[:, sl] = _sqrt_nonneg(-jnp.tanh(log_a) * (a * a + 1.0)) * (ig * y[:, sl])

    row = lax.broadcasted_iota(jnp.int32, (SUBLANES, LRU_W), 0)

    def group(r0, hc):
        a = a_s[pl.ds(r0, SUBLANES), :]
        b = b_s[pl.ds(r0, SUBLANES), :]
        for d in (1, 2, 4):
            a_sh = pltpu.roll(a, d, 0)
            b_sh = pltpu.roll(b, d, 0)
            m = row >= d
            b = jnp.where(m, a * b_sh + b, b)
            a = jnp.where(m, a * a_sh, a)
        h = a * hc + b
        b_s[pl.ds(r0, SUBLANES), :] = h
        return h[SUBLANES - 1:SUBLANES, :]

    if unroll:
        hc = h_in
        for g in range(tt // SUBLANES):
            hc = group(g * SUBLANES, hc)
    else:
        lax.fori_loop(0, tt // SUBLANES,
                      lambda g, hc: group(pl.multiple_of(g * SUBLANES, SUBLANES), hc), h_in)
    return b_s[...] * _silu(gate)


def _rglru_core_kernel(gate_ref, u_ref, buf_ref, h0_ref, *refs, tt, valid):
    gate_refs = refs[:7]
    out_ref, nbuf_ref, hlast_ref, ubuf, g_s, a_s, b_s = refs[7:]
    ubuf[0:HIST - 3, :] = jnp.zeros((HIST - 3, LRU_W), F32)
    ubuf[HIST - 3:HIST, :] = buf_ref[0]
    ubuf[HIST:HIST + valid, :] = u_ref[0]
    ubuf[HIST + valid:HIST + tt, :] = jnp.zeros((tt - valid, LRU_W), F32)
    g_s[0:valid, :] = gate_ref[0]
    g_s[valid:tt, :] = jnp.zeros((tt - valid, LRU_W), F32)
    hg = _rglru_core(ubuf, g_s[...], a_s, b_s, h0_ref[0], gate_refs, tt=tt, unroll=False)
    out_ref[0] = hg[0:valid, :]
    nbuf_ref[0] = ubuf[HIST + valid - 3:HIST + valid, :]
    hlast_ref[0] = b_s[valid - 1:valid, :]


def _rglru_layer_kernel(xa_ref, xb_ref, xn_ref, buf_ref, h0_ref, win_ref, *refs, tt, n_steps):
    gate_refs = refs[:7]
    wout_ref, lng_ref, lnb_ref, out_ref, nbuf_ref, hlast_ref, ubuf, g_s, a_s, b_s, hcar = refs[7:]
    s = pl.program_id(0)
    first = s % n_steps == 0

    def project(x_ref, slot):
        xb = x_ref[0].astype(BF16)
        g_s[slot] = jnp.dot(xb, win_ref[:, 0:LRU_W], preferred_element_type=F32)
        ubuf[slot, HIST:HIST + tt, :] = jnp.dot(xb, win_ref[:, LRU_W:2 * LRU_W],
                                                preferred_element_type=F32)

    def finish(x_ref, hg):
        f = jnp.dot(hg.astype(BF16), wout_ref[...], preferred_element_type=F32)
        return _layer_norm(DN_ALPHA * x_ref[0] + f, lng_ref[...], lnb_ref[...])

    @pl.when(s == 0)
    def _():
        for slot in range(2):
            ubuf[slot, 0:HIST - 3, :] = jnp.zeros((HIST - 3, LRU_W), F32)
        project(xa_ref, 0)

    @pl.when(first)
    def _():
        ubuf[0, HIST - 3:HIST, :] = buf_ref[0]
        hcar[...] = h0_ref[0]

    @pl.when(jnp.logical_not(first))
    def _():
        ubuf[0, HIST - 3:HIST, :] = ubuf[1, HIST + tt - 3:HIST + tt, :]

    project(xb_ref, 1)
    hg = _rglru_core(ubuf.at[0], g_s[0], a_s, b_s, hcar[...], gate_refs, tt=tt, unroll=True)
    h_a = b_s[tt - 1:tt, :]
    out_ref[0, 0:tt, :] = finish(xa_ref, hg)
    ubuf[1, HIST - 3:HIST, :] = ubuf[0, HIST + tt - 3:HIST + tt, :]
    project(xn_ref, 0)
    hg = _rglru_core(ubuf.at[1], g_s[1], a_s, b_s, h_a, gate_refs, tt=tt, unroll=True)
    h_b = b_s[tt - 1:tt, :]
    hcar[...] = h_b
    out_ref[0, tt:2 * tt, :] = finish(xb_ref, hg)

    @pl.when(s % n_steps == n_steps - 1)
    def _():
        nbuf_ref[0] = ubuf[1, HIST + tt - 3:HIST + tt, :]
        hlast_ref[0] = h_b


def rglru_layer(x, conv_buf, h0, w, ln_g, ln_b):
    in_w, conv_w, conv_b, wr_bd, r_b, wi_bd, i_b, lam, out_w = w
    bsz, t_len, _ = x.shape
    row = lambda v: v.reshape(1, LRU_W)
    gate_args = (conv_w, row(conv_b), wr_bd, row(r_b), wi_bd, row(i_b), row(lam))
    h0r = h0.reshape(bsz, 1, LRU_W)
    state_shapes = [jax.ShapeDtypeStruct((bsz, CONV_W - 1, LRU_W), F32),
                    jax.ShapeDtypeStruct((bsz, 1, LRU_W), F32)]
    tt = 256
    if t_len % (2 * tt) == 0:
        n_steps = t_len // (2 * tt)
        n_tiles = bsz * t_len // tt
        const = lambda shape: pl.BlockSpec(shape, lambda s: (0,) * len(shape))
        x_tile = lambda off: pl.BlockSpec(
            (1, tt, D_MODEL), lambda s: (jnp.minimum(2 * s + off, n_tiles - 1), 0, 0))
        per_seq = lambda r: pl.BlockSpec((1, r, LRU_W), lambda s: (s // n_steps, 0, 0))
        gate_specs = [const((CONV_W, LRU_W)), const((1, LRU_W)),
                      const((LRU_W // MXU_DIM, MXU_DIM, MXU_DIM)), const((1, LRU_W)),
                      const((LRU_W // MXU_DIM, MXU_DIM, MXU_DIM)), const((1, LRU_W)),
                      const((1, LRU_W))]
        xt = x.reshape(n_tiles, tt, D_MODEL)
        out, nbuf, hlast = pl.pallas_call(
            functools.partial(_rglru_layer_kernel, tt=tt, n_steps=n_steps),
            grid=(bsz * n_steps,),
            in_specs=([x_tile(0), x_tile(1), x_tile(2), per_seq(CONV_W - 1), per_seq(1),
                       const((D_MODEL, 2 * LRU_W))] + gate_specs
                      + [const((LRU_W, D_MODEL)), const((1, D_MODEL)), const((1, D_MODEL))]),
            out_specs=[pl.BlockSpec((1, 2 * tt, D_MODEL), lambda s: (s, 0, 0)),
                       per_seq(CONV_W - 1), per_seq(1)],
            out_shape=[jax.ShapeDtypeStruct((bsz * n_steps, 2 * tt, D_MODEL), F32)] + state_shapes,
            scratch_shapes=[pltpu.VMEM((2, HIST + tt, LRU_W), F32),
                            pltpu.VMEM((2, tt, LRU_W), F32),
                            pltpu.VMEM((tt, LRU_W), F32),
                            pltpu.VMEM((tt, LRU_W), F32),
                            pltpu.VMEM((1, LRU_W), F32)],
            compiler_params=_cparams(1),
            name="rglru_layer",
        )(xt, xt, xt, conv_buf, h0r, in_w, *gate_args, out_w, row(ln_g), row(ln_b))
        return out.reshape(bsz, t_len, D_MODEL), nbuf, hlast.reshape(bsz, LRU_W)

    tt, valid = 16, t_len
    assert t_len <= tt
    x2 = x.reshape(bsz * t_len, D_MODEL)
    gate, u = proj_split(x2, in_w, ((0, LRU_W), (LRU_W, LRU_W)), _row_tile(bsz * t_len))
    vec = lambda: pl.BlockSpec((1, LRU_W), lambda b: (0, 0))
    wspec = lambda: pl.BlockSpec((LRU_W // MXU_DIM, MXU_DIM, MXU_DIM), lambda b: (0, 0, 0))
    per_seq = lambda r: pl.BlockSpec((1, r, LRU_W), lambda b: (b, 0, 0))
    hg, nbuf, hlast = pl.pallas_call(
        functools.partial(_rglru_core_kernel, tt=tt, valid=valid),
        grid=(bsz,),
        in_specs=[per_seq(valid), per_seq(valid), per_seq(CONV_W - 1), per_seq(1),
                  pl.BlockSpec((CONV_W, LRU_W), lambda b: (0, 0)),
                  vec(), wspec(), vec(), wspec(), vec(), vec()],
        out_specs=[per_seq(valid), per_seq(CONV_W - 1), per_seq(1)],
        out_shape=[jax.ShapeDtypeStruct((bsz, t_len, LRU_W), F32)] + state_shapes,
        scratch_shapes=[pltpu.VMEM((HIST + tt, LRU_W), F32),
                        pltpu.VMEM((tt, LRU_W), F32),
                        pltpu.VMEM((tt, LRU_W), F32),
                        pltpu.VMEM((tt, LRU_W), F32)],
        compiler_params=_cparams(1),
        name="rglru_core",
    )(gate.reshape(bsz, t_len, LRU_W), u.reshape(bsz, t_len, LRU_W), conv_buf, h0r, *gate_args)
    out = outproj_ln(hg.reshape(bsz * t_len, LRU_W), out_w, x2, ln_g, ln_b,
                     _row_tile(bsz * t_len)).reshape(bsz, t_len, D_MODEL)
    return out, nbuf, hlast.reshape(bsz, LRU_W)


def _block_diag_tiles(w):
    per = MXU_DIM // LRU_BS
    w4 = w.reshape(LRU_W // MXU_DIM, per, LRU_BS, LRU_BS)
    eye = jnp.eye(per, dtype=w.dtype)
    t = jnp.einsum("jakc,ab->jakbc", w4, eye)
    return t.reshape(LRU_W // MXU_DIM, MXU_DIM, MXU_DIM).astype(BF16)


def _diff_lambda(lp, lam_init):
    s1 = jnp.sum(lp[0:1, :] * lp[1:2, :], axis=-1, keepdims=True)
    s2 = jnp.sum(lp[2:3, :] * lp[3:4, :], axis=-1, keepdims=True)
    return jnp.exp(s1) - jnp.exp(s2) + lam_init


def _subln_gate(o, sw, gate, lam_init):
    ms = jnp.mean(o * o, axis=-1, keepdims=True)
    return (o * lax.rsqrt(ms + EPS) * sw * (1.0 - lam_init)) * _silu(gate)


def _attn_prompt_kernel(lp_ref, q_ref, kt_ref, v_ref, gate_ref, sw_ref, o_ref,
                        kb_s, vx_s, m_s, accl_s, s_s, *, tq, lam_init):
    qi = pl.program_id(2)
    hd = 2 * ATT_DH
    t_len = v_ref.shape[1]

    @pl.when(qi == 0)
    def _():
        for i in range(t_len // tq):
            kb_s[i] = kt_ref[0, :, i * tq:(i + 1) * tq].astype(BF16)

        def cvt(i, carry):
            r = pl.multiple_of(i * tq, tq)
            vx_s[pl.ds(r, tq), 0:hd] = v_ref[0, pl.ds(r, tq), :].astype(BF16)
            vx_s[pl.ds(r, tq), hd:2 * hd] = jnp.ones((tq, hd), BF16)
            return carry
        lax.fori_loop(0, t_len // tq, cvt, 0)

    lam = _diff_lambda(lp_ref[...], lam_init)
    q = q_ref[0] * (ATT_DH ** -0.5 * LOG2E)
    lane = lax.broadcasted_iota(jnp.int32, (tq, hd), 1)
    q_maps = (jnp.where(lane < ATT_DH, q, 0.0).astype(BF16),
              jnp.where(lane >= ATT_DH, q, 0.0).astype(BF16))
    m_s[...] = jnp.full(m_s.shape, NEG_BIG, F32)
    accl_s[...] = jnp.zeros(accl_s.shape, F32)
    rowi = lax.broadcasted_iota(jnp.int32, (tq, tq), 0)
    coli = lax.broadcasted_iota(jnp.int32, (tq, tq), 1)

    def scores(ki, slot):
        kb = kb_s[ki]
        for c in range(2):
            s_s[slot, c] = jnp.dot(q_maps[c], kb, preferred_element_type=F32)

    def absorb(ki, slot, masked):
        ks = pl.multiple_of(ki * tq, tq)
        vx = vx_s[pl.ds(ks, tq), :]
        for c in range(2):
            s = s_s[slot, c]
            if masked:
                s = jnp.where(coli <= rowi, s, NEG_BIG)
            m_prev = m_s[c]
            m_new = jnp.maximum(m_prev, jnp.max(s, axis=-1, keepdims=True))
            alpha = jnp.exp2(m_prev - m_new)
            p = jnp.exp2(s - jnp.concatenate([m_new] * (tq // hd), axis=1))
            pv = jnp.dot(p.astype(BF16), vx, preferred_element_type=F32)
            accl_s[c] = jnp.concatenate([alpha, alpha], axis=1) * accl_s[c] + pv
            m_s[c] = m_new

    scores(0, 0)

    def pair(j, carry):
        k0 = 2 * j
        scores(k0 + 1, 1)
        absorb(k0, 0, False)
        scores(k0 + 2, 0)
        absorb(k0 + 1, 1, False)
        return carry

    lax.fori_loop(0, qi // 2, pair, 0)

    @pl.when(qi % 2 == 0)
    def _():
        absorb(qi, 0, True)

    @pl.when(qi % 2 == 1)
    def _():
        scores(qi, 1)
        absorb(qi - 1, 0, False)
        absorb(qi, 1, True)

    a1 = accl_s[0]
    a2 = accl_s[1]
    o = a1[:, 0:hd] / a1[:, hd:2 * hd] - lam * (a2[:, 0:hd] / a2[:, hd:2 * hd])
    o_ref[0] = _subln_gate(o, sw_ref[...], gate_ref[0], lam_init)


def attn_prompt(q, kt, v, gate, lam_p, subln_w, lam_init, tq=512):
    bsz, t_len, _ = q.shape
    hd = 2 * ATT_DH
    qspec = lambda: pl.BlockSpec((1, tq, hd), lambda b, h, i: (b, i, h))
    return pl.pallas_call(
        functools.partial(_attn_prompt_kernel, tq=tq, lam_init=lam_init),
        grid=(bsz, ATT_HEADS, t_len // tq),
        in_specs=[pl.BlockSpec((4, ATT_DH), lambda b, h, i: (0, 0)),
                  qspec(),
                  pl.BlockSpec((1, hd, t_len), lambda b, h, i: (b, h, 0)),
                  pl.BlockSpec((1, t_len, hd), lambda b, h, i: (b, 0, h)),
                  qspec(),
                  pl.BlockSpec((1, hd), lambda b, h, i: (0, 0))],
        out_specs=qspec(),
        out_shape=jax.ShapeDtypeStruct((bsz, t_len, ATT_W), F32),
        scratch_shapes=[pltpu.VMEM((t_len // tq, hd, tq), BF16),
                        pltpu.VMEM((t_len, 2 * hd), BF16),
                        pltpu.VMEM((2, tq, hd), F32), pltpu.VMEM((2, tq, 2 * hd), F32),
                        pltpu.VMEM((2, 2, tq, tq), F32)],
        compiler_params=_cparams(3),
        name="attn_prompt",
    )(lam_p, q, kt, v, gate, subln_w.reshape(1, hd))


def _attn_sample_kernel(pt_ref, lp_ref, qbd_ref, *refs, n_pg, t_new, lam_init):
    k_refs = refs[:n_pg]
    v_refs = refs[n_pg:2 * n_pg]
    knew_ref, vnew_ref, gate_ref, sw_ref, o_ref, m_s, l_s, acc_s = refs[2 * n_pg:]
    j = pl.program_id(1)
    n_j = pl.num_programs(1)
    n_rows = ATT_HEADS * 2 * t_new
    hd = 2 * ATT_DH

    @pl.when(j == 0)
    def _():
        m_s[...] = jnp.full(m_s.shape, NEG_BIG, F32)
        l_s[...] = jnp.zeros(l_s.shape, F32)
        acc_s[...] = jnp.zeros(acc_s.shape, F32)

    qbd = qbd_ref[0]

    def attend(kts, vbs, mask):
        n = len(kts)
        s = jnp.concatenate([jnp.dot(qbd, kt.astype(BF16), preferred_element_type=F32)
                             for kt in kts], axis=1)
        if mask is not None:
            s = jnp.where(mask, s, NEG_BIG)
        m_prev = m_s[...]
        m_new = jnp.maximum(m_prev, jnp.max(s, axis=-1, keepdims=True))
        alpha = jnp.exp2(m_prev - m_new)
        p = jnp.exp2(s - jnp.concatenate([m_new] * n, axis=1))
        l_s[...] = alpha * l_s[...] + jnp.sum(p, axis=-1, keepdims=True)
        pb = p.astype(BF16)
        pv = jnp.dot(pb[:, 0:PAGE_SIZE], vbs[0].astype(BF16), preferred_element_type=F32)
        for i in range(1, n):
            pv = pv + jnp.dot(pb[:, i * PAGE_SIZE:(i + 1) * PAGE_SIZE], vbs[i].astype(BF16),
                              preferred_element_type=F32)
        acc_s[...] = jnp.concatenate([alpha] * (ATT_W // LANES), axis=1) * acc_s[...] + pv
        m_s[...] = m_new

    def page_v(vr):
        return jnp.concatenate([vr[0, pl.ds(h, PAGE_SIZE, stride=ATT_HEADS), :]
                                for h in range(ATT_HEADS)], axis=1)

    attend([r[0] for r in k_refs], [page_v(r) for r in v_refs], None)

    @pl.when(j == n_j - 1)
    def _():
        rowi = lax.broadcasted_iota(jnp.int32, (n_rows, PAGE_SIZE), 0)
        coli = lax.broadcasted_iota(jnp.int32, (n_rows, PAGE_SIZE), 1)
        attend([knew_ref[0]], [vnew_ref[0]], coli <= (rowi % t_new))
        lam = _diff_lambda(lp_ref[...], lam_init)
        sw = sw_ref[...]
        for h in range(ATT_HEADS):
            r1 = h * 2 * t_new
            r2 = r1 + t_new
            cs = slice(h * hd, (h + 1) * hd)
            o1 = acc_s[r1:r1 + t_new, cs] / l_s[r1:r1 + t_new, :]
            o2 = acc_s[r2:r2 + t_new, cs] / l_s[r2:r2 + t_new, :]
            o_ref[0, :, cs] = _subln_gate(o1 - lam * o2, sw, gate_ref[0, :, cs], lam_init)


def attn_sample(q, k, v, gate, cache_k, cache_v, layer, page_table, lam_p, subln_w, lam_init,
                n_pg=8):
    bsz, t_new, _ = q.shape
    n_pages = page_table.shape[1]
    n_pool = cache_k.shape[1]
    hd = 2 * ATT_DH
    n_rows = ATT_HEADS * 2 * t_new
    q4 = (q * (ATT_DH ** -0.5 * LOG2E)).reshape(bsz, t_new, 2 * ATT_HEADS, ATT_DH)
    eye = jnp.eye(2 * ATT_HEADS, dtype=F32)
    qbd = jnp.einsum("bqhd,hg->bhqgd", q4, eye).reshape(bsz, n_rows, ATT_W).astype(BF16)
    k_new = jnp.pad(jnp.swapaxes(k, 1, 2), ((0, 0), (0, 0), (0, PAGE_SIZE - t_new)))
    v_new = jnp.pad(v, ((0, 0), (0, PAGE_SIZE - t_new), (0, 0)))
    ck = jnp.transpose(cache_k, (0, 1, 3, 4, 5, 2)).reshape(-1, ATT_W, PAGE_SIZE)
    cv = cache_v.reshape(-1, PAGE_SIZE * ATT_HEADS, hd)
    pt = page_table.reshape(-1) + layer * n_pool

    def page_spec(i, rows, width):
        return pl.BlockSpec((1, rows, width),
                            lambda b, j, pt_ref: (pt_ref[b * n_pages + j * n_pg + i], 0, 0))

    per_b = lambda r: pl.BlockSpec((1, r, ATT_W), lambda b, j, pt_ref: (b, 0, 0))
    grid_spec = pltpu.PrefetchScalarGridSpec(
        num_scalar_prefetch=1,
        grid=(bsz, n_pages // n_pg),
        in_specs=([pl.BlockSpec((4, ATT_DH), lambda b, j, pt_ref: (0, 0)), per_b(n_rows)]
                  + [page_spec(i, ATT_W, PAGE_SIZE) for i in range(n_pg)]
                  + [page_spec(i, PAGE_SIZE * ATT_HEADS, hd) for i in range(n_pg)]
                  + [pl.BlockSpec((1, ATT_W, PAGE_SIZE), lambda b, j, pt_ref: (b, 0, 0)),
                     per_b(PAGE_SIZE), per_b(t_new),
                     pl.BlockSpec((1, hd), lambda b, j, pt_ref: (0, 0))]),
        out_specs=per_b(t_new),
        scratch_shapes=[pltpu.VMEM((n_rows, LANES), F32), pltpu.VMEM((n_rows, LANES), F32),
                        pltpu.VMEM((n_rows, ATT_W), F32)],
    )
    return pl.pallas_call(
        functools.partial(_attn_sample_kernel, n_pg=n_pg, t_new=t_new, lam_init=lam_init),
        grid_spec=grid_spec,
        out_shape=jax.ShapeDtypeStruct((bsz, t_new, ATT_W), F32),
        compiler_params=_cparams(2),
        name="attn_sample",
    )(pt, lam_p, qbd, *([ck] * n_pg), *([cv] * n_pg), k_new, v_new, gate,
      subln_w.reshape(1, hd))


def _ssd_core(ub, z, dt_raw, hout_ref, y_s, scan_refs, *, vl):
    cw_ref, cb_ref, dtb_ref, alog_ref, dexp_ref, nw_ref = scan_refs
    L = SSD_CHUNK
    gw = SSD_HPG * SSD_HEADDIM
    rowL = lax.broadcasted_iota(jnp.int32, (L, LANES), 0)
    laneL = lax.broadcasted_iota(jnp.int32, (L, LANES), 1)
    dt = _softplus(dt_raw + dtb_ref[...])
    dt = jnp.where((rowL < vl) & (laneL < SSD_HEADS), dt, 0.0)
    adt = dt * (-jnp.exp(alog_ref[...]))
    tri = (lax.broadcasted_iota(jnp.int32, (L, L), 0)
           >= lax.broadcasted_iota(jnp.int32, (L, L), 1))
    cs = jnp.dot(tri.astype(F32), adt, preferred_element_type=F32,
                 precision=lax.Precision.HIGHEST)
    cs_last = cs[L - 1:L, :]
    w1 = dt * jnp.exp(cs_last - cs)
    cs_t = cs.T
    dt_t = dt.T
    w1_t = w1.T

    cw = cw_ref[...]
    lane_g = lax.broadcasted_iota(jnp.int32, (L, gw), 1)

    def conv_silu(lo, width):
        sl = slice(lo, lo + width)
        y = cb_ref[:, sl] + cw[3:4, sl] * ub[HIST:HIST + L, sl]
        for k in range(1, CONV_W):
            y = y + cw[3 - k:4 - k, sl] * ub[HIST - k:HIST - k + L, sl]
        return _silu(y)

    for g in range(SSD_GROUPS):
        xh = conv_silu(g * gw, gw)
        bm = conv_silu(SSD_INNER + g * SSD_STATE, SSD_STATE)
        cm = conv_silu(SSD_INNER + (SSD_GROUPS + g) * SSD_STATE, SSD_STATE)
        bmb = bm.astype(BF16)
        cmb = cm.astype(BF16)
        xhb = xh.astype(BF16)
        cbm = _nt(cmb, bmb)
        y_diag = jnp.zeros((L, gw), F32)
        e_cols = []
        w_rows = []
        d_rows = []
        for r in range(SSD_HPG):
            h = g * SSD_HPG + r
            cs_col = cs[:, h:h + 1]
            cs_row = cs_t[h:h + 1, :]
            lm = jnp.exp(jnp.where(tri, cs_col - cs_row, NEG_BIG))
            mat = (cbm * lm * dt_t[h:h + 1, :]).astype(BF16)
            in_head = (lane_g >= r * SSD_HEADDIM) & (lane_g < (r + 1) * SSD_HEADDIM)
            xr = jnp.where(in_head, xhb, jnp.zeros_like(xhb))
            y_diag = y_diag + jnp.dot(mat, xr, preferred_element_type=F32)
            e_cols.append(jnp.broadcast_to(jnp.exp(cs_col), (L, SSD_HEADDIM)))
            w_rows.append(jnp.broadcast_to(w1_t[h:h + 1, :], (SSD_HEADDIM, L)))
            d_rows.append(jnp.broadcast_to(jnp.exp(cs_t[h:h + 1, L - 1:L]),
                                           (SSD_HEADDIM, SSD_STATE)))
        hg = hout_ref[0, g]
        y_off = _nt(cmb, hg.astype(BF16)) * jnp.concatenate(e_cols, axis=1)
        y_s[:, g * gw:(g + 1) * gw] = y_diag + y_off + dexp_ref[:, g * gw:(g + 1) * gw] * xh
        xd_t = (xh.T * jnp.concatenate(w_rows, axis=0)).astype(BF16)
        states = jnp.dot(xd_t, bmb, preferred_element_type=F32)
        hout_ref[0, g] = hg * jnp.concatenate(d_rows, axis=0) + states

    gated = y_s[...] * _silu(z)
    ms = jnp.mean(gated * gated, axis=-1, keepdims=True)
    return gated * lax.rsqrt(ms + EPS) * nw_ref[...]


def _ssd_core_kernel(xbc_ref, z_ref, dtr_ref, buf_ref, h0_ref, *refs, vl):
    scan_refs = refs[:6]
    out_ref, nbuf_ref, hout_ref, ubuf, z_s, dt_s, y_s = refs[6:]
    L = SSD_CHUNK
    ubuf[HIST - 3:HIST, :] = buf_ref[0]
    ubuf[HIST:HIST + vl, :] = xbc_ref[0]
    ubuf[HIST + vl:HIST + L, :] = jnp.zeros((L - vl, SSD_CONV_DIM), F32)
    z_s[0:vl, :] = z_ref[0]
    z_s[vl:L, :] = jnp.zeros((L - vl, SSD_INNER), F32)
    dt_s[0:vl, :] = dtr_ref[0]
    dt_s[vl:L, :] = jnp.zeros((L - vl, LANES), F32)
    hout_ref[0] = h0_ref[0]
    yn = _ssd_core(ubuf, z_s[...], dt_s[...], hout_ref, y_s, scan_refs, vl=vl)
    out_ref[0] = yn[0:vl, :]
    nbuf_ref[0] = ubuf[HIST + vl - 3:HIST + vl, :]


def _ssd_layer_kernel(xa_ref, xb_ref, xn_ref, buf_ref, h0_ref, win_ref, *refs, n_steps):
    scan_refs = refs[:6]
    (wout_ref, lng_ref, lnb_ref, out_ref, nbuf_ref, hout_ref, ubuf, z_s, dt_s, y_s) = refs[6:]
    L = SSD_CHUNK
    s = pl.program_id(0)
    first = s % n_steps == 0
    xbc_lo = SSD_INNER
    dt_lo = SSD_INNER + SSD_CONV_DIM

    def project(x_ref, slot):
        xb = x_ref[0].astype(BF16)
        z_s[slot] = jnp.dot(xb, win_ref[:, 0:xbc_lo], preferred_element_type=F32)
        ubuf[slot, HIST:HIST + L, :] = jnp.dot(xb, win_ref[:, xbc_lo:dt_lo],
                                               preferred_element_type=F32)
        dt_s[slot] = jnp.dot(xb, win_ref[:, dt_lo:dt_lo + LANES], preferred_element_type=F32)

    def finish(x_ref, yn):
        f = jnp.dot(yn.astype(BF16), wout_ref[...], preferred_element_type=F32)
        return _layer_norm(DN_ALPHA * x_ref[0] + f, lng_ref[...], lnb_ref[...])

    @pl.when(s == 0)
    def _():
        project(xa_ref, 0)

    @pl.when(first)
    def _():
        ubuf[0, HIST - 3:HIST, :] = buf_ref[0]
        hout_ref[0] = h0_ref[0]

    @pl.when(jnp.logical_not(first))
    def _():
        ubuf[0, HIST - 3:HIST, :] = ubuf[1, HIST + L - 3:HIST + L, :]

    project(xb_ref, 1)
    yn = _ssd_core(ubuf.at[0], z_s[0], dt_s[0], hout_ref, y_s, scan_refs, vl=L)
    out_ref[0, 0:L, :] = finish(xa_ref, yn)
    ubuf[1, HIST - 3:HIST, :] = ubuf[0, HIST + L - 3:HIST + L, :]
    project(xn_ref, 0)
    yn = _ssd_core(ubuf.at[1], z_s[1], dt_s[1], hout_ref, y_s, scan_refs, vl=L)
    out_ref[0, L:2 * L, :] = finish(xb_ref, yn)

    @pl.when(s % n_steps == n_steps - 1)
    def _():
        nbuf_ref[0] = ubuf[1, HIST + L - 3:HIST + L, :]


def ssd_layer(x, conv_buf, h0, w, ln_g, ln_b):
    in_w_pad, conv_w, conv_b, dt_bias, a_log, d_skip, norm_w, out_w = w
    bsz, t_len, _ = x.shape
    L = SSD_CHUNK
    gw = SSD_HPG * SSD_HEADDIM
    n_proj = in_w_pad.shape[1]
    pad_lanes = lambda v: jnp.pad(v.reshape(1, SSD_HEADS), ((0, 0), (0, LANES - SSD_HEADS)))
    dexp = jnp.repeat(d_skip, SSD_HEADDIM).reshape(1, SSD_INNER)
    h0g = h0.reshape(bsz, SSD_GROUPS, gw, SSD_STATE)
    scan_args = (conv_w, conv_b.reshape(1, SSD_CONV_DIM), pad_lanes(dt_bias), pad_lanes(a_log),
                 dexp, norm_w.reshape(1, SSD_INNER))
    scan_shapes = [(CONV_W, SSD_CONV_DIM), (1, SSD_CONV_DIM), (1, LANES), (1, LANES),
                   (1, SSD_INNER), (1, SSD_INNER)]
    state_shapes = [jax.ShapeDtypeStruct((bsz, CONV_W - 1, SSD_CONV_DIM), F32),
                    jax.ShapeDtypeStruct((bsz, SSD_GROUPS, gw, SSD_STATE), F32)]
    if t_len % (2 * L) == 0:
        n_steps = t_len // (2 * L)
        n_chunks = bsz * t_len // L
        const = lambda shape: pl.BlockSpec(shape, lambda s: (0,) * len(shape))
        resident = lambda shape: pl.BlockSpec(shape, lambda s: (0,) * len(shape),
                                              pipeline_mode=pl.Buffered(1))
        x_chunk = lambda off: pl.BlockSpec(
            (1, L, D_MODEL), lambda s: (jnp.minimum(2 * s + off, n_chunks - 1), 0, 0))
        state_specs = [pl.BlockSpec((1, CONV_W - 1, SSD_CONV_DIM), lambda s: (s // n_steps, 0, 0)),
                       pl.BlockSpec((1, SSD_GROUPS, gw, SSD_STATE),
                                    lambda s: (s // n_steps, 0, 0, 0))]
        xt = x.reshape(n_chunks, L, D_MODEL)
        out, nbuf, hout = pl.pallas_call(
            functools.partial(_ssd_layer_kernel, n_steps=n_steps),
            grid=(bsz * n_steps,),
            in_specs=([x_chunk(0), x_chunk(1), x_chunk(2)] + state_specs
                      + [resident((D_MODEL, n_proj))] + [const(sh) for sh in scan_shapes]
                      + [resident((SSD_INNER, D_MODEL)), const((1, D_MODEL)), const((1, D_MODEL))]),
            out_specs=[pl.BlockSpec((1, 2 * L, D_MODEL), lambda s: (s, 0, 0))] + state_specs,
            out_shape=[jax.ShapeDtypeStruct((bsz * n_steps, 2 * L, D_MODEL), F32)] + state_shapes,
            scratch_shapes=[pltpu.VMEM((2, HIST + L, SSD_CONV_DIM), F32),
                            pltpu.VMEM((2, L, SSD_INNER), F32),
                            pltpu.VMEM((2, L, LANES), F32),
                            pltpu.VMEM((L, SSD_INNER), F32)],
            compiler_params=_cparams(1),
            name="ssd_layer",
        )(xt, xt, xt, conv_buf, h0g, in_w_pad, *scan_args, out_w,
          ln_g.reshape(1, D_MODEL), ln_b.reshape(1, D_MODEL))
        return (out.reshape(bsz, t_len, D_MODEL), nbuf,
                hout.reshape(bsz, SSD_HEADS, SSD_HEADDIM, SSD_STATE))

    vl = t_len
    assert vl <= L
    x2 = x.reshape(bsz * t_len, D_MODEL)
    splits = ((0, SSD_INNER), (SSD_INNER, SSD_CONV_DIM), (SSD_INNER + SSD_CONV_DIM, LANES))
    z, xbc, dt_raw = proj_split(x2, in_w_pad, splits, _row_tile(bsz * t_len))
    const = lambda shape: pl.BlockSpec(shape, lambda b: (0,) * len(shape))
    rows = lambda width: pl.BlockSpec((1, vl, width), lambda b: (b, 0, 0))
    state_specs = [pl.BlockSpec((1, CONV_W - 1, SSD_CONV_DIM), lambda b: (b, 0, 0)),
                   pl.BlockSpec((1, SSD_GROUPS, gw, SSD_STATE), lambda b: (b, 0, 0, 0))]
    yn, nbuf, hout = pl.pallas_call(
        functools.partial(_ssd_core_kernel, vl=vl),
        grid=(bsz,),
        in_specs=([rows(SSD_CONV_DIM), rows(SSD_INNER), rows(LANES)] + state_specs
                  + [const(sh) for sh in scan_shapes]),
        out_specs=[rows(SSD_INNER)] + state_specs,
        out_shape=[jax.ShapeDtypeStruct((bsz, t_len, SSD_INNER), F32)] + state_shapes,
        scratch_shapes=[pltpu.VMEM((HIST + L, SSD_CONV_DIM), F32),
                        pltpu.VMEM((L, SSD_INNER), F32),
                        pltpu.VMEM((L, LANES), F32),
                        pltpu.VMEM((L, SSD_INNER), F32)],
        compiler_params=_cparams(1),
        name="ssd_core",
    )(xbc.reshape(bsz, t_len, SSD_CONV_DIM), z.reshape(bsz, t_len, SSD_INNER),
      dt_raw.reshape(bsz, t_len, LANES), conv_buf, h0g, *scan_args)
    out = outproj_ln(yn.reshape(bsz * t_len, SSD_INNER), out_w, x2, ln_g, ln_b,
                     _row_tile(bsz * t_len)).reshape(bsz, t_len, D_MODEL)
    return out, nbuf, hout.reshape(bsz, SSD_HEADS, SSD_HEADDIM, SSD_STATE)


def _row_tile(rows):
    return 256 if rows % 256 == 0 else rows


def _attn_proj_t_kernel(x_ref, w_ref, wkt_ref, q_ref, kt_ref, v_ref, g_ref):
    x = x_ref[0].astype(BF16)
    q_ref[0] = jnp.dot(x, w_ref[:, 0:ATT_W], preferred_element_type=F32)
    kt_ref[0] = _nt(wkt_ref[...], x)
    v_ref[0] = jnp.dot(x, w_ref[:, 2 * ATT_W:3 * ATT_W], preferred_element_type=F32)
    g_ref[0] = jnp.dot(x, w_ref[:, 3 * ATT_W:4 * ATT_W], preferred_element_type=F32)


def attn_project_t(x, in_w, wkt, tm=256):
    bsz, t_len, _ = x.shape
    rows = lambda: pl.BlockSpec((1, tm, ATT_W), lambda b, i: (b, i, 0))
    return pl.pallas_call(
        _attn_proj_t_kernel,
        grid=(bsz, t_len // tm),
        in_specs=[pl.BlockSpec((1, tm, D_MODEL), lambda b, i: (b, i, 0)),
                  pl.BlockSpec((D_MODEL, 4 * ATT_W), lambda b, i: (0, 0)),
                  pl.BlockSpec((ATT_W, D_MODEL), lambda b, i: (0, 0))],
        out_specs=[rows(), pl.BlockSpec((1, ATT_W, tm), lambda b, i: (b, 0, i)), rows(), rows()],
        out_shape=[jax.ShapeDtypeStruct((bsz, t_len, ATT_W), F32),
                   jax.ShapeDtypeStruct((bsz, ATT_W, t_len), F32),
                   jax.ShapeDtypeStruct((bsz, t_len, ATT_W), F32),
                   jax.ShapeDtypeStruct((bsz, t_len, ATT_W), F32)],
        compiler_params=_cparams(2),
        name="attn_project_t",
    )(x, in_w, wkt)


def _attn_project(x, in_w):
    bsz, t_len, _ = x.shape
    x2 = x.reshape(bsz * t_len, D_MODEL)
    tm = _row_tile(bsz * t_len)
    splits = tuple((i * ATT_W, ATT_W) for i in range(4))
    q, k, v, gate = proj_split(x2, in_w, splits, tm)
    shp = (bsz, t_len, ATT_W)
    return x2, tm, q.reshape(shp), k.reshape(shp), v.reshape(shp), gate.reshape(shp)


def kernel(x_prompt, x_sample, cache_k, cache_v, page_table, state_lru_conv, state_lru_h, state_ssd_conv, state_ssd_h, ln_g, ln_b, a_in_w, a_conv_w, a_conv_b, a_gate_r_w, a_gate_r_b, a_gate_i_w, a_gate_i_b, a_lambda, a_out_w, b_in_w, b_lambda, b_subln_w, b_out_w, c_in_w, c_conv_w, c_conv_b, c_dt_bias, c_a_log, c_d, c_norm_w, c_out_w):
    xp, xs = x_prompt, x_sample
    bp, bs = xp.shape[0], xs.shape[0]
    tp, ts = xp.shape[1], xs.shape[1]
    k_p, v_p, k_s, v_s = [], [], [], []
    lc_p, lh_p, lc_s, lh_s = [], [], [], []
    sc_p, sh_p, sc_s, sh_s = [], [], [], []
    for i in range(DEPTH):
        j = i // N_MIXERS
        kind = i % N_MIXERS
        if kind == 0:
            w = (a_in_w[j].astype(BF16), a_conv_w[j], a_conv_b[j], _block_diag_tiles(a_gate_r_w[j]),
                 a_gate_r_b[j], _block_diag_tiles(a_gate_i_w[j]), a_gate_i_b[j], a_lambda[j],
                 a_out_w[j].astype(BF16))
            zc = jnp.zeros((bp, CONV_W - 1, LRU_W), F32)
            zh = jnp.zeros((bp, LRU_W), F32)
            xp, c1, h1 = rglru_layer(xp, zc, zh, w, ln_g[i], ln_b[i])
            xs, c2, h2 = rglru_layer(xs, state_lru_conv[j], state_lru_h[j], w, ln_g[i], ln_b[i])
            lc_p.append(c1); lh_p.append(h1); lc_s.append(c2); lh_s.append(h2)
        elif kind == 1:
            lam_init = 0.8 - 0.6 * math.exp(-0.3 * i)
            in_w = b_in_w[j].astype(BF16)
            out_w = b_out_w[j].astype(BF16)
            wkt = b_in_w[j][:, ATT_W:2 * ATT_W].T.astype(BF16)
            q, kt, v, g = attn_project_t(xp, in_w, wkt)
            og = attn_prompt(q, kt, v, g, b_lambda[j], b_subln_w[j], lam_init)
            xp = outproj_ln(og.reshape(bp * tp, ATT_W), out_w, xp.reshape(bp * tp, D_MODEL),
                            ln_g[i], ln_b[i], _row_tile(bp * tp)).reshape(bp, tp, D_MODEL)
            k_p.append(jnp.transpose(kt.reshape(bp, ATT_HEADS, 2, ATT_DH, tp), (0, 4, 1, 2, 3)))
            v_p.append(v.reshape(bp, tp, ATT_HEADS, 2 * ATT_DH))
            x2, tm, q, k, v, g = _attn_project(xs, in_w)
            og = attn_sample(q, k, v, g, cache_k, cache_v, j, page_table, b_lambda[j],
                             b_subln_w[j], lam_init)
            xs = outproj_ln(og.reshape(bs * ts, ATT_W), out_w, x2, ln_g[i], ln_b[i], tm
                            ).reshape(bs, ts, D_MODEL)
            k_s.append(k.reshape(bs, ts, ATT_HEADS, 2, ATT_DH))
            v_s.append(v.reshape(bs, ts, ATT_HEADS, 2 * ATT_DH))
        else:
            in_w_pad = jnp.pad(c_in_w[j], ((0, 0), (0, LANES - SSD_HEADS))).astype(BF16)
            w = (in_w_pad, c_conv_w[j], c_conv_b[j], c_dt_bias[j], c_a_log[j], c_d[j], c_norm_w[j],
                 c_out_w[j].astype(BF16))
            zc = jnp.zeros((bp, CONV_W - 1, SSD_CONV_DIM), F32)
            zh = jnp.zeros((bp, SSD_HEADS, SSD_HEADDIM, SSD_STATE), F32)
            xp, c1, h1 = ssd_layer(xp, zc, zh, w, ln_g[i], ln_b[i])
            xs, c2, h2 = ssd_layer(xs, state_ssd_conv[j], state_ssd_h[j], w, ln_g[i], ln_b[i])
            sc_p.append(c1); sh_p.append(h1); sc_s.append(c2); sh_s.append(h2)
    return (xp, xs, jnp.stack(k_p), jnp.stack(v_p), jnp.stack(k_s), jnp.stack(v_s),
            jnp.stack(lc_p), jnp.stack(lh_p), jnp.stack(lc_s), jnp.stack(lh_s),
            jnp.stack(sc_p), jnp.stack(sh_p), jnp.stack(sc_s), jnp.stack(sh_s))
```

```python
import functools
import math

import jax
import jax.numpy as jnp
from jax import lax
from jax.experimental import pallas as pl
from jax.experimental.pallas import tpu as pltpu

F32 = jnp.float32
BF16 = jnp.bfloat16

D_MODEL = 1024
DEPTH = 4
PAGE_SIZE = 128
N_MIXERS = 3
DN_ALPHA = (2.0 * DEPTH) ** 0.25
EPS = 1e-5
CONV_W = 4
LRU_W = D_MODEL
LRU_BLOCKS = 16
LRU_BS = LRU_W // LRU_BLOCKS
LRU_C = 8.0
ATT_HEADS = 8
ATT_DH = D_MODEL // (2 * ATT_HEADS)
ATT_W = ATT_HEADS * 2 * ATT_DH
SSD_INNER = 2 * D_MODEL
SSD_HEADDIM = 64
SSD_HEADS = SSD_INNER // SSD_HEADDIM
SSD_GROUPS = 8
SSD_HPG = SSD_HEADS // SSD_GROUPS
SSD_STATE = 128
SSD_CONV_DIM = SSD_INNER + 2 * SSD_GROUPS * SSD_STATE
SSD_CHUNK = 128

LANES = 128
SUBLANES = 8
MXU_DIM = 256
VMEM_LIMIT = 56 * 1024 * 1024
LOG2E = math.log2(math.e)
F32_TINY = 1.1754944e-38
NEG_BIG = -1e30
HIST = SUBLANES


def _cparams(n_grid):
    return pltpu.CompilerParams(dimension_semantics=("arbitrary",) * n_grid,
                                vmem_limit_bytes=VMEM_LIMIT)


def _nt(a, b):
    return lax.dot_general(a, b, (((1,), (1,)), ((), ())), preferred_element_type=F32)


def _sigmoid(x):
    return 0.5 * jnp.tanh(0.5 * x) + 0.5


def _sqrt_nonneg(x):
    return x * lax.rsqrt(jnp.maximum(x, F32_TINY))


def _layer_norm(v, g, b):
    mu = jnp.mean(v, axis=-1, keepdims=True)
    d = v - mu
    var = jnp.mean(d * d, axis=-1, keepdims=True)
    return d * lax.rsqrt(var + EPS) * g + b


def _silu(x):
    return x * _sigmoid(x)


def _softplus(x):
    return jnp.maximum(x, 0.0) + jnp.log(1.0 + jnp.exp(-jnp.abs(x)))


def _proj_kernel(x_ref, w_ref, *out_refs, splits):
    x = x_ref[...].astype(BF16)
    for (off, width), o_ref in zip(splits, out_refs):
        o_ref[...] = jnp.dot(x, w_ref[:, off:off + width], preferred_element_type=F32)


def proj_split(x2d, w_bf16, splits, tm):
    rows, kdim = x2d.shape
    ndim = w_bf16.shape[1]
    return pl.pallas_call(
        functools.partial(_proj_kernel, splits=splits),
        grid=(rows // tm,),
        in_specs=[pl.BlockSpec((tm, kdim), lambda i: (i, 0)),
                  pl.BlockSpec((kdim, ndim), lambda i: (0, 0))],
        out_specs=[pl.BlockSpec((tm, wd), lambda i: (i, 0)) for _, wd in splits],
        out_shape=[jax.ShapeDtypeStruct((rows, wd), F32) for _, wd in splits],
        compiler_params=_cparams(1),
        name="proj_split",
    )(x2d, w_bf16)


def _outproj_ln_kernel(y_ref, w_ref, x_ref, g_ref, b_ref, o_ref):
    f = jnp.dot(y_ref[...].astype(BF16), w_ref[...], preferred_element_type=F32)
    o_ref[...] = _layer_norm(DN_ALPHA * x_ref[...] + f, g_ref[...], b_ref[...])


def outproj_ln(y2d, w_bf16, x2d, g, b, tm):
    rows, kdim = y2d.shape
    return pl.pallas_call(
        _outproj_ln_kernel,
        grid=(rows // tm,),
        in_specs=[pl.BlockSpec((tm, kdim), lambda i: (i, 0)),
                  pl.BlockSpec((kdim, D_MODEL), lambda i: (0, 0)),
                  pl.BlockSpec((tm, D_MODEL), lambda i: (i, 0)),
                  pl.BlockSpec((1, D_MODEL), lambda i: (0, 0)),
                  pl.BlockSpec((1, D_MODEL), lambda i: (0, 0))],
        out_specs=pl.BlockSpec((tm, D_MODEL), lambda i: (i, 0)),
        out_shape=jax.ShapeDtypeStruct((rows, D_MODEL), F32),
        compiler_params=_cparams(1),
        name="outproj_ln",
    )(y2d, w_bf16, x2d, g.reshape(1, D_MODEL), b.reshape(1, D_MODEL))


def _rglru_core(ub, gate, a_s, b_s, h_in, gate_refs, *, tt, unroll):
    cw_ref, cb_ref, wr_ref, rb_ref, wi_ref, ib_ref, lam_ref = gate_refs
    cw = cw_ref[...]
    ext = ub[...]
    y = cb_ref[...] + cw[3:4, :] * ext[HIST:, :]
    for k in range(1, CONV_W):
        y = y + cw[3 - k:4 - k, :] * pltpu.roll(ext, k, 0)[HIST:, :]

    sp = _softplus(-lam_ref[...])
    yb = y.astype(BF16)
    for j in range(LRU_W // MXU_DIM):
        sl = slice(j * MXU_DIM, (j + 1) * MXU_DIM)
        ys = yb[:, sl]
        r = _sigmoid(jnp.dot(ys, wr_ref[j], preferred_element_type=F32) + rb_ref[:, sl])
        ig = _sigmoid(jnp.dot(ys, wi_ref[j], preferred_element_type=F32) + ib_ref[:, sl])
        log_a = -LRU_C * r * sp[:, sl]
        a = jnp.exp(log_a)
        a_s[:, sl] = a
        b_s[:, sl] = _sqrt_nonneg(-jnp.tanh(log_a) * (a * a + 1.0)) * (ig * y[:, sl])

    row = lax.broadcasted_iota(jnp.int32, (SUBLANES, LRU_W), 0)

    def group(r0, hc):
        a = a_s[pl.ds(r0, SUBLANES), :]
        b = b_s[pl.ds(r0, SUBLANES), :]
        for d in (1, 2, 4):
            a_sh = pltpu.roll(a, d, 0)
            b_sh = pltpu.roll(b, d, 0)
            m = row >= d
            b = jnp.where(m, a * b_sh + b, b)
            a = jnp.where(m, a * a_sh, a)
        h = a * hc + b
        b_s[pl.ds(r0, SUBLANES), :] = h
        return h[SUBLANES - 1:SUBLANES, :]

    if unroll:
        hc = h_in
        for g in range(tt // SUBLANES):
            hc = group(g * SUBLANES, hc)
    else:
        lax.fori_loop(0, tt // SUBLANES,
                      lambda g, hc: group(pl.multiple_of(g * SUBLANES, SUBLANES), hc), h_in)
    return b_s[...] * _silu(gate)


def _rglru_core_kernel(gate_ref, u_ref, buf_ref, h0_ref, *refs, tt, valid):
    gate_refs = refs[:7]
    out_ref, nbuf_ref, hlast_ref, ubuf, g_s, a_s, b_s = refs[7:]
    ubuf[0:HIST - 3, :] = jnp.zeros((HIST - 3, LRU_W), F32)
    ubuf[HIST - 3:HIST, :] = buf_ref[0]
    ubuf[HIST:HIST + valid, :] = u_ref[0]
    ubuf[HIST + valid:HIST + tt, :] = jnp.zeros((tt - valid, LRU_W), F32)
    g_s[0:valid, :] = gate_ref[0]
    g_s[valid:tt, :] = jnp.zeros((tt - valid, LRU_W), F32)
    hg = _rglru_core(ubuf, g_s[...], a_s, b_s, h0_ref[0], gate_refs, tt=tt, unroll=False)
    out_ref[0] = hg[0:valid, :]
    nbuf_ref[0] = ubuf[HIST + valid - 3:HIST + valid, :]
    hlast_ref[0] = b_s[valid - 1:valid, :]


def _rglru_layer_kernel(xa_ref, xb_ref, xn_ref, buf_ref, h0_ref, win_ref, *refs, tt, n_steps):
    gate_refs = refs[:7]
    wout_ref, lng_ref, lnb_ref, out_ref, nbuf_ref, hlast_ref, ubuf, g_s, a_s, b_s, hcar = refs[7:]
    s = pl.program_id(0)
    first = s % n_steps == 0

    def project(x_ref, slot):
        xb = x_ref[0].astype(BF16)
        g_s[slot] = jnp.dot(xb, win_ref[:, 0:LRU_W], preferred_element_type=F32)
        ubuf[slot, HIST:HIST + tt, :] = jnp.dot(xb, win_ref[:, LRU_W:2 * LRU_W],
                                                preferred_element_type=F32)

    def finish(x_ref, hg):
        f = jnp.dot(hg.astype(BF16), wout_ref[...], preferred_element_type=F32)
        return _layer_norm(DN_ALPHA * x_ref[0] + f, lng_ref[...], lnb_ref[...])

    @pl.when(s == 0)
    def _():
        for slot in range(2):
            ubuf[slot, 0:HIST - 3, :] = jnp.zeros((HIST - 3, LRU_W), F32)
        project(xa_ref, 0)

    @pl.when(first)
    def _():
        ubuf[0, HIST - 3:HIST, :] = buf_ref[0]
        hcar[...] = h0_ref[0]

    @pl.when(jnp.logical_not(first))
    def _():
        ubuf[0, HIST - 3:HIST, :] = ubuf[1, HIST + tt - 3:HIST + tt, :]

    project(xb_ref, 1)
    hg = _rglru_core(ubuf.at[0], g_s[0], a_s, b_s, hcar[...], gate_refs, tt=tt, unroll=True)
    h_a = b_s[tt - 1:tt, :]
    out_ref[0, 0:tt, :] = finish(xa_ref, hg)
    ubuf[1, HIST - 3:HIST, :] = ubuf[0, HIST + tt - 3:HIST + tt, :]
    project(xn_ref, 0)
    hg = _rglru_core(ubuf.at[1], g_s[1], a_s, b_s, h_a, gate_refs, tt=tt, unroll=True)
    h_b = b_s[tt - 1:tt, :]
    hcar[...] = h_b
    out_ref[0, tt:2 * tt, :] = finish(xb_ref, hg)

    @pl.when(s % n_steps == n_steps - 1)
    def _():
        nbuf_ref[0] = ubuf[1, HIST + tt - 3:HIST + tt, :]
        hlast_ref[0] = h_b


def rglru_layer(x, conv_buf, h0, w, ln_g, ln_b):
    in_w, conv_w, conv_b, wr_bd, r_b, wi_bd, i_b, lam, out_w = w
    bsz, t_len, _ = x.shape
    row = lambda v: v.reshape(1, LRU_W)
    gate_args = (conv_w, row(conv_b), wr_bd, row(r_b), wi_bd, row(i_b), row(lam))
    h0r = h0.reshape(bsz, 1, LRU_W)
    state_shapes = [jax.ShapeDtypeStruct((bsz, CONV_W - 1, LRU_W), F32),
                    jax.ShapeDtypeStruct((bsz, 1, LRU_W), F32)]
    tt = 256
    if t_len % (2 * tt) == 0:
        n_steps = t_len // (2 * tt)
        n_tiles = bsz * t_len // tt
        const = lambda shape: pl.BlockSpec(shape, lambda s: (0,) * len(shape))
        x_tile = lambda off: pl.BlockSpec(
            (1, tt, D_MODEL), lambda s: (jnp.minimum(2 * s + off, n_tiles - 1), 0, 0))
        per_seq = lambda r: pl.BlockSpec((1, r, LRU_W), lambda s: (s // n_steps, 0, 0))
        gate_specs = [const((CONV_W, LRU_W)), const((1, LRU_W)),
                      const((LRU_W // MXU_DIM, MXU_DIM, MXU_DIM)), const((1, LRU_W)),
                      const((LRU_W // MXU_DIM, MXU_DIM, MXU_DIM)), const((1, LRU_W)),
                      const((1, LRU_W))]
        xt = x.reshape(n_tiles, tt, D_MODEL)
        out, nbuf, hlast = pl.pallas_call(
            functools.partial(_rglru_layer_kernel, tt=tt, n_steps=n_steps),
            grid=(bsz * n_steps,),
            in_specs=([x_tile(0), x_tile(1), x_tile(2), per_seq(CONV_W - 1), per_seq(1),
                       const((D_MODEL, 2 * LRU_W))] + gate_specs
                      + [const((LRU_W, D_MODEL)), const((1, D_MODEL)), const((1, D_MODEL))]),
            out_specs=[pl.BlockSpec((1, 2 * tt, D_MODEL), lambda s: (s, 0, 0)),
                       per_seq(CONV_W - 1), per_seq(1)],
            out_shape=[jax.ShapeDtypeStruct((bsz * n_steps, 2 * tt, D_MODEL), F32)] + state_shapes,
            scratch_shapes=[pltpu.VMEM((2, HIST + tt, LRU_W), F32),
                            pltpu.VMEM((2, tt, LRU_W), F32),
                            pltpu.VMEM((tt, LRU_W), F32),
                            pltpu.VMEM((tt, LRU_W), F32),
                            pltpu.VMEM((1, LRU_W), F32)],
            compiler_params=_cparams(1),
            name="rglru_layer",
        )(xt, xt, xt, conv_buf, h0r, in_w, *gate_args, out_w, row(ln_g), row(ln_b))
        return out.reshape(bsz, t_len, D_MODEL), nbuf, hlast.reshape(bsz, LRU_W)

    tt, valid = 16, t_len
    assert t_len <= tt
    x2 = x.reshape(bsz * t_len, D_MODEL)
    gate, u = proj_split(x2, in_w, ((0, LRU_W), (LRU_W, LRU_W)), _row_tile(bsz * t_len))
    vec = lambda: pl.BlockSpec((1, LRU_W), lambda b: (0, 0))
    wspec = lambda: pl.BlockSpec((LRU_W // MXU_DIM, MXU_DIM, MXU_DIM), lambda b: (0, 0, 0))
    per_seq = lambda r: pl.BlockSpec((1, r, LRU_W), lambda b: (b, 0, 0))
    hg, nbuf, hlast = pl.pallas_call(
        functools.partial(_rglru_core_kernel, tt=tt, valid=valid),
        grid=(bsz,),
        in_specs=[per_seq(valid), per_seq(valid), per_seq(CONV_W - 1), per_seq(1),
                  pl.BlockSpec((CONV_W, LRU_W), lambda b: (0, 0)),
                  vec(), wspec(), vec(), wspec(), vec(), vec()],
        out_specs=[per_seq(valid), per_seq(CONV_W - 1), per_seq(1)],
        out_shape=[jax.ShapeDtypeStruct((bsz, t_len, LRU_W), F32)] + state_shapes,
        scratch_shapes=[pltpu.VMEM((HIST + tt, LRU_W), F32),
                        pltpu.VMEM((tt, LRU_W), F32),
                        pltpu.VMEM((tt, LRU_W), F32),
                        pltpu.VMEM((tt, LRU_W), F32)],
        compiler_params=_cparams(1),
        name="rglru_core",
    )(gate.reshape(bsz, t_len, LRU_W), u.reshape(bsz, t_len, LRU_W), conv_buf, h0r, *gate_args)
    out = outproj_ln(hg.reshape(bsz * t_len, LRU_W), out_w, x2, ln_g, ln_b,
                     _row_tile(bsz * t_len)).reshape(bsz, t_len, D_MODEL)
    return out, nbuf, hlast.reshape(bsz, LRU_W)


def _block_diag_tiles(w):
    per = MXU_DIM // LRU_BS
    w4 = w.reshape(LRU_W // MXU_DIM, per, LRU_BS, LRU_BS)
    eye = jnp.eye(per, dtype=w.dtype)
    t = jnp.einsum("jakc,ab->jakbc", w4, eye)
    return t.reshape(LRU_W // MXU_DIM, MXU_DIM, MXU_DIM).astype(BF16)


def _diff_lambda(lp, lam_init):
    s1 = jnp.sum(lp[0:1, :] * lp[1:2, :], axis=-1, keepdims=True)
    s2 = jnp.sum(lp[2:3, :] * lp[3:4, :], axis=-1, keepdims=True)
    return jnp.exp(s1) - jnp.exp(s2) + lam_init


def _subln_gate(o, sw, gate, lam_init):
    ms = jnp.mean(o * o, axis=-1, keepdims=True)
    return (o * lax.rsqrt(ms + EPS) * sw * (1.0 - lam_init)) * _silu(gate)


def _attn_prompt_kernel(lp_ref, q_ref, kt_ref, v_ref, gate_ref, sw_ref, o_ref,
                        kb_s, vx_s, m_s, accl_s, s_s, *, tq, lam_init):
    qi = pl.program_id(2)
    hd = 2 * ATT_DH
    t_len = v_ref.shape[1]

    @pl.when(qi == 0)
    def _():
        for i in range(t_len // tq):
            kb_s[i] = kt_ref[0, :, i * tq:(i + 1) * tq].astype(BF16)

        def cvt(i, carry):
            r = pl.multiple_of(i * tq, tq)
            vx_s[pl.ds(r, tq), 0:hd] = v_ref[0, pl.ds(r, tq), :].astype(BF16)
            vx_s[pl.ds(r, tq), hd:2 * hd] = jnp.ones((tq, hd), BF16)
            return carry
        lax.fori_loop(0, t_len // tq, cvt, 0)

    lam = _diff_lambda(lp_ref[...], lam_init)
    q = q_ref[0] * (ATT_DH ** -0.5 * LOG2E)
    lane = lax.broadcasted_iota(jnp.int32, (tq, hd), 1)
    q_maps = (jnp.where(lane < ATT_DH, q, 0.0).astype(BF16),
              jnp.where(lane >= ATT_DH, q, 0.0).astype(BF16))
    m_s[...] = jnp.full(m_s.shape, NEG_BIG, F32)
    accl_s[...] = jnp.zeros(accl_s.shape, F32)
    rowi = lax.broadcasted_iota(jnp.int32, (tq, tq), 0)
    coli = lax.broadcasted_iota(jnp.int32, (tq, tq), 1)

    def scores(ki, slot):
        kb = kb_s[ki]
        for c in range(2):
            s_s[slot, c] = jnp.dot(q_maps[c], kb, preferred_element_type=F32)

    def absorb(ki, slot, masked):
        ks = pl.multiple_of(ki * tq, tq)
        vx = vx_s[pl.ds(ks, tq), :]
        for c in range(2):
            s = s_s[slot, c]
            if masked:
                s = jnp.where(coli <= rowi, s, NEG_BIG)
            m_prev = m_s[c]
            m_new = jnp.maximum(m_prev, jnp.max(s, axis=-1, keepdims=True))
            alpha = jnp.exp2(m_prev - m_new)
            p = jnp.exp2(s - jnp.concatenate([m_new] * (tq // hd), axis=1))
            pv = jnp.dot(p.astype(BF16), vx, preferred_element_type=F32)
            accl_s[c] = jnp.concatenate([alpha, alpha], axis=1) * accl_s[c] + pv
            m_s[c] = m_new

    scores(0, 0)

    def pair(j, carry):
        k0 = 2 * j
        scores(k0 + 1, 1)
        absorb(k0, 0, False)
        scores(k0 + 2, 0)
        absorb(k0 + 1, 1, False)
        return carry

    lax.fori_loop(0, qi // 2, pair, 0)

    @pl.when(qi % 2 == 0)
    def _():
        absorb(qi, 0, True)

    @pl.when(qi % 2 == 1)
    def _():
        scores(qi, 1)
        absorb(qi - 1, 0, False)
        absorb(qi, 1, True)

    a1 = accl_s[0]
    a2 = accl_s[1]
    o = a1[:, 0:hd] / a1[:, hd:2 * hd] - lam * (a2[:, 0:hd] / a2[:, hd:2 * hd])
    o_ref[0] = _subln_gate(o, sw_ref[...], gate_ref[0], lam_init)


def _attn_sample_kernel(pt_ref, lp_ref, qbd_ref, *refs, n_pg, t_new, lam_init):
    k_refs = refs[:n_pg]
    v_refs = refs[n_pg:2 * n_pg]
    knew_ref, vnew_ref, gate_ref, sw_ref, o_ref, m_s, l_s, acc_s = refs[2 * n_pg:]
    j = pl.program_id(2)
    n_j = pl.num_programs(2)
    n_rows = ATT_HEADS * 2 * t_new
    hd = 2 * ATT_DH

    @pl.when(j == 0)
    def _():
        m_s[...] = jnp.full(m_s.shape, NEG_BIG, F32)
        l_s[...] = jnp.zeros(l_s.shape, F32)
        acc_s[...] = jnp.zeros(acc_s.shape, F32)

    qbd = qbd_ref[0]

    def attend(kts, vbs, mask):
        n = len(kts)
        s = jnp.concatenate([jnp.dot(qbd, kt.astype(BF16), preferred_element_type=F32)
                             for kt in kts], axis=1)
        if mask is not None:
            s = jnp.where(mask, s, NEG_BIG)
        m_prev = m_s[...]
        m_new = jnp.maximum(m_prev, jnp.max(s, axis=-1, keepdims=True))
        alpha = jnp.exp2(m_prev - m_new)
        p = jnp.exp2(s - jnp.concatenate([m_new] * n, axis=1))
        l_s[...] = alpha * l_s[...] + jnp.sum(p, axis=-1, keepdims=True)
        pb = p.astype(BF16)
        pv = jnp.dot(pb[:, 0:PAGE_SIZE], vbs[0].astype(BF16), preferred_element_type=F32)
        for i in range(1, n):
            pv = pv + jnp.dot(pb[:, i * PAGE_SIZE:(i + 1) * PAGE_SIZE], vbs[i].astype(BF16),
                              preferred_element_type=F32)
        acc_s[...] = jnp.concatenate([alpha] * (ATT_W // LANES), axis=1) * acc_s[...] + pv
        m_s[...] = m_new

    def page_v(vr):
        return jnp.concatenate([vr[0, pl.ds(h, PAGE_SIZE, stride=ATT_HEADS), :]
                                for h in range(ATT_HEADS)], axis=1)

    attend([r[0] for r in k_refs], [page_v(r) for r in v_refs], None)

    @pl.when(j == n_j - 1)
    def _():
        rowi = lax.broadcasted_iota(jnp.int32, (n_rows, PAGE_SIZE), 0)
        coli = lax.broadcasted_iota(jnp.int32, (n_rows, PAGE_SIZE), 1)
        attend([knew_ref[0]], [vnew_ref[0]], coli <= (rowi % t_new))
        lam = _diff_lambda(lp_ref[...], lam_init)
        sw = sw_ref[...]
        for h in range(ATT_HEADS):
            r1 = h * 2 * t_new
            r2 = r1 + t_new
            cs = slice(h * hd, (h + 1) * hd)
            o1 = acc_s[r1:r1 + t_new, cs] / l_s[r1:r1 + t_new, :]
            o2 = acc_s[r2:r2 + t_new, cs] / l_s[r2:r2 + t_new, :]
            o_ref[0, :, cs] = _subln_gate(o1 - lam * o2, sw, gate_ref[0, :, cs], lam_init)


def _attn_kernel(pt_ref, lp_ref, q_ref, kt_ref, v_ref, gate_ref, sw_ref, qbd_ref, *refs,
                 n_pg, tq, t_new, lam_init):
    sample_in = refs[:2 * n_pg + 3]
    o_ref, os_ref, kb_s, vx_s, m_s, accl_s, s_s, ms_s, ls_s, accs_s = refs[2 * n_pg + 3:]
    _attn_prompt_kernel(lp_ref, q_ref, kt_ref, v_ref, gate_ref, sw_ref, o_ref,
                        kb_s, vx_s, m_s, accl_s, s_s, tq=tq, lam_init=lam_init)
    _attn_sample_kernel(pt_ref, lp_ref, qbd_ref, *sample_in, sw_ref, os_ref, ms_s, ls_s, accs_s,
                        n_pg=n_pg, t_new=t_new, lam_init=lam_init)


def attn_core(q, kt, v, gate, qs, ks, vs, gate_s, cache_k, cache_v, layer, page_table, lam_p,
              subln_w, lam_init, tq=512):
    bsz_p, t_len, _ = q.shape
    bsz, t_new, _ = qs.shape
    n_pages = page_table.shape[1]
    n_pool = cache_k.shape[1]
    hd = 2 * ATT_DH
    n_rows = ATT_HEADS * 2 * t_new
    n_q = t_len // tq
    assert bsz == bsz_p * ATT_HEADS and n_pages % n_q == 0
    n_pg = n_pages // n_q
    q4 = (qs * (ATT_DH ** -0.5 * LOG2E)).reshape(bsz, t_new, 2 * ATT_HEADS, ATT_DH)
    eye = jnp.eye(2 * ATT_HEADS, dtype=F32)
    qbd = jnp.einsum("bqhd,hg->bhqgd", q4, eye).reshape(bsz, n_rows, ATT_W).astype(BF16)
    k_new = jnp.pad(jnp.swapaxes(ks, 1, 2), ((0, 0), (0, 0), (0, PAGE_SIZE - t_new)))
    v_new = jnp.pad(vs, ((0, 0), (0, PAGE_SIZE - t_new), (0, 0)))
    ck = jnp.transpose(cache_k, (0, 1, 3, 4, 5, 2)).reshape(-1, ATT_W, PAGE_SIZE)
    cv = cache_v.reshape(-1, PAGE_SIZE * ATT_HEADS, hd)
    pt = page_table.reshape(-1) + layer * n_pool

    sample = lambda b, h: b * ATT_HEADS + h

    def page_spec(i, rows, width):
        return pl.BlockSpec(
            (1, rows, width),
            lambda b, h, j, pt_ref: (pt_ref[sample(b, h) * n_pages + j * n_pg + i], 0, 0))

    const = lambda shape: pl.BlockSpec(shape, lambda b, h, j, pt_ref: (0,) * len(shape))
    qspec = lambda: pl.BlockSpec((1, tq, hd), lambda b, h, j, pt_ref: (b, j, h))
    per_s = lambda r, w: pl.BlockSpec((1, r, w), lambda b, h, j, pt_ref: (sample(b, h), 0, 0))
    grid_spec = pltpu.PrefetchScalarGridSpec(
        num_scalar_prefetch=1,
        grid=(bsz_p, ATT_HEADS, n_q),
        in_specs=([const((4, ATT_DH)), qspec(),
                   pl.BlockSpec((1, hd, t_len), lambda b, h, j, pt_ref: (b, h, 0)),
                   pl.BlockSpec((1, t_len, hd), lambda b, h, j, pt_ref: (b, 0, h)),
                   qspec(), const((1, hd)), per_s(n_rows, ATT_W)]
                  + [page_spec(i, ATT_W, PAGE_SIZE) for i in range(n_pg)]
                  + [page_spec(i, PAGE_SIZE * ATT_HEADS, hd) for i in range(n_pg)]
                  + [per_s(ATT_W, PAGE_SIZE), per_s(PAGE_SIZE, ATT_W), per_s(t_new, ATT_W)]),
        out_specs=[qspec(), per_s(t_new, ATT_W)],
        scratch_shapes=[pltpu.VMEM((n_q, hd, tq), BF16),
                        pltpu.VMEM((t_len, 2 * hd), BF16),
                        pltpu.VMEM((2, tq, hd), F32), pltpu.VMEM((2, tq, 2 * hd), F32),
                        pltpu.VMEM((2, 2, tq, tq), F32),
                        pltpu.VMEM((n_rows, LANES), F32), pltpu.VMEM((n_rows, LANES), F32),
                        pltpu.VMEM((n_rows, ATT_W), F32)],
    )
    return pl.pallas_call(
        functools.partial(_attn_kernel, n_pg=n_pg, tq=tq, t_new=t_new, lam_init=lam_init),
        grid_spec=grid_spec,
        out_shape=[jax.ShapeDtypeStruct((bsz_p, t_len, ATT_W), F32),
                   jax.ShapeDtypeStruct((bsz, t_new, ATT_W), F32)],
        compiler_params=_cparams(3),
        name="attn_core",
    )(pt, lam_p, q, kt, v, gate, subln_w.reshape(1, hd), qbd, *([ck] * n_pg), *([cv] * n_pg),
      k_new, v_new, gate_s)


def _ssd_core(ub, z, dt_raw, hout_ref, y_s, scan_refs, *, vl):
    cw_ref, cb_ref, dtb_ref, alog_ref, dexp_ref, nw_ref = scan_refs
    L = SSD_CHUNK
    gw = SSD_HPG * SSD_HEADDIM
    rowL = lax.broadcasted_iota(jnp.int32, (L, LANES), 0)
    laneL = lax.broadcasted_iota(jnp.int32, (L, LANES), 1)
    dt = _softplus(dt_raw + dtb_ref[...])
    dt = jnp.where((rowL < vl) & (laneL < SSD_HEADS), dt, 0.0)
    adt = dt * (-jnp.exp(alog_ref[...]))
    tri = (lax.broadcasted_iota(jnp.int32, (L, L), 0)
           >= lax.broadcasted_iota(jnp.int32, (L, L), 1))
    cs = jnp.dot(tri.astype(F32), adt, preferred_element_type=F32,
                 precision=lax.Precision.HIGHEST)
    cs_last = cs[L - 1:L, :]
    w1 = dt * jnp.exp(cs_last - cs)
    cs_t = cs.T
    dt_t = dt.T
    w1_t = w1.T

    cw = cw_ref[...]
    lane_g = lax.broadcasted_iota(jnp.int32, (L, gw), 1)

    def conv_silu(lo, width):
        sl = slice(lo, lo + width)
        y = cb_ref[:, sl] + cw[3:4, sl] * ub[HIST:HIST + L, sl]
        for k in range(1, CONV_W):
            y = y + cw[3 - k:4 - k, sl] * ub[HIST - k:HIST - k + L, sl]
        return _silu(y)

    for g in range(SSD_GROUPS):
        xh = conv_silu(g * gw, gw)
        bm = conv_silu(SSD_INNER + g * SSD_STATE, SSD_STATE)
        cm = conv_silu(SSD_INNER + (SSD_GROUPS + g) * SSD_STATE, SSD_STATE)
        bmb = bm.astype(BF16)
        cmb = cm.astype(BF16)
        xhb = xh.astype(BF16)
        cbm = _nt(cmb, bmb)
        y_diag = jnp.zeros((L, gw), F32)
        e_cols = []
        w_rows = []
        d_rows = []
        for r in range(SSD_HPG):
            h = g * SSD_HPG + r
            cs_col = cs[:, h:h + 1]
            cs_row = cs_t[h:h + 1, :]
            lm = jnp.exp(jnp.where(tri, cs_col - cs_row, NEG_BIG))
            mat = (cbm * lm * dt_t[h:h + 1, :]).astype(BF16)
            in_head = (lane_g >= r * SSD_HEADDIM) & (lane_g < (r + 1) * SSD_HEADDIM)
            xr = jnp.where(in_head, xhb, jnp.zeros_like(xhb))
            y_diag = y_diag + jnp.dot(mat, xr, preferred_element_type=F32)
            e_cols.append(jnp.broadcast_to(jnp.exp(cs_col), (L, SSD_HEADDIM)))
            w_rows.append(jnp.broadcast_to(w1_t[h:h + 1, :], (SSD_HEADDIM, L)))
            d_rows.append(jnp.broadcast_to(jnp.exp(cs_t[h:h + 1, L - 1:L]),
                                           (SSD_HEADDIM, SSD_STATE)))
        hg = hout_ref[0, g]
        y_off = _nt(cmb, hg.astype(BF16)) * jnp.concatenate(e_cols, axis=1)
        y_s[:, g * gw:(g + 1) * gw] = y_diag + y_off + dexp_ref[:, g * gw:(g + 1) * gw] * xh
        xd_t = (xh.T * jnp.concatenate(w_rows, axis=0)).astype(BF16)
        states = jnp.dot(xd_t, bmb, preferred_element_type=F32)
        hout_ref[0, g] = hg * jnp.concatenate(d_rows, axis=0) + states

    gated = y_s[...] * _silu(z)
    ms = jnp.mean(gated * gated, axis=-1, keepdims=True)
    return gated * lax.rsqrt(ms + EPS) * nw_ref[...]


def _ssd_core_kernel(xbc_ref, z_ref, dtr_ref, buf_ref, h0_ref, *refs, vl):
    scan_refs = refs[:6]
    out_ref, nbuf_ref, hout_ref, ubuf, z_s, dt_s, y_s = refs[6:]
    L = SSD_CHUNK
    ubuf[HIST - 3:HIST, :] = buf_ref[0]
    ubuf[HIST:HIST + vl, :] = xbc_ref[0]
    ubuf[HIST + vl:HIST + L, :] = jnp.zeros((L - vl, SSD_CONV_DIM), F32)
    z_s[0:vl, :] = z_ref[0]
    z_s[vl:L, :] = jnp.zeros((L - vl, SSD_INNER), F32)
    dt_s[0:vl, :] = dtr_ref[0]
    dt_s[vl:L, :] = jnp.zeros((L - vl, LANES), F32)
    hout_ref[0] = h0_ref[0]
    yn = _ssd_core(ubuf, z_s[...], dt_s[...], hout_ref, y_s, scan_refs, vl=vl)
    out_ref[0] = yn[0:vl, :]
    nbuf_ref[0] = ubuf[HIST + vl - 3:HIST + vl, :]


def _ssd_layer_kernel(xa_ref, xb_ref, xn_ref, buf_ref, h0_ref, win_ref, *refs, n_steps):
    scan_refs = refs[:6]
    (wout_ref, lng_ref, lnb_ref, out_ref, nbuf_ref, hout_ref, ubuf, z_s, dt_s, y_s) = refs[6:]
    L = SSD_CHUNK
    s = pl.program_id(0)
    first = s % n_steps == 0
    xbc_lo = SSD_INNER
    dt_lo = SSD_INNER + SSD_CONV_DIM

    def project(x_ref, slot):
        xb = x_ref[0].astype(BF16)
        z_s[slot] = jnp.dot(xb, win_ref[:, 0:xbc_lo], preferred_element_type=F32)
        ubuf[slot, HIST:HIST + L, :] = jnp.dot(xb, win_ref[:, xbc_lo:dt_lo],
                                               preferred_element_type=F32)
        dt_s[slot] = jnp.dot(xb, win_ref[:, dt_lo:dt_lo + LANES], preferred_element_type=F32)

    def finish(x_ref, yn):
        f = jnp.dot(yn.astype(BF16), wout_ref[...], preferred_element_type=F32)
        return _layer_norm(DN_ALPHA * x_ref[0] + f, lng_ref[...], lnb_ref[...])

    @pl.when(s == 0)
    def _():
        project(xa_ref, 0)

    @pl.when(first)
    def _():
        ubuf[0, HIST - 3:HIST, :] = buf_ref[0]
        hout_ref[0] = h0_ref[0]

    @pl.when(jnp.logical_not(first))
    def _():
        ubuf[0, HIST - 3:HIST, :] = ubuf[1, HIST + L - 3:HIST + L, :]

    project(xb_ref, 1)
    yn = _ssd_core(ubuf.at[0], z_s[0], dt_s[0], hout_ref, y_s, scan_refs, vl=L)
    out_ref[0, 0:L, :] = finish(xa_ref, yn)
    ubuf[1, HIST - 3:HIST, :] = ubuf[0, HIST + L - 3:HIST + L, :]
    project(xn_ref, 0)
    yn = _ssd_core(ubuf.at[1], z_s[1], dt_s[1], hout_ref, y_s, scan_refs, vl=L)
    out_ref[0, L:2 * L, :] = finish(xb_ref, yn)

    @pl.when(s % n_steps == n_steps - 1)
    def _():
        nbuf_ref[0] = ubuf[1, HIST + L - 3:HIST + L, :]


def ssd_layer(x, conv_buf, h0, w, ln_g, ln_b):
    in_w_pad, conv_w, conv_b, dt_bias, a_log, d_skip, norm_w, out_w = w
    bsz, t_len, _ = x.shape
    L = SSD_CHUNK
    gw = SSD_HPG * SSD_HEADDIM
    n_proj = in_w_pad.shape[1]
    pad_lanes = lambda v: jnp.pad(v.reshape(1, SSD_HEADS), ((0, 0), (0, LANES - SSD_HEADS)))
    dexp = jnp.repeat(d_skip, SSD_HEADDIM).reshape(1, SSD_INNER)
    h0g = h0.reshape(bsz, SSD_GROUPS, gw, SSD_STATE)
    scan_args = (conv_w, conv_b.reshape(1, SSD_CONV_DIM), pad_lanes(dt_bias), pad_lanes(a_log),
                 dexp, norm_w.reshape(1, SSD_INNER))
    scan_shapes = [(CONV_W, SSD_CONV_DIM), (1, SSD_CONV_DIM), (1, LANES), (1, LANES),
                   (1, SSD_INNER), (1, SSD_INNER)]
    state_shapes = [jax.ShapeDtypeStruct((bsz, CONV_W - 1, SSD_CONV_DIM), F32),
                    jax.ShapeDtypeStruct((bsz, SSD_GROUPS, gw, SSD_STATE), F32)]
    if t_len % (2 * L) == 0:
        n_steps = t_len // (2 * L)
        n_chunks = bsz * t_len // L
        const = lambda shape: pl.BlockSpec(shape, lambda s: (0,) * len(shape))
        resident = lambda shape: pl.BlockSpec(shape, lambda s: (0,) * len(shape),
                                              pipeline_mode=pl.Buffered(1))
        x_chunk = lambda off: pl.BlockSpec(
            (1, L, D_MODEL), lambda s: (jnp.minimum(2 * s + off, n_chunks - 1), 0, 0))
        state_specs = [pl.BlockSpec((1, CONV_W - 1, SSD_CONV_DIM), lambda s: (s // n_steps, 0, 0)),
                       pl.BlockSpec((1, SSD_GROUPS, gw, SSD_STATE),
                                    lambda s: (s // n_steps, 0, 0, 0))]
        xt = x.reshape(n_chunks, L, D_MODEL)
        out, nbuf, hout = pl.pallas_call(
            functools.partial(_ssd_layer_kernel, n_steps=n_steps),
            grid=(bsz * n_steps,),
            in_specs=([x_chunk(0), x_chunk(1), x_chunk(2)] + state_specs
                      + [resident((D_MODEL, n_proj))] + [const(sh) for sh in scan_shapes]
                      + [resident((SSD_INNER, D_MODEL)), const((1, D_MODEL)), const((1, D_MODEL))]),
            out_specs=[pl.BlockSpec((1, 2 * L, D_MODEL), lambda s: (s, 0, 0))] + state_specs,
            out_shape=[jax.ShapeDtypeStruct((bsz * n_steps, 2 * L, D_MODEL), F32)] + state_shapes,
            scratch_shapes=[pltpu.VMEM((2, HIST + L, SSD_CONV_DIM), F32),
                            pltpu.VMEM((2, L, SSD_INNER), F32),
                            pltpu.VMEM((2, L, LANES), F32),
                            pltpu.VMEM((L, SSD_INNER), F32)],
            compiler_params=_cparams(1),
            name="ssd_layer",
        )(xt, xt, xt, conv_buf, h0g, in_w_pad, *scan_args, out_w,
          ln_g.reshape(1, D_MODEL), ln_b.reshape(1, D_MODEL))
        return (out.reshape(bsz, t_len, D_MODEL), nbuf,
                hout.reshape(bsz, SSD_HEADS, SSD_HEADDIM, SSD_STATE))

    vl = t_len
    assert vl <= L
    x2 = x.reshape(bsz * t_len, D_MODEL)
    splits = ((0, SSD_INNER), (SSD_INNER, SSD_CONV_DIM), (SSD_INNER + SSD_CONV_DIM, LANES))
    z, xbc, dt_raw = proj_split(x2, in_w_pad, splits, _row_tile(bsz * t_len))
    const = lambda shape: pl.BlockSpec(shape, lambda b: (0,) * len(shape))
    rows = lambda width: pl.BlockSpec((1, vl, width), lambda b: (b, 0, 0))
    state_specs = [pl.BlockSpec((1, CONV_W - 1, SSD_CONV_DIM), lambda b: (b, 0, 0)),
                   pl.BlockSpec((1, SSD_GROUPS, gw, SSD_STATE), lambda b: (b, 0, 0, 0))]
    yn, nbuf, hout = pl.pallas_call(
        functools.partial(_ssd_core_kernel, vl=vl),
        grid=(bsz,),
        in_specs=([rows(SSD_CONV_DIM), rows(SSD_INNER), rows(LANES)] + state_specs
                  + [const(sh) for sh in scan_shapes]),
        out_specs=[rows(SSD_INNER)] + state_specs,
        out_shape=[jax.ShapeDtypeStruct((bsz, t_len, SSD_INNER), F32)] + state_shapes,
        scratch_shapes=[pltpu.VMEM((HIST + L, SSD_CONV_DIM), F32),
                        pltpu.VMEM((L, SSD_INNER), F32),
                        pltpu.VMEM((L, LANES), F32),
                        pltpu.VMEM((L, SSD_INNER), F32)],
        compiler_params=_cparams(1),
        name="ssd_core",
    )(xbc.reshape(bsz, t_len, SSD_CONV_DIM), z.reshape(bsz, t_len, SSD_INNER),
      dt_raw.reshape(bsz, t_len, LANES), conv_buf, h0g, *scan_args)
    out = outproj_ln(yn.reshape(bsz * t_len, SSD_INNER), out_w, x2, ln_g, ln_b,
                     _row_tile(bsz * t_len)).reshape(bsz, t_len, D_MODEL)
    return out, nbuf, hout.reshape(bsz, SSD_HEADS, SSD_HEADDIM, SSD_STATE)


def _row_tile(rows):
    return 256 if rows % 256 == 0 else rows


def _attn_proj_t_kernel(x_ref, w_ref, wkt_ref, q_ref, kt_ref, v_ref, g_ref):
    x = x_ref[0].astype(BF16)
    q_ref[0] = jnp.dot(x, w_ref[:, 0:ATT_W], preferred_element_type=F32)
    kt_ref[0] = _nt(wkt_ref[...], x)
    v_ref[0] = jnp.dot(x, w_ref[:, 2 * ATT_W:3 * ATT_W], preferred_element_type=F32)
    g_ref[0] = jnp.dot(x, w_ref[:, 3 * ATT_W:4 * ATT_W], preferred_element_type=F32)


def attn_project_t(x, in_w, wkt, tm=256):
    bsz, t_len, _ = x.shape
    rows = lambda: pl.BlockSpec((1, tm, ATT_W), lambda b, i: (b, i, 0))
    return pl.pallas_call(
        _attn_proj_t_kernel,
        grid=(bsz, t_len // tm),
        in_specs=[pl.BlockSpec((1, tm, D_MODEL), lambda b, i: (b, i, 0)),
                  pl.BlockSpec((D_MODEL, 4 * ATT_W), lambda b, i: (0, 0)),
                  pl.BlockSpec((ATT_W, D_MODEL), lambda b, i: (0, 0))],
        out_specs=[rows(), pl.BlockSpec((1, ATT_W, tm), lambda b, i: (b, 0, i)), rows(), rows()],
        out_shape=[jax.ShapeDtypeStruct((bsz, t_len, ATT_W), F32),
                   jax.ShapeDtypeStruct((bsz, ATT_W, t_len), F32),
                   jax.ShapeDtypeStruct((bsz, t_len, ATT_W), F32),
                   jax.ShapeDtypeStruct((bsz, t_len, ATT_W), F32)],
        compiler_params=_cparams(2),
        name="attn_project_t",
    )(x, in_w, wkt)


def _attn_project(x, in_w):
    bsz, t_len, _ = x.shape
    x2 = x.reshape(bsz * t_len, D_MODEL)
    tm = _row_tile(bsz * t_len)
    splits = tuple((i * ATT_W, ATT_W) for i in range(4))
    q, k, v, gate = proj_split(x2, in_w, splits, tm)
    shp = (bsz, t_len, ATT_W)
    return x2, tm, q.reshape(shp), k.reshape(shp), v.reshape(shp), gate.reshape(shp)


def kernel(x_prompt, x_sample, cache_k, cache_v, page_table, state_lru_conv, state_lru_h, state_ssd_conv, state_ssd_h, ln_g, ln_b, a_in_w, a_conv_w, a_conv_b, a_gate_r_w, a_gate_r_b, a_gate_i_w, a_gate_i_b, a_lambda, a_out_w, b_in_w, b_lambda, b_subln_w, b_out_w, c_in_w, c_conv_w, c_conv_b, c_dt_bias, c_a_log, c_d, c_norm_w, c_out_w):
    xp, xs = x_prompt, x_sample
    bp, bs = xp.shape[0], xs.shape[0]
    tp, ts = xp.shape[1], xs.shape[1]
    k_p, v_p, k_s, v_s = [], [], [], []
    lc_p, lh_p, lc_s, lh_s = [], [], [], []
    sc_p, sh_p, sc_s, sh_s = [], [], [], []
    for i in range(DEPTH):
        j = i // N_MIXERS
        kind = i % N_MIXERS
        if kind == 0:
            w = (a_in_w[j].astype(BF16), a_conv_w[j], a_conv_b[j], _block_diag_tiles(a_gate_r_w[j]),
                 a_gate_r_b[j], _block_diag_tiles(a_gate_i_w[j]), a_gate_i_b[j], a_lambda[j],
                 a_out_w[j].astype(BF16))
            zc = jnp.zeros((bp, CONV_W - 1, LRU_W), F32)
            zh = jnp.zeros((bp, LRU_W), F32)
            xp, c1, h1 = rglru_layer(xp, zc, zh, w, ln_g[i], ln_b[i])
            xs, c2, h2 = rglru_layer(xs, state_lru_conv[j], state_lru_h[j], w, ln_g[i], ln_b[i])
            lc_p.append(c1); lh_p.append(h1); lc_s.append(c2); lh_s.append(h2)
        elif kind == 1:
            lam_init = 0.8 - 0.6 * math.exp(-0.3 * i)
            in_w = b_in_w[j].astype(BF16)
            out_w = b_out_w[j].astype(BF16)
            wkt = b_in_w[j][:, ATT_W:2 * ATT_W].T.astype(BF16)
            q, kt, v, g = attn_project_t(xp, in_w, wkt)
            x2s, tms, qs, ks, vs, gs = _attn_project(xs, in_w)
            og, ogs = attn_core(q, kt, v, g, qs, ks, vs, gs, cache_k, cache_v, j, page_table,
                                b_lambda[j], b_subln_w[j], lam_init)
            xp = outproj_ln(og.reshape(bp * tp, ATT_W), out_w, xp.reshape(bp * tp, D_MODEL),
                            ln_g[i], ln_b[i], _row_tile(bp * tp)).reshape(bp, tp, D_MODEL)
            xs = outproj_ln(ogs.reshape(bs * ts, ATT_W), out_w, x2s, ln_g[i], ln_b[i], tms
                            ).reshape(bs, ts, D_MODEL)
            k_p.append(jnp.transpose(kt.reshape(bp, ATT_HEADS, 2, ATT_DH, tp), (0, 4, 1, 2, 3)))
            v_p.append(v.reshape(bp, tp, ATT_HEADS, 2 * ATT_DH))
            k_s.append(ks.reshape(bs, ts, ATT_HEADS, 2, ATT_DH))
            v_s.append(vs.reshape(bs, ts, ATT_HEADS, 2 * ATT_DH))
        else:
            in_w_pad = jnp.pad(c_in_w[j], ((0, 0), (0, LANES - SSD_HEADS))).astype(BF16)
            w = (in_w_pad, c_conv_w[j], c_conv_b[j], c_dt_bias[j], c_a_log[j], c_d[j], c_norm_w[j],
                 c_out_w[j].astype(BF16))
            zc = jnp.zeros((bp, CONV_W - 1, SSD_CONV_DIM), F32)
            zh = jnp.zeros((bp, SSD_HEADS, SSD_HEADDIM, SSD_STATE), F32)
            xp, c1, h1 = ssd_layer(xp, zc, zh, w, ln_g[i], ln_b[i])
            xs, c2, h2 = ssd_layer(xs, state_ssd_conv[j], state_ssd_h[j], w, ln_g[i], ln_b[i])
            sc_p.append(c1); sh_p.append(h1); sc_s.append(c2); sh_s.append(h2)
    return (xp, xs, jnp.stack(k_p), jnp.stack(v_p), jnp.stack(k_s), jnp.stack(v_s),
            jnp.stack(lc_p), jnp.stack(lh_p), jnp.stack(lc_s), jnp.stack(lh_s),
            jnp.stack(sc_p), jnp.stack(sh_p), jnp.stack(sc_s), jnp.stack(sh_s))
```

```python
import functools
import math

import jax
import jax.numpy as jnp
from jax import lax
from jax.experimental import pallas as pl
from jax.experimental.pallas import tpu as pltpu

F32 = jnp.float32
BF16 = jnp.bfloat16

D_MODEL = 1024
DEPTH = 4
PAGE_SIZE = 128
N_MIXERS = 3
DN_ALPHA = (2.0 * DEPTH) ** 0.25
EPS = 1e-5
CONV_W = 4
LRU_W = D_MODEL
LRU_BLOCKS = 16
LRU_BS = LRU_W // LRU_BLOCKS
LRU_C = 8.0
ATT_HEADS = 8
ATT_DH = D_MODEL // (2 * ATT_HEADS)
ATT_W = ATT_HEADS * 2 * ATT_DH
SSD_INNER = 2 * D_MODEL
SSD_HEADDIM = 64
SSD_HEADS = SSD_INNER // SSD_HEADDIM
SSD_GROUPS = 8
SSD_HPG = SSD_HEADS // SSD_GROUPS
SSD_STATE = 128
SSD_CONV_DIM = SSD_INNER + 2 * SSD_GROUPS * SSD_STATE
SSD_CHUNK = 128

LANES = 128
SUBLANES = 8
MXU_DIM = 256
VMEM_LIMIT = 56 * 1024 * 1024
LOG2E = math.log2(math.e)
F32_TINY = 1.1754944e-38
NEG_BIG = -1e30
HIST = SUBLANES


def _cparams(n_grid):
    return pltpu.CompilerParams(dimension_semantics=("arbitrary",) * n_grid,
                                vmem_limit_bytes=VMEM_LIMIT)


def _nt(a, b):
    return lax.dot_general(a, b, (((1,), (1,)), ((), ())), preferred_element_type=F32)


def _sigmoid(x):
    return 0.5 * jnp.tanh(0.5 * x) + 0.5


def _sqrt_nonneg(x):
    return x * lax.rsqrt(jnp.maximum(x, F32_TINY))


def _layer_norm(v, g, b):
    mu = jnp.mean(v, axis=-1, keepdims=True)
    d = v - mu
    var = jnp.mean(d * d, axis=-1, keepdims=True)
    return d * lax.rsqrt(var + EPS) * g + b


def _silu(x):
    return x * _sigmoid(x)


def _softplus(x):
    return jnp.maximum(x, 0.0) + jnp.log(1.0 + jnp.exp(-jnp.abs(x)))


def _proj_kernel(x_ref, w_ref, *out_refs, splits):
    x = x_ref[...].astype(BF16)
    for (off, width), o_ref in zip(splits, out_refs):
        o_ref[...] = jnp.dot(x, w_ref[:, off:off + width], preferred_element_type=F32)


def proj_split(x2d, w_bf16, splits, tm):
    rows, kdim = x2d.shape
    ndim = w_bf16.shape[1]
    return pl.pallas_call(
        functools.partial(_proj_kernel, splits=splits),
        grid=(rows // tm,),
        in_specs=[pl.BlockSpec((tm, kdim), lambda i: (i, 0)),
                  pl.BlockSpec((kdim, ndim), lambda i: (0, 0))],
        out_specs=[pl.BlockSpec((tm, wd), lambda i: (i, 0)) for _, wd in splits],
        out_shape=[jax.ShapeDtypeStruct((rows, wd), F32) for _, wd in splits],
        compiler_params=_cparams(1),
        name="proj_split",
    )(x2d, w_bf16)


def _outproj_ln_kernel(y_ref, w_ref, x_ref, g_ref, b_ref, o_ref):
    f = jnp.dot(y_ref[...].astype(BF16), w_ref[...], preferred_element_type=F32)
    o_ref[...] = _layer_norm(DN_ALPHA * x_ref[...] + f, g_ref[...], b_ref[...])


def outproj_ln(y2d, w_bf16, x2d, g, b, tm):
    rows, kdim = y2d.shape
    return pl.pallas_call(
        _outproj_ln_kernel,
        grid=(rows // tm,),
        in_specs=[pl.BlockSpec((tm, kdim), lambda i: (i, 0)),
                  pl.BlockSpec((kdim, D_MODEL), lambda i: (0, 0)),
                  pl.BlockSpec((tm, D_MODEL), lambda i: (i, 0)),
                  pl.BlockSpec((1, D_MODEL), lambda i: (0, 0)),
                  pl.BlockSpec((1, D_MODEL), lambda i: (0, 0))],
        out_specs=pl.BlockSpec((tm, D_MODEL), lambda i: (i, 0)),
        out_shape=jax.ShapeDtypeStruct((rows, D_MODEL), F32),
        compiler_params=_cparams(1),
        name="outproj_ln",
    )(y2d, w_bf16, x2d, g.reshape(1, D_MODEL), b.reshape(1, D_MODEL))


def _rglru_core(ub, gate, a_s, b_s, h_in, gate_refs, *, tt, unroll):
    cw_ref, cb_ref, wr_ref, rb_ref, wi_ref, ib_ref, lam_ref = gate_refs
    cw = cw_ref[...]
    ext = ub[...]
    y = cb_ref[...] + cw[3:4, :] * ext[HIST:, :]
    for k in range(1, CONV_W):
        y = y + cw[3 - k:4 - k, :] * pltpu.roll(ext, k, 0)[HIST:, :]

    sp = _softplus(-lam_ref[...])
    yb = y.astype(BF16)
    for j in range(LRU_W // MXU_DIM):
        sl = slice(j * MXU_DIM, (j + 1) * MXU_DIM)
        ys = yb[:, sl]
        r = _sigmoid(jnp.dot(ys, wr_ref[j], preferred_element_type=F32) + rb_ref[:, sl])
        ig = _sigmoid(jnp.dot(ys, wi_ref[j], preferred_element_type=F32) + ib_ref[:, sl])
        log_a = -LRU_C * r * sp[:, sl]
        a = jnp.exp(log_a)
        a_s[:, sl] = a
        b_s[:, sl] = _sqrt_nonneg(-jnp.tanh(log_a) * (a * a + 1.0)) * (ig * y[:, sl])

    row = lax.broadcasted_iota(jnp.int32, (SUBLANES, LRU_W), 0)

    def group(r0, hc):
        a = a_s[pl.ds(r0, SUBLANES), :]
        b = b_s[pl.ds(r0, SUBLANES), :]
        for d in (1, 2, 4):
            a_sh = pltpu.roll(a, d, 0)
            b_sh = pltpu.roll(b, d, 0)
            m = row >= d
            b = jnp.where(m, a * b_sh + b, b)
            a = jnp.where(m, a * a_sh, a)
        h = a * hc + b
        b_s[pl.ds(r0, SUBLANES), :] = h
        return h[SUBLANES - 1:SUBLANES, :]

    if unroll:
        hc = h_in
        for g in range(tt // SUBLANES):
            hc = group(g * SUBLANES, hc)
    else:
        lax.fori_loop(0, tt // SUBLANES,
                      lambda g, hc: group(pl.multiple_of(g * SUBLANES, SUBLANES), hc), h_in)
    return b_s[...] * _silu(gate[...])


def _rglru_core_kernel(gate_ref, u_ref, buf_ref, h0_ref, *refs, tt, valid):
    gate_refs = refs[:7]
    out_ref, nbuf_ref, hlast_ref, ubuf, g_s, a_s, b_s = refs[7:]
    ubuf[0:HIST - 3, :] = jnp.zeros((HIST - 3, LRU_W), F32)
    ubuf[HIST - 3:HIST, :] = buf_ref[0]
    ubuf[HIST:HIST + valid, :] = u_ref[0]
    ubuf[HIST + valid:HIST + tt, :] = jnp.zeros((tt - valid, LRU_W), F32)
    g_s[0:valid, :] = gate_ref[0]
    g_s[valid:tt, :] = jnp.zeros((tt - valid, LRU_W), F32)
    hg = _rglru_core(ubuf, g_s, a_s, b_s, h0_ref[0], gate_refs, tt=tt, unroll=False)
    out_ref[0] = hg[0:valid, :]
    nbuf_ref[0] = ubuf[HIST + valid - 3:HIST + valid, :]
    hlast_ref[0] = b_s[valid - 1:valid, :]


def _rglru_layer_kernel(xa_ref, xb_ref, xn_ref, buf_ref, h0_ref, win_ref, *refs, tt, n_steps):
    gate_refs = refs[:7]
    wout_ref, lng_ref, lnb_ref, out_ref, nbuf_ref, hlast_ref, ubuf, g_s, a_s, b_s, hcar = refs[7:]
    s = pl.program_id(0)
    first = s % n_steps == 0

    def project(x_ref, slot):
        xb = x_ref[0].astype(BF16)
        g_s[slot] = jnp.dot(xb, win_ref[:, 0:LRU_W], preferred_element_type=F32)
        ubuf[slot, HIST:HIST + tt, :] = jnp.dot(xb, win_ref[:, LRU_W:2 * LRU_W],
                                                preferred_element_type=F32)

    def finish(x_ref, hg):
        f = jnp.dot(hg.astype(BF16), wout_ref[...], preferred_element_type=F32)
        return _layer_norm(DN_ALPHA * x_ref[0] + f, lng_ref[...], lnb_ref[...])

    @pl.when(s == 0)
    def _():
        for slot in range(2):
            ubuf[slot, 0:HIST - 3, :] = jnp.zeros((HIST - 3, LRU_W), F32)
        project(xa_ref, 0)

    @pl.when(first)
    def _():
        ubuf[0, HIST - 3:HIST, :] = buf_ref[0]
        hcar[...] = h0_ref[0]

    @pl.when(jnp.logical_not(first))
    def _():
        ubuf[0, HIST - 3:HIST, :] = ubuf[1, HIST + tt - 3:HIST + tt, :]

    project(xb_ref, 1)
    hg = _rglru_core(ubuf.at[0], g_s.at[0], a_s, b_s, hcar[...], gate_refs, tt=tt, unroll=True)
    h_a = b_s[tt - 1:tt, :]
    out_ref[0, 0:tt, :] = finish(xa_ref, hg)
    ubuf[1, HIST - 3:HIST, :] = ubuf[0, HIST + tt - 3:HIST + tt, :]
    project(xn_ref, 0)
    hg = _rglru_core(ubuf.at[1], g_s.at[1], a_s, b_s, h_a, gate_refs, tt=tt, unroll=True)
    h_b = b_s[tt - 1:tt, :]
    hcar[...] = h_b
    out_ref[0, tt:2 * tt, :] = finish(xb_ref, hg)

    @pl.when(s % n_steps == n_steps - 1)
    def _():
        nbuf_ref[0] = ubuf[1, HIST + tt - 3:HIST + tt, :]
        hlast_ref[0] = h_b


def rglru_layer(x, conv_buf, h0, w, ln_g, ln_b):
    in_w, conv_w, conv_b, wr_bd, r_b, wi_bd, i_b, lam, out_w = w
    bsz, t_len, _ = x.shape
    row = lambda v: v.reshape(1, LRU_W)
    gate_args = (conv_w, row(conv_b), wr_bd, row(r_b), wi_bd, row(i_b), row(lam))
    h0r = h0.reshape(bsz, 1, LRU_W)
    state_shapes = [jax.ShapeDtypeStruct((bsz, CONV_W - 1, LRU_W), F32),
                    jax.ShapeDtypeStruct((bsz, 1, LRU_W), F32)]
    tt = 256
    if t_len % (2 * tt) == 0:
        n_steps = t_len // (2 * tt)
        n_tiles = bsz * t_len // tt
        const = lambda shape: pl.BlockSpec(shape, lambda s: (0,) * len(shape))
        x_tile = lambda off: pl.BlockSpec(
            (1, tt, D_MODEL), lambda s: (jnp.minimum(2 * s + off, n_tiles - 1), 0, 0))
        per_seq = lambda r: pl.BlockSpec((1, r, LRU_W), lambda s: (s // n_steps, 0, 0))
        gate_specs = [const((CONV_W, LRU_W)), const((1, LRU_W)),
                      const((LRU_W // MXU_DIM, MXU_DIM, MXU_DIM)), const((1, LRU_W)),
                      const((LRU_W // MXU_DIM, MXU_DIM, MXU_DIM)), const((1, LRU_W)),
                      const((1, LRU_W))]
        xt = x.reshape(n_tiles, tt, D_MODEL)
        out, nbuf, hlast = pl.pallas_call(
            functools.partial(_rglru_layer_kernel, tt=tt, n_steps=n_steps),
            grid=(bsz * n_steps,),
            in_specs=([x_tile(0), x_tile(1), x_tile(2), per_seq(CONV_W - 1), per_seq(1),
                       const((D_MODEL, 2 * LRU_W))] + gate_specs
                      + [const((LRU_W, D_MODEL)), const((1, D_MODEL)), const((1, D_MODEL))]),
            out_specs=[pl.BlockSpec((1, 2 * tt, D_MODEL), lambda s: (s, 0, 0)),
                       per_seq(CONV_W - 1), per_seq(1)],
            out_shape=[jax.ShapeDtypeStruct((bsz * n_steps, 2 * tt, D_MODEL), F32)] + state_shapes,
            scratch_shapes=[pltpu.VMEM((2, HIST + tt, LRU_W), F32),
                            pltpu.VMEM((2, tt, LRU_W), F32),
                            pltpu.VMEM((tt, LRU_W), F32),
                            pltpu.VMEM((tt, LRU_W), F32),
                            pltpu.VMEM((1, LRU_W), F32)],
            compiler_params=_cparams(1),
            name="rglru_layer",
        )(xt, xt, xt, conv_buf, h0r, in_w, *gate_args, out_w, row(ln_g), row(ln_b))
        return out.reshape(bsz, t_len, D_MODEL), nbuf, hlast.reshape(bsz, LRU_W)

    tt, valid = 16, t_len
    assert t_len <= tt
    x2 = x.reshape(bsz * t_len, D_MODEL)
    gate, u = proj_split(x2, in_w, ((0, LRU_W), (LRU_W, LRU_W)), _row_tile(bsz * t_len))
    vec = lambda: pl.BlockSpec((1, LRU_W), lambda b: (0, 0))
    wspec = lambda: pl.BlockSpec((LRU_W // MXU_DIM, MXU_DIM, MXU_DIM), lambda b: (0, 0, 0))
    per_seq = lambda r: pl.BlockSpec((1, r, LRU_W), lambda b: (b, 0, 0))
    hg, nbuf, hlast = pl.pallas_call(
        functools.partial(_rglru_core_kernel, tt=tt, valid=valid),
        grid=(bsz,),
        in_specs=[per_seq(valid), per_seq(valid), per_seq(CONV_W - 1), per_seq(1),
                  pl.BlockSpec((CONV_W, LRU_W), lambda b: (0, 0)),
                  vec(), wspec(), vec(), wspec(), vec(), vec()],
        out_specs=[per_seq(valid), per_seq(CONV_W - 1), per_seq(1)],
        out_shape=[jax.ShapeDtypeStruct((bsz, t_len, LRU_W), F32)] + state_shapes,
        scratch_shapes=[pltpu.VMEM((HIST + tt, LRU_W), F32),
                        pltpu.VMEM((tt, LRU_W), F32),
                        pltpu.VMEM((tt, LRU_W), F32),
                        pltpu.VMEM((tt, LRU_W), F32)],
        compiler_params=_cparams(1),
        name="rglru_core",
    )(gate.reshape(bsz, t_len, LRU_W), u.reshape(bsz, t_len, LRU_W), conv_buf, h0r, *gate_args)
    out = outproj_ln(hg.reshape(bsz * t_len, LRU_W), out_w, x2, ln_g, ln_b,
                     _row_tile(bsz * t_len)).reshape(bsz, t_len, D_MODEL)
    return out, nbuf, hlast.reshape(bsz, LRU_W)


def _block_diag_tiles(w):
    per = MXU_DIM // LRU_BS
    w4 = w.reshape(LRU_W // MXU_DIM, per, LRU_BS, LRU_BS)
    eye = jnp.eye(per, dtype=w.dtype)
    t = jnp.einsum("jakc,ab->jakbc", w4, eye)
    return t.reshape(LRU_W // MXU_DIM, MXU_DIM, MXU_DIM).astype(BF16)


def _diff_lambda(lp, lam_init):
    s1 = jnp.sum(lp[0:1, :] * lp[1:2, :], axis=-1, keepdims=True)
    s2 = jnp.sum(lp[2:3, :] * lp[3:4, :], axis=-1, keepdims=True)
    return jnp.exp(s1) - jnp.exp(s2) + lam_init


def _subln_gate(o, sw, gate, lam_init):
    ms = jnp.mean(o * o, axis=-1, keepdims=True)
    return (o * lax.rsqrt(ms + EPS) * sw * (1.0 - lam_init)) * _silu(gate)


def _attn_prompt_kernel(lp_ref, q_ref, kt_ref, v_ref, gate_ref, sw_ref, o_ref,
                        kb_s, vx_s, m_s, accl_s, s_s, *, tq, lam_init, side_work):
    qi = pl.program_id(2)
    hd = 2 * ATT_DH
    t_len = v_ref.shape[1]

    @pl.when(qi == 0)
    def _():
        for i in range(t_len // tq):
            kb_s[i] = kt_ref[0, :, i * tq:(i + 1) * tq].astype(BF16)

        def cvt(i, carry):
            r = pl.multiple_of(i * tq, tq)
            vx_s[pl.ds(r, tq), 0:hd] = v_ref[0, pl.ds(r, tq), :].astype(BF16)
            vx_s[pl.ds(r, tq), hd:2 * hd] = jnp.ones((tq, hd), BF16)
            return carry
        lax.fori_loop(0, t_len // tq, cvt, 0)

    lam = _diff_lambda(lp_ref[...], lam_init)
    q = q_ref[0] * (ATT_DH ** -0.5 * LOG2E)
    lane = lax.broadcasted_iota(jnp.int32, (tq, hd), 1)
    q_maps = (jnp.where(lane < ATT_DH, q, 0.0).astype(BF16),
              jnp.where(lane >= ATT_DH, q, 0.0).astype(BF16))
    m_s[...] = jnp.full(m_s.shape, NEG_BIG, F32)
    accl_s[...] = jnp.zeros(accl_s.shape, F32)
    rowi = lax.broadcasted_iota(jnp.int32, (tq, tq), 0)
    coli = lax.broadcasted_iota(jnp.int32, (tq, tq), 1)

    def scores(ki, slot):
        kb = kb_s[ki]
        for c in range(2):
            s_s[slot, c] = jnp.dot(q_maps[c], kb, preferred_element_type=F32)

    def absorb(ki, slot, masked):
        ks = pl.multiple_of(ki * tq, tq)
        vx = vx_s[pl.ds(ks, tq), :]
        for c in range(2):
            s = s_s[slot, c]
            if masked:
                s = jnp.where(coli <= rowi, s, NEG_BIG)
            m_prev = m_s[c]
            m_new = jnp.maximum(m_prev, jnp.max(s, axis=-1, keepdims=True))
            alpha = jnp.exp2(m_prev - m_new)
            p = jnp.exp2(s - jnp.concatenate([m_new] * (tq // hd), axis=1))
            pv = jnp.dot(p.astype(BF16), vx, preferred_element_type=F32)
            accl_s[c] = jnp.concatenate([alpha, alpha], axis=1) * accl_s[c] + pv
            m_s[c] = m_new

    scores(0, 0)

    def pair(j, carry):
        k0 = 2 * j
        scores(k0 + 1, 1)
        absorb(k0, 0, False)
        scores(k0 + 2, 0)
        absorb(k0 + 1, 1, False)
        return carry

    lax.fori_loop(0, qi // 2, pair, 0)

    @pl.when(qi % 2 == 0)
    def _():
        side_work()
        absorb(qi, 0, True)

    @pl.when(qi % 2 == 1)
    def _():
        scores(qi, 1)
        side_work()
        absorb(qi - 1, 0, False)
        absorb(qi, 1, True)

    a1 = accl_s[0]
    a2 = accl_s[1]
    o = a1[:, 0:hd] / a1[:, hd:2 * hd] - lam * (a2[:, 0:hd] / a2[:, hd:2 * hd])
    o_ref[0] = _subln_gate(o, sw_ref[...], gate_ref[0], lam_init)


def _attn_sample_parts(lp_ref, qbd_ref, *refs, n_pg, t_new, lam_init):
    k_refs = refs[:n_pg]
    v_refs = refs[n_pg:2 * n_pg]
    knew_ref, vnew_ref, gate_ref, sw_ref, o_ref, m_s, l_s, acc_s = refs[2 * n_pg:]
    j = pl.program_id(2)
    n_j = pl.num_programs(2)
    n_rows = ATT_HEADS * 2 * t_new
    hd = 2 * ATT_DH

    def start():
        @pl.when(j == 0)
        def _():
            m_s[...] = jnp.full(m_s.shape, NEG_BIG, F32)
            l_s[...] = jnp.zeros(l_s.shape, F32)
            acc_s[...] = jnp.zeros(acc_s.shape, F32)

    def attend(kts, vbs, mask):
        qbd = qbd_ref[0]
        n = len(kts)
        s = jnp.concatenate([jnp.dot(qbd, kt.astype(BF16), preferred_element_type=F32)
                             for kt in kts], axis=1)
        if mask is not None:
            s = jnp.where(mask, s, NEG_BIG)
        m_prev = m_s[...]
        m_new = jnp.maximum(m_prev, jnp.max(s, axis=-1, keepdims=True))
        alpha = jnp.exp2(m_prev - m_new)
        p = jnp.exp2(s - jnp.concatenate([m_new] * n, axis=1))
        l_s[...] = alpha * l_s[...] + jnp.sum(p, axis=-1, keepdims=True)
        pb = p.astype(BF16)
        pv = jnp.dot(pb[:, 0:PAGE_SIZE], vbs[0].astype(BF16), preferred_element_type=F32)
        for i in range(1, n):
            pv = pv + jnp.dot(pb[:, i * PAGE_SIZE:(i + 1) * PAGE_SIZE], vbs[i].astype(BF16),
                              preferred_element_type=F32)
        acc_s[...] = jnp.concatenate([alpha] * (ATT_W // LANES), axis=1) * acc_s[...] + pv
        m_s[...] = m_new

    def page_v(vr):
        return jnp.concatenate([vr[0, pl.ds(h, PAGE_SIZE, stride=ATT_HEADS), :]
                                for h in range(ATT_HEADS)], axis=1)

    def pages():
        attend([r[0] for r in k_refs], [page_v(r) for r in v_refs], None)

    def finish():
        @pl.when(j == n_j - 1)
        def _():
            rowi = lax.broadcasted_iota(jnp.int32, (n_rows, PAGE_SIZE), 0)
            coli = lax.broadcasted_iota(jnp.int32, (n_rows, PAGE_SIZE), 1)
            attend([knew_ref[0]], [vnew_ref[0]], coli <= (rowi % t_new))
            lam = _diff_lambda(lp_ref[...], lam_init)
            sw = sw_ref[...]
            for h in range(ATT_HEADS):
                r1 = h * 2 * t_new
                r2 = r1 + t_new
                cs = slice(h * hd, (h + 1) * hd)
                o1 = acc_s[r1:r1 + t_new, cs] / l_s[r1:r1 + t_new, :]
                o2 = acc_s[r2:r2 + t_new, cs] / l_s[r2:r2 + t_new, :]
                o_ref[0, :, cs] = _subln_gate(o1 - lam * o2, sw, gate_ref[0, :, cs], lam_init)

    return start, pages, finish


def _attn_kernel(pt_ref, lp_ref, q_ref, kt_ref, v_ref, gate_ref, sw_ref, qbd_ref, *refs,
                 n_pg, tq, t_new, lam_init):
    del pt_ref
    sample_in = refs[:2 * n_pg + 3]
    o_ref, os_ref, kb_s, vx_s, m_s, accl_s, s_s, ms_s, ls_s, accs_s = refs[2 * n_pg + 3:]
    start, pages, finish = _attn_sample_parts(
        lp_ref, qbd_ref, *sample_in, sw_ref, os_ref, ms_s, ls_s, accs_s,
        n_pg=n_pg, t_new=t_new, lam_init=lam_init)
    start()
    _attn_prompt_kernel(lp_ref, q_ref, kt_ref, v_ref, gate_ref, sw_ref, o_ref,
                        kb_s, vx_s, m_s, accl_s, s_s, tq=tq, lam_init=lam_init, side_work=pages)
    finish()


def attn_core(q, kt, v, gate, qs, ks, vs, gate_s, cache_k, cache_v, layer, page_table, lam_p,
              subln_w, lam_init, tq=512):
    bsz_p, t_len, _ = q.shape
    bsz, t_new, _ = qs.shape
    n_pages = page_table.shape[1]
    n_pool = cache_k.shape[1]
    hd = 2 * ATT_DH
    n_rows = ATT_HEADS * 2 * t_new
    n_q = t_len // tq
    assert bsz == bsz_p * ATT_HEADS and n_pages % n_q == 0
    n_pg = n_pages // n_q
    q4 = (qs * (ATT_DH ** -0.5 * LOG2E)).reshape(bsz, t_new, 2 * ATT_HEADS, ATT_DH)
    eye = jnp.eye(2 * ATT_HEADS, dtype=F32)
    qbd = jnp.einsum("bqhd,hg->bhqgd", q4, eye).reshape(bsz, n_rows, ATT_W).astype(BF16)
    k_new = jnp.pad(jnp.swapaxes(ks, 1, 2), ((0, 0), (0, 0), (0, PAGE_SIZE - t_new)))
    v_new = jnp.pad(vs, ((0, 0), (0, PAGE_SIZE - t_new), (0, 0)))
    ck = jnp.transpose(cache_k, (0, 1, 3, 4, 5, 2)).reshape(-1, ATT_W, PAGE_SIZE)
    cv = cache_v.reshape(-1, PAGE_SIZE * ATT_HEADS, hd)
    pt = page_table.reshape(-1) + layer * n_pool

    sample = lambda b, h: b * ATT_HEADS + h

    def page_spec(i, rows, width):
        return pl.BlockSpec(
            (1, rows, width),
            lambda b, h, j, pt_ref: (pt_ref[sample(b, h) * n_pages + j * n_pg + i], 0, 0))

    const = lambda shape: pl.BlockSpec(shape, lambda b, h, j, pt_ref: (0,) * len(shape))
    qspec = lambda: pl.BlockSpec((1, tq, hd), lambda b, h, j, pt_ref: (b, j, h))
    per_s = lambda r, w: pl.BlockSpec((1, r, w), lambda b, h, j, pt_ref: (sample(b, h), 0, 0))
    grid_spec = pltpu.PrefetchScalarGridSpec(
        num_scalar_prefetch=1,
        grid=(bsz_p, ATT_HEADS, n_q),
        in_specs=([const((4, ATT_DH)), qspec(),
                   pl.BlockSpec((1, hd, t_len), lambda b, h, j, pt_ref: (b, h, 0)),
                   pl.BlockSpec((1, t_len, hd), lambda b, h, j, pt_ref: (b, 0, h)),
                   qspec(), const((1, hd)), per_s(n_rows, ATT_W)]
                  + [page_spec(i, ATT_W, PAGE_SIZE) for i in range(n_pg)]
                  + [page_spec(i, PAGE_SIZE * ATT_HEADS, hd) for i in range(n_pg)]
                  + [per_s(ATT_W, PAGE_SIZE), per_s(PAGE_SIZE, ATT_W), per_s(t_new, ATT_W)]),
        out_specs=[qspec(), per_s(t_new, ATT_W)],
        scratch_shapes=[pltpu.VMEM((n_q, hd, tq), BF16),
                        pltpu.VMEM((t_len, 2 * hd), BF16),
                        pltpu.VMEM((2, tq, hd), F32), pltpu.VMEM((2, tq, 2 * hd), F32),
                        pltpu.VMEM((2, 2, tq, tq), F32),
                        pltpu.VMEM((n_rows, LANES), F32), pltpu.VMEM((n_rows, LANES), F32),
                        pltpu.VMEM((n_rows, ATT_W), F32)],
    )
    return pl.pallas_call(
        functools.partial(_attn_kernel, n_pg=n_pg, tq=tq, t_new=t_new, lam_init=lam_init),
        grid_spec=grid_spec,
        out_shape=[jax.ShapeDtypeStruct((bsz_p, t_len, ATT_W), F32),
                   jax.ShapeDtypeStruct((bsz, t_new, ATT_W), F32)],
        compiler_params=_cparams(3),
        name="attn_core",
    )(pt, lam_p, q, kt, v, gate, subln_w.reshape(1, hd), qbd, *([ck] * n_pg), *([cv] * n_pg),
      k_new, v_new, gate_s)


def _ssd_core(ub, z, dt_raw, hout_ref, y_s, scan_refs, *, vl, after_group=None):
    cw_ref, cb_ref, dtb_ref, alog_ref, dexp_ref, nw_ref = scan_refs
    L = SSD_CHUNK
    gw = SSD_HPG * SSD_HEADDIM
    rowL = lax.broadcasted_iota(jnp.int32, (L, LANES), 0)
    laneL = lax.broadcasted_iota(jnp.int32, (L, LANES), 1)
    dt = _softplus(dt_raw + dtb_ref[...])
    dt = jnp.where((rowL < vl) & (laneL < SSD_HEADS), dt, 0.0)
    adt = dt * (-jnp.exp(alog_ref[...]))
    tri = (lax.broadcasted_iota(jnp.int32, (L, L), 0)
           >= lax.broadcasted_iota(jnp.int32, (L, L), 1))
    cs = jnp.dot(tri.astype(F32), adt, preferred_element_type=F32,
                 precision=lax.Precision.HIGHEST)
    cs_last = cs[L - 1:L, :]
    w1 = dt * jnp.exp(cs_last - cs)
    cs_t = cs.T
    dt_t = dt.T
    w1_t = w1.T

    cw = cw_ref[...]
    lane_g = lax.broadcasted_iota(jnp.int32, (L, gw), 1)

    def conv_silu(lo, width):
        sl = slice(lo, lo + width)
        y = cb_ref[:, sl] + cw[3:4, sl] * ub[HIST:HIST + L, sl]
        for k in range(1, CONV_W):
            y = y + cw[3 - k:4 - k, sl] * ub[HIST - k:HIST - k + L, sl]
        return _silu(y)

    for g in range(SSD_GROUPS):
        xh = conv_silu(g * gw, gw)
        bm = conv_silu(SSD_INNER + g * SSD_STATE, SSD_STATE)
        cm = conv_silu(SSD_INNER + (SSD_GROUPS + g) * SSD_STATE, SSD_STATE)
        bmb = bm.astype(BF16)
        cmb = cm.astype(BF16)
        xhb = xh.astype(BF16)
        cbm = _nt(cmb, bmb)
        y_diag = jnp.zeros((L, gw), F32)
        e_cols = []
        w_rows = []
        d_rows = []
        for r in range(SSD_HPG):
            h = g * SSD_HPG + r
            cs_col = cs[:, h:h + 1]
            cs_row = cs_t[h:h + 1, :]
            lm = jnp.exp(jnp.where(tri, cs_col - cs_row, NEG_BIG))
            mat = (cbm * lm * dt_t[h:h + 1, :]).astype(BF16)
            in_head = (lane_g >= r * SSD_HEADDIM) & (lane_g < (r + 1) * SSD_HEADDIM)
            xr = jnp.where(in_head, xhb, jnp.zeros_like(xhb))
            y_diag = y_diag + jnp.dot(mat, xr, preferred_element_type=F32)
            e_cols.append(jnp.broadcast_to(jnp.exp(cs_col), (L, SSD_HEADDIM)))
            w_rows.append(jnp.broadcast_to(w1_t[h:h + 1, :], (SSD_HEADDIM, L)))
            d_rows.append(jnp.broadcast_to(jnp.exp(cs_t[h:h + 1, L - 1:L]),
                                           (SSD_HEADDIM, SSD_STATE)))
        hg = hout_ref[0, g]
        y_off = _nt(cmb, hg.astype(BF16)) * jnp.concatenate(e_cols, axis=1)
        y_s[:, g * gw:(g + 1) * gw] = y_diag + y_off + dexp_ref[:, g * gw:(g + 1) * gw] * xh
        xd_t = (xh.T * jnp.concatenate(w_rows, axis=0)).astype(BF16)
        states = jnp.dot(xd_t, bmb, preferred_element_type=F32)
        hout_ref[0, g] = hg * jnp.concatenate(d_rows, axis=0) + states
        if after_group is not None:
            after_group(g)

    gated = y_s[...] * _silu(z[...])
    ms = jnp.mean(gated * gated, axis=-1, keepdims=True)
    return gated * lax.rsqrt(ms + EPS) * nw_ref[...]


def _ssd_core_kernel(xbc_ref, z_ref, dtr_ref, buf_ref, h0_ref, *refs, vl):
    scan_refs = refs[:6]
    out_ref, nbuf_ref, hout_ref, ubuf, z_s, dt_s, y_s = refs[6:]
    L = SSD_CHUNK
    ubuf[HIST - 3:HIST, :] = buf_ref[0]
    ubuf[HIST:HIST + vl, :] = xbc_ref[0]
    ubuf[HIST + vl:HIST + L, :] = jnp.zeros((L - vl, SSD_CONV_DIM), F32)
    z_s[0:vl, :] = z_ref[0]
    z_s[vl:L, :] = jnp.zeros((L - vl, SSD_INNER), F32)
    dt_s[0:vl, :] = dtr_ref[0]
    dt_s[vl:L, :] = jnp.zeros((L - vl, LANES), F32)
    hout_ref[0] = h0_ref[0]
    yn = _ssd_core(ubuf, z_s, dt_s[...], hout_ref, y_s, scan_refs, vl=vl)
    out_ref[0] = yn[0:vl, :]
    nbuf_ref[0] = ubuf[HIST + vl - 3:HIST + vl, :]


def _ssd_layer_kernel(xa_ref, xb_ref, xn_ref, buf_ref, h0_ref, win_ref, *refs, n_steps):
    scan_refs = refs[:6]
    (wout_ref, lng_ref, lnb_ref, out_ref, nbuf_ref, hout_ref, ubuf, z_s, dt_s, y_s) = refs[6:]
    L = SSD_CHUNK
    s = pl.program_id(0)
    first = s % n_steps == 0
    xbc_lo = SSD_INNER
    dt_lo = SSD_INNER + SSD_CONV_DIM

    n_sl = SSD_GROUPS
    zw = SSD_INNER // n_sl
    cwid = SSD_CONV_DIM // n_sl

    def project_slice(xb, slot, i):
        z_s[slot, :, i * zw:(i + 1) * zw] = jnp.dot(
            xb, win_ref[:, i * zw:(i + 1) * zw], preferred_element_type=F32)
        ubuf[slot, HIST:HIST + L, i * cwid:(i + 1) * cwid] = jnp.dot(
            xb, win_ref[:, xbc_lo + i * cwid:xbc_lo + (i + 1) * cwid],
            preferred_element_type=F32)
        if i == 0:
            dt_s[slot] = jnp.dot(xb, win_ref[:, dt_lo:dt_lo + LANES],
                                 preferred_element_type=F32)

    def project(x_ref, slot):
        xb = x_ref[0].astype(BF16)
        for i in range(n_sl):
            project_slice(xb, slot, i)

    def finish(x_ref, yn):
        f = jnp.dot(yn.astype(BF16), wout_ref[...], preferred_element_type=F32)
        return _layer_norm(DN_ALPHA * x_ref[0] + f, lng_ref[...], lnb_ref[...])

    @pl.when(s == 0)
    def _():
        project(xa_ref, 0)

    @pl.when(first)
    def _():
        ubuf[0, HIST - 3:HIST, :] = buf_ref[0]
        hout_ref[0] = h0_ref[0]

    @pl.when(jnp.logical_not(first))
    def _():
        ubuf[0, HIST - 3:HIST, :] = ubuf[1, HIST + L - 3:HIST + L, :]

    xb_b = xb_ref[0].astype(BF16)
    yn = _ssd_core(ubuf.at[0], z_s.at[0], dt_s[0], hout_ref, y_s, scan_refs, vl=L,
                   after_group=lambda g: project_slice(xb_b, 1, g))
    out_ref[0, 0:L, :] = finish(xa_ref, yn)
    ubuf[1, HIST - 3:HIST, :] = ubuf[0, HIST + L - 3:HIST + L, :]
    xb_n = xn_ref[0].astype(BF16)
    yn = _ssd_core(ubuf.at[1], z_s.at[1], dt_s[1], hout_ref, y_s, scan_refs, vl=L,
                   after_group=lambda g: project_slice(xb_n, 0, g))
    out_ref[0, L:2 * L, :] = finish(xb_ref, yn)

    @pl.when(s % n_steps == n_steps - 1)
    def _():
        nbuf_ref[0] = ubuf[1, HIST + L - 3:HIST + L, :]


def ssd_layer(x, conv_buf, h0, w, ln_g, ln_b):
    in_w_pad, conv_w, conv_b, dt_bias, a_log, d_skip, norm_w, out_w = w
    bsz, t_len, _ = x.shape
    L = SSD_CHUNK
    gw = SSD_HPG * SSD_HEADDIM
    n_proj = in_w_pad.shape[1]
    pad_lanes = lambda v: jnp.pad(v.reshape(1, SSD_HEADS), ((0, 0), (0, LANES - SSD_HEADS)))
    dexp = jnp.repeat(d_skip, SSD_HEADDIM).reshape(1, SSD_INNER)
    h0g = h0.reshape(bsz, SSD_GROUPS, gw, SSD_STATE)
    scan_args = (conv_w, conv_b.reshape(1, SSD_CONV_DIM), pad_lanes(dt_bias), pad_lanes(a_log),
                 dexp, norm_w.reshape(1, SSD_INNER))
    scan_shapes = [(CONV_W, SSD_CONV_DIM), (1, SSD_CONV_DIM), (1, LANES), (1, LANES),
                   (1, SSD_INNER), (1, SSD_INNER)]
    state_shapes = [jax.ShapeDtypeStruct((bsz, CONV_W - 1, SSD_CONV_DIM), F32),
                    jax.ShapeDtypeStruct((bsz, SSD_GROUPS, gw, SSD_STATE), F32)]
    if t_len % (2 * L) == 0:
        n_steps = t_len // (2 * L)
        n_chunks = bsz * t_len // L
        const = lambda shape: pl.BlockSpec(shape, lambda s: (0,) * len(shape))
        resident = lambda shape: pl.BlockSpec(shape, lambda s: (0,) * len(shape),
                                              pipeline_mode=pl.Buffered(1))
        x_chunk = lambda off: pl.BlockSpec(
            (1, L, D_MODEL), lambda s: (jnp.minimum(2 * s + off, n_chunks - 1), 0, 0))
        state_specs = [pl.BlockSpec((1, CONV_W - 1, SSD_CONV_DIM), lambda s: (s // n_steps, 0, 0)),
                       pl.BlockSpec((1, SSD_GROUPS, gw, SSD_STATE),
                                    lambda s: (s // n_steps, 0, 0, 0))]
        xt = x.reshape(n_chunks, L, D_MODEL)
        out, nbuf, hout = pl.pallas_call(
            functools.partial(_ssd_layer_kernel, n_steps=n_steps),
            grid=(bsz * n_steps,),
            in_specs=([x_chunk(0), x_chunk(1), x_chunk(2)] + state_specs
                      + [resident((D_MODEL, n_proj))] + [const(sh) for sh in scan_shapes]
                      + [resident((SSD_INNER, D_MODEL)), const((1, D_MODEL)), const((1, D_MODEL))]),
            out_specs=[pl.BlockSpec((1, 2 * L, D_MODEL), lambda s: (s, 0, 0))] + state_specs,
            out_shape=[jax.ShapeDtypeStruct((bsz * n_steps, 2 * L, D_MODEL), F32)] + state_shapes,
            scratch_shapes=[pltpu.VMEM((2, HIST + L, SSD_CONV_DIM), F32),
                            pltpu.VMEM((2, L, SSD_INNER), F32),
                            pltpu.VMEM((2, L, LANES), F32),
                            pltpu.VMEM((L, SSD_INNER), F32)],
            compiler_params=_cparams(1),
            name="ssd_layer",
        )(xt, xt, xt, conv_buf, h0g, in_w_pad, *scan_args, out_w,
          ln_g.reshape(1, D_MODEL), ln_b.reshape(1, D_MODEL))
        return (out.reshape(bsz, t_len, D_MODEL), nbuf,
                hout.reshape(bsz, SSD_HEADS, SSD_HEADDIM, SSD_STATE))

    vl = t_len
    assert vl <= L
    x2 = x.reshape(bsz * t_len, D_MODEL)
    splits = ((0, SSD_INNER), (SSD_INNER, SSD_CONV_DIM), (SSD_INNER + SSD_CONV_DIM, LANES))
    z, xbc, dt_raw = proj_split(x2, in_w_pad, splits, _row_tile(bsz * t_len))
    const = lambda shape: pl.BlockSpec(shape, lambda b: (0,) * len(shape))
    rows = lambda width: pl.BlockSpec((1, vl, width), lambda b: (b, 0, 0))
    state_specs = [pl.BlockSpec((1, CONV_W - 1, SSD_CONV_DIM), lambda b: (b, 0, 0)),
                   pl.BlockSpec((1, SSD_GROUPS, gw, SSD_STATE), lambda b: (b, 0, 0, 0))]
    yn, nbuf, hout = pl.pallas_call(
        functools.partial(_ssd_core_kernel, vl=vl),
        grid=(bsz,),
        in_specs=([rows(SSD_CONV_DIM), rows(SSD_INNER), rows(LANES)] + state_specs
                  + [const(sh) for sh in scan_shapes]),
        out_specs=[rows(SSD_INNER)] + state_specs,
        out_shape=[jax.ShapeDtypeStruct((bsz, t_len, SSD_INNER), F32)] + state_shapes,
        scratch_shapes=[pltpu.VMEM((HIST + L, SSD_CONV_DIM), F32),
                        pltpu.VMEM((L, SSD_INNER), F32),
                        pltpu.VMEM((L, LANES), F32),
                        pltpu.VMEM((L, SSD_INNER), F32)],
        compiler_params=_cparams(1),
        name="ssd_core",
    )(xbc.reshape(bsz, t_len, SSD_CONV_DIM), z.reshape(bsz, t_len, SSD_INNER),
      dt_raw.reshape(bsz, t_len, LANES), conv_buf, h0g, *scan_args)
    out = outproj_ln(yn.reshape(bsz * t_len, SSD_INNER), out_w, x2, ln_g, ln_b,
                     _row_tile(bsz * t_len)).reshape(bsz, t_len, D_MODEL)
    return out, nbuf, hout.reshape(bsz, SSD_HEADS, SSD_HEADDIM, SSD_STATE)


def _row_tile(rows):
    return 256 if rows % 256 == 0 else rows


def _attn_proj_t_kernel(x_ref, w_ref, wkt_ref, q_ref, kt_ref, v_ref, g_ref):
    x = x_ref[0].astype(BF16)
    q_ref[0] = jnp.dot(x, w_ref[:, 0:ATT_W], preferred_element_type=F32)
    kt_ref[0] = _nt(wkt_ref[...], x)
    v_ref[0] = jnp.dot(x, w_ref[:, 2 * ATT_W:3 * ATT_W], preferred_element_type=F32)
    g_ref[0] = jnp.dot(x, w_ref[:, 3 * ATT_W:4 * ATT_W], preferred_element_type=F32)


def attn_project_t(x, in_w, wkt, tm=256):
    bsz, t_len, _ = x.shape
    rows = lambda: pl.BlockSpec((1, tm, ATT_W), lambda b, i: (b, i, 0))
    return pl.pallas_call(
        _attn_proj_t_kernel,
        grid=(bsz, t_len // tm),
        in_specs=[pl.BlockSpec((1, tm, D_MODEL), lambda b, i: (b, i, 0)),
                  pl.BlockSpec((D_MODEL, 4 * ATT_W), lambda b, i: (0, 0)),
                  pl.BlockSpec((ATT_W, D_MODEL), lambda b, i: (0, 0))],
        out_specs=[rows(), pl.BlockSpec((1, ATT_W, tm), lambda b, i: (b, 0, i)), rows(), rows()],
        out_shape=[jax.ShapeDtypeStruct((bsz, t_len, ATT_W), F32),
                   jax.ShapeDtypeStruct((bsz, ATT_W, t_len), F32),
                   jax.ShapeDtypeStruct((bsz, t_len, ATT_W), F32),
                   jax.ShapeDtypeStruct((bsz, t_len, ATT_W), F32)],
        compiler_params=_cparams(2),
        name="attn_project_t",
    )(x, in_w, wkt)


def _attn_project(x, in_w):
    bsz, t_len, _ = x.shape
    x2 = x.reshape(bsz * t_len, D_MODEL)
    tm = _row_tile(bsz * t_len)
    splits = tuple((i * ATT_W, ATT_W) for i in range(4))
    q, k, v, gate = proj_split(x2, in_w, splits, tm)
    shp = (bsz, t_len, ATT_W)
    return x2, tm, q.reshape(shp), k.reshape(shp), v.reshape(shp), gate.reshape(shp)


def kernel(x_prompt, x_sample, cache_k, cache_v, page_table, state_lru_conv, state_lru_h, state_ssd_conv, state_ssd_h, ln_g, ln_b, a_in_w, a_conv_w, a_conv_b, a_gate_r_w, a_gate_r_b, a_gate_i_w, a_gate_i_b, a_lambda, a_out_w, b_in_w, b_lambda, b_subln_w, b_out_w, c_in_w, c_conv_w, c_conv_b, c_dt_bias, c_a_log, c_d, c_norm_w, c_out_w):
    xp, xs = x_prompt, x_sample
    bp, bs = xp.shape[0], xs.shape[0]
    tp, ts = xp.shape[1], xs.shape[1]
    k_p, v_p, k_s, v_s = [], [], [], []
    lc_p, lh_p, lc_s, lh_s = [], [], [], []
    sc_p, sh_p, sc_s, sh_s = [], [], [], []
    for i in range(DEPTH):
        j = i // N_MIXERS
        kind = i % N_MIXERS
        if kind == 0:
            w = (a_in_w[j].astype(BF16), a_conv_w[j], a_conv_b[j], _block_diag_tiles(a_gate_r_w[j]),
                 a_gate_r_b[j], _block_diag_tiles(a_gate_i_w[j]), a_gate_i_b[j], a_lambda[j],
                 a_out_w[j].astype(BF16))
            zc = jnp.zeros((bp, CONV_W - 1, LRU_W), F32)
            zh = jnp.zeros((bp, LRU_W), F32)
            xp, c1, h1 = rglru_layer(xp, zc, zh, w, ln_g[i], ln_b[i])
            xs, c2, h2 = rglru_layer(xs, state_lru_conv[j], state_lru_h[j], w, ln_g[i], ln_b[i])
            lc_p.append(c1); lh_p.append(h1); lc_s.append(c2); lh_s.append(h2)
        elif kind == 1:
            lam_init = 0.8 - 0.6 * math.exp(-0.3 * i)
            in_w = b_in_w[j].astype(BF16)
            out_w = b_out_w[j].astype(BF16)
            wkt = b_in_w[j][:, ATT_W:2 * ATT_W].T.astype(BF16)
            q, kt, v, g = attn_project_t(xp, in_w, wkt)
            x2s, tms, qs, ks, vs, gs = _attn_project(xs, in_w)
            og, ogs = attn_core(q, kt, v, g, qs, ks, vs, gs, cache_k, cache_v, j, page_table,
                                b_lambda[j], b_subln_w[j], lam_init)
            xp = outproj_ln(og.reshape(bp * tp, ATT_W), out_w, xp.reshape(bp * tp, D_MODEL),
                            ln_g[i], ln_b[i], _row_tile(bp * tp)).reshape(bp, tp, D_MODEL)
            xs = outproj_ln(ogs.reshape(bs * ts, ATT_W), out_w, x2s, ln_g[i], ln_b[i], tms
                            ).reshape(bs, ts, D_MODEL)
            k_p.append(jnp.transpose(kt.reshape(bp, ATT_HEADS, 2, ATT_DH, tp), (0, 4, 1, 2, 3)))
            v_p.append(v.reshape(bp, tp, ATT_HEADS, 2 * ATT_DH))
            k_s.append(ks.reshape(bs, ts, ATT_HEADS, 2, ATT_DH))
            v_s.append(vs.reshape(bs, ts, ATT_HEADS, 2 * ATT_DH))
        else:
            in_w_pad = jnp.pad(c_in_w[j], ((0, 0), (0, LANES - SSD_HEADS))).astype(BF16)
            w = (in_w_pad, c_conv_w[j], c_conv_b[j], c_dt_bias[j], c_a_log[j], c_d[j], c_norm_w[j],
                 c_out_w[j].astype(BF16))
            zc = jnp.zeros((bp, CONV_W - 1, SSD_CONV_DIM), F32)
            zh = jnp.zeros((bp, SSD_HEADS, SSD_HEADDIM, SSD_STATE), F32)
            xp, c1, h1 = ssd_layer(xp, zc, zh, w, ln_g[i], ln_b[i])
            xs, c2, h2 = ssd_layer(xs, state_ssd_conv[j], state_ssd_h[j], w, ln_g[i], ln_b[i])
            sc_p.append(c1); sh_p.append(h1); sc_s.append(c2); sh_s.append(h2)
    return (xp, xs, jnp.stack(k_p), jnp.stack(v_p), jnp.stack(k_s), jnp.stack(v_s),
            jnp.stack(lc_p), jnp.stack(lh_p), jnp.stack(lc_s), jnp.stack(lh_s),
            jnp.stack(sc_p), jnp.stack(sh_p), jnp.stack(sc_s), jnp.stack(sh_s))
```

```python
import functools
import math

import jax
import jax.numpy as jnp
from jax import lax
from jax.experimental import pallas as pl
from jax.experimental.pallas import tpu as pltpu

F32 = jnp.float32
BF16 = jnp.bfloat16

D_MODEL = 1024
DEPTH = 4
PAGE_SIZE = 128
N_MIXERS = 3
DN_ALPHA = (2.0 * DEPTH) ** 0.25
EPS = 1e-5
CONV_W = 4
LRU_W = D_MODEL
LRU_BLOCKS = 16
LRU_BS = LRU_W // LRU_BLOCKS
LRU_C = 8.0
ATT_HEADS = 8
ATT_DH = D_MODEL // (2 * ATT_HEADS)
ATT_W = ATT_HEADS * 2 * ATT_DH
SSD_INNER = 2 * D_MODEL
SSD_HEADDIM = 64
SSD_HEADS = SSD_INNER // SSD_HEADDIM
SSD_GROUPS = 8
SSD_HPG = SSD_HEADS // SSD_GROUPS
SSD_STATE = 128
SSD_CONV_DIM = SSD_INNER + 2 * SSD_GROUPS * SSD_STATE
SSD_CHUNK = 128

LANES = 128
SUBLANES = 8
MXU_DIM = 256
VMEM_LIMIT = 56 * 1024 * 1024
LOG2E = math.log2(math.e)
F32_TINY = 1.1754944e-38
NEG_BIG = -1e30
HIST = SUBLANES


def _cparams(n_grid):
    return pltpu.CompilerParams(dimension_semantics=("arbitrary",) * n_grid,
                                vmem_limit_bytes=VMEM_LIMIT)


def _nt(a, b):
    return lax.dot_general(a, b, (((1,), (1,)), ((), ())), preferred_element_type=F32)


def _sigmoid(x):
    return 0.5 * jnp.tanh(0.5 * x) + 0.5


def _sqrt_nonneg(x):
    return x * lax.rsqrt(jnp.maximum(x, F32_TINY))


def _layer_norm(v, g, b):
    mu = jnp.mean(v, axis=-1, keepdims=True)
    d = v - mu
    var = jnp.mean(d * d, axis=-1, keepdims=True)
    return d * lax.rsqrt(var + EPS) * g + b


def _silu(x):
    return x * _sigmoid(x)


def _softplus(x):
    return jnp.maximum(x, 0.0) + jnp.log(1.0 + jnp.exp(-jnp.abs(x)))


def _proj_kernel(x_ref, w_ref, *out_refs, splits):
    x = x_ref[...].astype(BF16)
    for (off, width), o_ref in zip(splits, out_refs):
        o_ref[...] = jnp.dot(x, w_ref[:, off:off + width], preferred_element_type=F32)


def proj_split(x2d, w_bf16, splits, tm):
    rows, kdim = x2d.shape
    ndim = w_bf16.shape[1]
    return pl.pallas_call(
        functools.partial(_proj_kernel, splits=splits),
        grid=(rows // tm,),
        in_specs=[pl.BlockSpec((tm, kdim), lambda i: (i, 0)),
                  pl.BlockSpec((kdim, ndim), lambda i: (0, 0))],
        out_specs=[pl.BlockSpec((tm, wd), lambda i: (i, 0)) for _, wd in splits],
        out_shape=[jax.ShapeDtypeStruct((rows, wd), F32) for _, wd in splits],
        compiler_params=_cparams(1),
        name="proj_split",
    )(x2d, w_bf16)


def _outproj_ln_kernel(y_ref, w_ref, x_ref, g_ref, b_ref, o_ref):
    f = jnp.dot(y_ref[...].astype(BF16), w_ref[...], preferred_element_type=F32)
    o_ref[...] = _layer_norm(DN_ALPHA * x_ref[...] + f, g_ref[...], b_ref[...])


def outproj_ln(y2d, w_bf16, x2d, g, b, tm):
    rows, kdim = y2d.shape
    return pl.pallas_call(
        _outproj_ln_kernel,
        grid=(rows // tm,),
        in_specs=[pl.BlockSpec((tm, kdim), lambda i: (i, 0)),
                  pl.BlockSpec((kdim, D_MODEL), lambda i: (0, 0)),
                  pl.BlockSpec((tm, D_MODEL), lambda i: (i, 0)),
                  pl.BlockSpec((1, D_MODEL), lambda i: (0, 0)),
                  pl.BlockSpec((1, D_MODEL), lambda i: (0, 0))],
        out_specs=pl.BlockSpec((tm, D_MODEL), lambda i: (i, 0)),
        out_shape=jax.ShapeDtypeStruct((rows, D_MODEL), F32),
        compiler_params=_cparams(1),
        name="outproj_ln",
    )(y2d, w_bf16, x2d, g.reshape(1, D_MODEL), b.reshape(1, D_MODEL))


def _rglru_core(ub, gate, a_s, b_s, h_in, gate_refs, *, tt, unroll):
    cw_ref, cb_ref, wr_ref, rb_ref, wi_ref, ib_ref, lam_ref = gate_refs
    cw = cw_ref[...]
    ext = ub[...]
    y = cb_ref[...] + cw[3:4, :] * ext[HIST:, :]
    for k in range(1, CONV_W):
        y = y + cw[3 - k:4 - k, :] * pltpu.roll(ext, k, 0)[HIST:, :]

    c = (-0.5 * LRU_C) * _softplus(-lam_ref[...])
    yb = y.astype(BF16)
    for j in range(LRU_W // MXU_DIM):
        sl = slice(j * MXU_DIM, (j + 1) * MXU_DIM)
        ys = yb[:, sl]
        t_r = jnp.tanh(jnp.dot(ys, wr_ref[j], preferred_element_type=F32) + rb_ref[:, sl])
        t_i = jnp.tanh(jnp.dot(ys, wi_ref[j], preferred_element_type=F32) + ib_ref[:, sl])
        log_a = c[:, sl] * t_r + c[:, sl]
        a = jnp.exp(log_a)
        a_s[:, sl] = a
        half_y = 0.5 * y[:, sl]
        b_s[:, sl] = (_sqrt_nonneg(-jnp.tanh(log_a) * (a * a + 1.0))
                      * (half_y * t_i + half_y))

    row = lax.broadcasted_iota(jnp.int32, (SUBLANES, LRU_W), 0)

    def group(r0, hc):
        a = a_s[pl.ds(r0, SUBLANES), :]
        b = b_s[pl.ds(r0, SUBLANES), :]
        for d in (1, 2, 4):
            a_sh = pltpu.roll(a, d, 0)
            b_sh = pltpu.roll(b, d, 0)
            m = row >= d
            b = jnp.where(m, a * b_sh + b, b)
            a = jnp.where(m, a * a_sh, a)
        h = a * hc + b
        b_s[pl.ds(r0, SUBLANES), :] = h
        return h[SUBLANES - 1:SUBLANES, :]

    if unroll:
        hc = h_in
        for g in range(tt // SUBLANES):
            hc = group(g * SUBLANES, hc)
    else:
        lax.fori_loop(0, tt // SUBLANES,
                      lambda g, hc: group(pl.multiple_of(g * SUBLANES, SUBLANES), hc), h_in)
    return b_s[...] * _silu(gate[...])


def _rglru_core_kernel(gate_ref, u_ref, buf_ref, h0_ref, *refs, tt, valid):
    gate_refs = refs[:7]
    out_ref, nbuf_ref, hlast_ref, ubuf, g_s, a_s, b_s = refs[7:]
    ubuf[0:HIST - 3, :] = jnp.zeros((HIST - 3, LRU_W), F32)
    ubuf[HIST - 3:HIST, :] = buf_ref[0]
    ubuf[HIST:HIST + valid, :] = u_ref[0]
    ubuf[HIST + valid:HIST + tt, :] = jnp.zeros((tt - valid, LRU_W), F32)
    g_s[0:valid, :] = gate_ref[0]
    g_s[valid:tt, :] = jnp.zeros((tt - valid, LRU_W), F32)
    hg = _rglru_core(ubuf, g_s, a_s, b_s, h0_ref[0], gate_refs, tt=tt, unroll=False)
    out_ref[0] = hg[0:valid, :]
    nbuf_ref[0] = ubuf[HIST + valid - 3:HIST + valid, :]
    hlast_ref[0] = b_s[valid - 1:valid, :]


def _rglru_layer_kernel(xa_ref, xb_ref, xn_ref, buf_ref, h0_ref, win_ref, *refs, tt, n_steps):
    gate_refs = refs[:7]
    wout_ref, lng_ref, lnb_ref, out_ref, nbuf_ref, hlast_ref, ubuf, g_s, a_s, b_s, hcar = refs[7:]
    s = pl.program_id(0)
    first = s % n_steps == 0

    def project(x_ref, slot):
        xb = x_ref[0].astype(BF16)
        g_s[slot] = jnp.dot(xb, win_ref[:, 0:LRU_W], preferred_element_type=F32)
        ubuf[slot, HIST:HIST + tt, :] = jnp.dot(xb, win_ref[:, LRU_W:2 * LRU_W],
                                                preferred_element_type=F32)

    def finish(x_ref, hg):
        f = jnp.dot(hg.astype(BF16), wout_ref[...], preferred_element_type=F32)
        return _layer_norm(DN_ALPHA * x_ref[0] + f, lng_ref[...], lnb_ref[...])

    @pl.when(s == 0)
    def _():
        for slot in range(2):
            ubuf[slot, 0:HIST - 3, :] = jnp.zeros((HIST - 3, LRU_W), F32)
        project(xa_ref, 0)

    @pl.when(first)
    def _():
        ubuf[0, HIST - 3:HIST, :] = buf_ref[0]
        hcar[...] = h0_ref[0]

    @pl.when(jnp.logical_not(first))
    def _():
        ubuf[0, HIST - 3:HIST, :] = ubuf[1, HIST + tt - 3:HIST + tt, :]

    project(xb_ref, 1)
    hg = _rglru_core(ubuf.at[0], g_s.at[0], a_s, b_s, hcar[...], gate_refs, tt=tt, unroll=True)
    h_a = b_s[tt - 1:tt, :]
    out_ref[0, 0:tt, :] = finish(xa_ref, hg)
    ubuf[1, HIST - 3:HIST, :] = ubuf[0, HIST + tt - 3:HIST + tt, :]
    project(xn_ref, 0)
    hg = _rglru_core(ubuf.at[1], g_s.at[1], a_s, b_s, h_a, gate_refs, tt=tt, unroll=True)
    h_b = b_s[tt - 1:tt, :]
    hcar[...] = h_b
    out_ref[0, tt:2 * tt, :] = finish(xb_ref, hg)

    @pl.when(s % n_steps == n_steps - 1)
    def _():
        nbuf_ref[0] = ubuf[1, HIST + tt - 3:HIST + tt, :]
        hlast_ref[0] = h_b


def rglru_layer(x, conv_buf, h0, w, ln_g, ln_b):
    in_w, conv_w, conv_b, wr_bd, r_b, wi_bd, i_b, lam, out_w = w
    bsz, t_len, _ = x.shape
    row = lambda v: v.reshape(1, LRU_W)
    gate_args = (conv_w, row(conv_b), wr_bd, row(r_b), wi_bd, row(i_b), row(lam))
    h0r = h0.reshape(bsz, 1, LRU_W)
    state_shapes = [jax.ShapeDtypeStruct((bsz, CONV_W - 1, LRU_W), F32),
                    jax.ShapeDtypeStruct((bsz, 1, LRU_W), F32)]
    tt = 256
    if t_len % (2 * tt) == 0:
        n_steps = t_len // (2 * tt)
        n_tiles = bsz * t_len // tt
        const = lambda shape: pl.BlockSpec(shape, lambda s: (0,) * len(shape))
        x_tile = lambda off: pl.BlockSpec(
            (1, tt, D_MODEL), lambda s: (jnp.minimum(2 * s + off, n_tiles - 1), 0, 0))
        per_seq = lambda r: pl.BlockSpec((1, r, LRU_W), lambda s: (s // n_steps, 0, 0))
        gate_specs = [const((CONV_W, LRU_W)), const((1, LRU_W)),
                      const((LRU_W // MXU_DIM, MXU_DIM, MXU_DIM)), const((1, LRU_W)),
                      const((LRU_W // MXU_DIM, MXU_DIM, MXU_DIM)), const((1, LRU_W)),
                      const((1, LRU_W))]
        xt = x.reshape(n_tiles, tt, D_MODEL)
        out, nbuf, hlast = pl.pallas_call(
            functools.partial(_rglru_layer_kernel, tt=tt, n_steps=n_steps),
            grid=(bsz * n_steps,),
            in_specs=([x_tile(0), x_tile(1), x_tile(2), per_seq(CONV_W - 1), per_seq(1),
                       const((D_MODEL, 2 * LRU_W))] + gate_specs
                      + [const((LRU_W, D_MODEL)), const((1, D_MODEL)), const((1, D_MODEL))]),
            out_specs=[pl.BlockSpec((1, 2 * tt, D_MODEL), lambda s: (s, 0, 0)),
                       per_seq(CONV_W - 1), per_seq(1)],
            out_shape=[jax.ShapeDtypeStruct((bsz * n_steps, 2 * tt, D_MODEL), F32)] + state_shapes,
            scratch_shapes=[pltpu.VMEM((2, HIST + tt, LRU_W), F32),
                            pltpu.VMEM((2, tt, LRU_W), F32),
                            pltpu.VMEM((tt, LRU_W), F32),
                            pltpu.VMEM((tt, LRU_W), F32),
                            pltpu.VMEM((1, LRU_W), F32)],
            compiler_params=_cparams(1),
            name="rglru_layer",
        )(xt, xt, xt, conv_buf, h0r, in_w, *gate_args, out_w, row(ln_g), row(ln_b))
        return out.reshape(bsz, t_len, D_MODEL), nbuf, hlast.reshape(bsz, LRU_W)

    tt, valid = 16, t_len
    assert t_len <= tt
    x2 = x.reshape(bsz * t_len, D_MODEL)
    gate, u = proj_split(x2, in_w, ((0, LRU_W), (LRU_W, LRU_W)), _row_tile(bsz * t_len))
    vec = lambda: pl.BlockSpec((1, LRU_W), lambda b: (0, 0))
    wspec = lambda: pl.BlockSpec((LRU_W // MXU_DIM, MXU_DIM, MXU_DIM), lambda b: (0, 0, 0))
    per_seq = lambda r: pl.BlockSpec((1, r, LRU_W), lambda b: (b, 0, 0))
    hg, nbuf, hlast = pl.pallas_call(
        functools.partial(_rglru_core_kernel, tt=tt, valid=valid),
        grid=(bsz,),
        in_specs=[per_seq(valid), per_seq(valid), per_seq(CONV_W - 1), per_seq(1),
                  pl.BlockSpec((CONV_W, LRU_W), lambda b: (0, 0)),
                  vec(), wspec(), vec(), wspec(), vec(), vec()],
        out_specs=[per_seq(valid), per_seq(CONV_W - 1), per_seq(1)],
        out_shape=[jax.ShapeDtypeStruct((bsz, t_len, LRU_W), F32)] + state_shapes,
        scratch_shapes=[pltpu.VMEM((HIST + tt, LRU_W), F32),
                        pltpu.VMEM((tt, LRU_W), F32),
                        pltpu.VMEM((tt, LRU_W), F32),
                        pltpu.VMEM((tt, LRU_W), F32)],
        compiler_params=_cparams(1),
        name="rglru_core",
    )(gate.reshape(bsz, t_len, LRU_W), u.reshape(bsz, t_len, LRU_W), conv_buf, h0r, *gate_args)
    out = outproj_ln(hg.reshape(bsz * t_len, LRU_W), out_w, x2, ln_g, ln_b,
                     _row_tile(bsz * t_len)).reshape(bsz, t_len, D_MODEL)
    return out, nbuf, hlast.reshape(bsz, LRU_W)


def _block_diag_tiles(w):
    per = MXU_DIM // LRU_BS
    w4 = w.reshape(LRU_W // MXU_DIM, per, LRU_BS, LRU_BS)
    eye = jnp.eye(per, dtype=w.dtype)
    t = jnp.einsum("jakc,ab->jakbc", w4, eye)
    return t.reshape(LRU_W // MXU_DIM, MXU_DIM, MXU_DIM).astype(BF16)


def _rglru_weights(in_w, conv_w, conv_b, r_w, r_b, i_w, i_b, lam, out_w):
    return (in_w.astype(BF16), conv_w, conv_b, _block_diag_tiles(0.5 * r_w), 0.5 * r_b,
            _block_diag_tiles(0.5 * i_w), 0.5 * i_b, lam, out_w.astype(BF16))


def _diff_lambda(lp, lam_init):
    s1 = jnp.sum(lp[0:1, :] * lp[1:2, :], axis=-1, keepdims=True)
    s2 = jnp.sum(lp[2:3, :] * lp[3:4, :], axis=-1, keepdims=True)
    return jnp.exp(s1) - jnp.exp(s2) + lam_init


def _subln_gate(o, sw, gate, lam_init):
    ms = jnp.mean(o * o, axis=-1, keepdims=True)
    return (o * lax.rsqrt(ms + EPS) * sw * (1.0 - lam_init)) * _silu(gate)


def _attn_prompt_kernel(lp_ref, q_ref, kt_ref, v_ref, gate_ref, sw_ref, o_ref,
                        kb_s, vx_s, m_s, accl_s, s_s, *, tq, lam_init):
    qi = pl.program_id(2)
    hd = 2 * ATT_DH
    t_len = v_ref.shape[1]

    @pl.when(qi == 0)
    def _():
        for i in range(t_len // tq):
            kb_s[i] = kt_ref[0, :, i * tq:(i + 1) * tq].astype(BF16)

        def cvt(i, carry):
            r = pl.multiple_of(i * tq, tq)
            vx_s[pl.ds(r, tq), 0:hd] = v_ref[0, pl.ds(r, tq), :].astype(BF16)
            vx_s[pl.ds(r, tq), hd:2 * hd] = jnp.ones((tq, hd), BF16)
            return carry
        lax.fori_loop(0, t_len // tq, cvt, 0)

    lam = _diff_lambda(lp_ref[...], lam_init)
    q = q_ref[0] * (ATT_DH ** -0.5 * LOG2E)
    lane = lax.broadcasted_iota(jnp.int32, (tq, hd), 1)
    q_maps = (jnp.where(lane < ATT_DH, q, 0.0).astype(BF16),
              jnp.where(lane >= ATT_DH, q, 0.0).astype(BF16))
    m_s[...] = jnp.full(m_s.shape, NEG_BIG, F32)
    accl_s[...] = jnp.zeros(accl_s.shape, F32)
    rowi = lax.broadcasted_iota(jnp.int32, (tq, tq), 0)
    coli = lax.broadcasted_iota(jnp.int32, (tq, tq), 1)

    def scores(ki, slot):
        kb = kb_s[ki]
        for c in range(2):
            s_s[slot, c] = jnp.dot(q_maps[c], kb, preferred_element_type=F32)

    def absorb(ki, slot, masked):
        ks = pl.multiple_of(ki * tq, tq)
        vx = vx_s[pl.ds(ks, tq), :]
        for c in range(2):
            s = s_s[slot, c]
            if masked:
                s = jnp.where(coli <= rowi, s, NEG_BIG)
            m_prev = m_s[c]
            m_new = jnp.maximum(m_prev, jnp.max(s, axis=-1, keepdims=True))
            alpha = jnp.exp2(m_prev - m_new)
            p = jnp.exp2(s - jnp.concatenate([m_new] * (tq // hd), axis=1))
            pv = jnp.dot(p.astype(BF16), vx, preferred_element_type=F32)
            accl_s[c] = jnp.concatenate([alpha, alpha], axis=1) * accl_s[c] + pv
            m_s[c] = m_new

    scores(0, 0)

    def pair(j, carry):
        k0 = 2 * j
        scores(k0 + 1, 1)
        absorb(k0, 0, False)
        scores(k0 + 2, 0)
        absorb(k0 + 1, 1, False)
        return carry

    lax.fori_loop(0, qi // 2, pair, 0)

    @pl.when(qi % 2 == 0)
    def _():
        absorb(qi, 0, True)

    @pl.when(qi % 2 == 1)
    def _():
        scores(qi, 1)
        absorb(qi - 1, 0, False)
        absorb(qi, 1, True)

    a1 = accl_s[0]
    a2 = accl_s[1]
    o = a1[:, 0:hd] / a1[:, hd:2 * hd] - lam * (a2[:, 0:hd] / a2[:, hd:2 * hd])
    o_ref[0] = _subln_gate(o, sw_ref[...], gate_ref[0], lam_init)


def _attn_sample_parts(lp_ref, qbd_ref, *refs, n_pg, t_new, lam_init):
    k_refs = refs[:n_pg]
    v_refs = refs[n_pg:2 * n_pg]
    knew_ref, vnew_ref, gate_ref, sw_ref, o_ref, m_s, l_s, acc_s = refs[2 * n_pg:]
    j = pl.program_id(2)
    n_j = pl.num_programs(2)
    n_rows = ATT_HEADS * 2 * t_new
    hd = 2 * ATT_DH

    def start():
        @pl.when(j == 0)
        def _():
            m_s[...] = jnp.full(m_s.shape, NEG_BIG, F32)
            l_s[...] = jnp.zeros(l_s.shape, F32)
            acc_s[...] = jnp.zeros(acc_s.shape, F32)

    def attend(kts, vbs, mask):
        qbd = qbd_ref[0]
        n = len(kts)
        s = jnp.concatenate([jnp.dot(qbd, kt.astype(BF16), preferred_element_type=F32)
                             for kt in kts], axis=1)
        if mask is not None:
            s = jnp.where(mask, s, NEG_BIG)
        m_prev = m_s[...]
        m_new = jnp.maximum(m_prev, jnp.max(s, axis=-1, keepdims=True))
        alpha = jnp.exp2(m_prev - m_new)
        p = jnp.exp2(s - jnp.concatenate([m_new] * n, axis=1))
        l_s[...] = alpha * l_s[...] + jnp.sum(p, axis=-1, keepdims=True)
        pb = p.astype(BF16)
        pv = jnp.dot(pb[:, 0:PAGE_SIZE], vbs[0].astype(BF16), preferred_element_type=F32)
        for i in range(1, n):
            pv = pv + jnp.dot(pb[:, i * PAGE_SIZE:(i + 1) * PAGE_SIZE], vbs[i].astype(BF16),
                              preferred_element_type=F32)
        acc_s[...] = jnp.concatenate([alpha] * (ATT_W // LANES), axis=1) * acc_s[...] + pv
        m_s[...] = m_new

    def page_v(vr):
        return jnp.concatenate([vr[0, pl.ds(h, PAGE_SIZE, stride=ATT_HEADS), :]
                                for h in range(ATT_HEADS)], axis=1)

    def pages():
        attend([r[0] for r in k_refs], [page_v(r) for r in v_refs], None)

    def finish():
        @pl.when(j == n_j - 1)
        def _():
            rowi = lax.broadcasted_iota(jnp.int32, (n_rows, PAGE_SIZE), 0)
            coli = lax.broadcasted_iota(jnp.int32, (n_rows, PAGE_SIZE), 1)
            attend([knew_ref[0]], [vnew_ref[0]], coli <= (rowi % t_new))
            lam = _diff_lambda(lp_ref[...], lam_init)
            sw = sw_ref[...]
            for h in range(ATT_HEADS):
                r1 = h * 2 * t_new
                r2 = r1 + t_new
                cs = slice(h * hd, (h + 1) * hd)
                o1 = acc_s[r1:r1 + t_new, cs] / l_s[r1:r1 + t_new, :]
                o2 = acc_s[r2:r2 + t_new, cs] / l_s[r2:r2 + t_new, :]
                o_ref[0, :, cs] = _subln_gate(o1 - lam * o2, sw, gate_ref[0, :, cs], lam_init)

    return start, pages, finish


def _attn_kernel(pt_ref, lp_ref, q_ref, kt_ref, v_ref, gate_ref, sw_ref, qbd_ref, *refs,
                 n_pg, tq, t_new, lam_init):
    del pt_ref
    sample_in = refs[:2 * n_pg + 3]
    o_ref, os_ref, kb_s, vx_s, m_s, accl_s, s_s, ms_s, ls_s, accs_s = refs[2 * n_pg + 3:]
    start, pages, finish = _attn_sample_parts(
        lp_ref, qbd_ref, *sample_in, sw_ref, os_ref, ms_s, ls_s, accs_s,
        n_pg=n_pg, t_new=t_new, lam_init=lam_init)
    _attn_prompt_kernel(lp_ref, q_ref, kt_ref, v_ref, gate_ref, sw_ref, o_ref,
                        kb_s, vx_s, m_s, accl_s, s_s, tq=tq, lam_init=lam_init)
    start()
    pages()
    finish()


def attn_core(q, kt, v, gate, qs, ks, vs, gate_s, cache_k, cache_v, layer, page_table, lam_p,
              subln_w, lam_init, tq=512):
    bsz_p, t_len, _ = q.shape
    bsz, t_new, _ = qs.shape
    n_pages = page_table.shape[1]
    n_pool = cache_k.shape[1]
    hd = 2 * ATT_DH
    n_rows = ATT_HEADS * 2 * t_new
    n_q = t_len // tq
    assert bsz == bsz_p * ATT_HEADS and n_pages % n_q == 0
    n_pg = n_pages // n_q
    q4 = (qs * (ATT_DH ** -0.5 * LOG2E)).reshape(bsz, t_new, 2 * ATT_HEADS, ATT_DH)
    eye = jnp.eye(2 * ATT_HEADS, dtype=F32)
    qbd = jnp.einsum("bqhd,hg->bhqgd", q4, eye).reshape(bsz, n_rows, ATT_W).astype(BF16)
    k_new = jnp.pad(jnp.swapaxes(ks, 1, 2), ((0, 0), (0, 0), (0, PAGE_SIZE - t_new)))
    v_new = jnp.pad(vs, ((0, 0), (0, PAGE_SIZE - t_new), (0, 0)))
    ck = jnp.transpose(cache_k, (0, 1, 3, 4, 5, 2)).reshape(-1, ATT_W, PAGE_SIZE)
    cv = cache_v.reshape(-1, PAGE_SIZE * ATT_HEADS, hd)
    pt = page_table.reshape(-1) + layer * n_pool

    sample = lambda b, h: b * ATT_HEADS + h

    def page_spec(i, rows, width):
        return pl.BlockSpec(
            (1, rows, width),
            lambda b, h, j, pt_ref: (pt_ref[sample(b, h) * n_pages + j * n_pg + i], 0, 0))

    const = lambda shape: pl.BlockSpec(shape, lambda b, h, j, pt_ref: (0,) * len(shape))
    qspec = lambda: pl.BlockSpec((1, tq, hd), lambda b, h, j, pt_ref: (b, j, h))
    per_s = lambda r, w: pl.BlockSpec((1, r, w), lambda b, h, j, pt_ref: (sample(b, h), 0, 0))
    grid_spec = pltpu.PrefetchScalarGridSpec(
        num_scalar_prefetch=1,
        grid=(bsz_p, ATT_HEADS, n_q),
        in_specs=([const((4, ATT_DH)), qspec(),
                   pl.BlockSpec((1, hd, t_len), lambda b, h, j, pt_ref: (b, h, 0)),
                   pl.BlockSpec((1, t_len, hd), lambda b, h, j, pt_ref: (b, 0, h)),
                   qspec(), const((1, hd)), per_s(n_rows, ATT_W)]
                  + [page_spec(i, ATT_W, PAGE_SIZE) for i in range(n_pg)]
                  + [page_spec(i, PAGE_SIZE * ATT_HEADS, hd) for i in range(n_pg)]
                  + [per_s(ATT_W, PAGE_SIZE), per_s(PAGE_SIZE, ATT_W), per_s(t_new, ATT_W)]),
        out_specs=[qspec(), per_s(t_new, ATT_W)],
        scratch_shapes=[pltpu.VMEM((n_q, hd, tq), BF16),
                        pltpu.VMEM((t_len, 2 * hd), BF16),
                        pltpu.VMEM((2, tq, hd), F32), pltpu.VMEM((2, tq, 2 * hd), F32),
                        pltpu.VMEM((2, 2, tq, tq), F32),
                        pltpu.VMEM((n_rows, LANES), F32), pltpu.VMEM((n_rows, LANES), F32),
                        pltpu.VMEM((n_rows, ATT_W), F32)],
    )
    return pl.pallas_call(
        functools.partial(_attn_kernel, n_pg=n_pg, tq=tq, t_new=t_new, lam_init=lam_init),
        grid_spec=grid_spec,
        out_shape=[jax.ShapeDtypeStruct((bsz_p, t_len, ATT_W), F32),
                   jax.ShapeDtypeStruct((bsz, t_new, ATT_W), F32)],
        compiler_params=_cparams(3),
        name="attn_core",
    )(pt, lam_p, q, kt, v, gate, subln_w.reshape(1, hd), qbd, *([ck] * n_pg), *([cv] * n_pg),
      k_new, v_new, gate_s)


def _ssd_core(ub, z, dt_raw, hout_ref, y_s, scan_refs, *, vl, after_group=None):
    cw_ref, cb_ref, dtb_ref, alog_ref, dexp_ref, nw_ref = scan_refs
    L = SSD_CHUNK
    gw = SSD_HPG * SSD_HEADDIM
    rowL = lax.broadcasted_iota(jnp.int32, (L, LANES), 0)
    laneL = lax.broadcasted_iota(jnp.int32, (L, LANES), 1)
    dt = _softplus(dt_raw + dtb_ref[...])
    dt = jnp.where((rowL < vl) & (laneL < SSD_HEADS), dt, 0.0)
    adt = dt * (-jnp.exp(alog_ref[...]))
    tri = (lax.broadcasted_iota(jnp.int32, (L, L), 0)
           >= lax.broadcasted_iota(jnp.int32, (L, L), 1))
    cs = jnp.dot(tri.astype(F32), adt, preferred_element_type=F32,
                 precision=lax.Precision.HIGHEST)
    cs_last = cs[L - 1:L, :]
    w1 = dt * jnp.exp(cs_last - cs)
    cs_t = cs.T
    dt_t = dt.T
    w1_t = w1.T

    cw = cw_ref[...]
    lane_g = lax.broadcasted_iota(jnp.int32, (L, gw), 1)

    def conv_silu(lo, width):
        sl = slice(lo, lo + width)
        y = cb_ref[:, sl] + cw[3:4, sl] * ub[HIST:HIST + L, sl]
        for k in range(1, CONV_W):
            y = y + cw[3 - k:4 - k, sl] * ub[HIST - k:HIST - k + L, sl]
        return _silu(y)

    for g in range(SSD_GROUPS):
        xh = conv_silu(g * gw, gw)
        bm = conv_silu(SSD_INNER + g * SSD_STATE, SSD_STATE)
        cm = conv_silu(SSD_INNER + (SSD_GROUPS + g) * SSD_STATE, SSD_STATE)
        bmb = bm.astype(BF16)
        cmb = cm.astype(BF16)
        xhb = xh.astype(BF16)
        cbm = _nt(cmb, bmb)
        y_diag = jnp.zeros((L, gw), F32)
        e_cols = []
        w_rows = []
        d_rows = []
        for r in range(SSD_HPG):
            h = g * SSD_HPG + r
            cs_col = cs[:, h:h + 1]
            cs_row = cs_t[h:h + 1, :]
            lm = jnp.exp(jnp.where(tri, cs_col - cs_row, NEG_BIG))
            mat = (cbm * lm * dt_t[h:h + 1, :]).astype(BF16)
            in_head = (lane_g >= r * SSD_HEADDIM) & (lane_g < (r + 1) * SSD_HEADDIM)
            xr = jnp.where(in_head, xhb, jnp.zeros_like(xhb))
            y_diag = y_diag + jnp.dot(mat, xr, preferred_element_type=F32)
            e_cols.append(jnp.broadcast_to(jnp.exp(cs_col), (L, SSD_HEADDIM)))
            w_rows.append(jnp.broadcast_to(w1_t[h:h + 1, :], (SSD_HEADDIM, L)))
            d_rows.append(jnp.broadcast_to(jnp.exp(cs_t[h:h + 1, L - 1:L]),
                                           (SSD_HEADDIM, SSD_STATE)))
        hg = hout_ref[0, g]
        y_off = _nt(cmb, hg.astype(BF16)) * jnp.concatenate(e_cols, axis=1)
        y_s[:, g * gw:(g + 1) * gw] = y_diag + y_off + dexp_ref[:, g * gw:(g + 1) * gw] * xh
        xd_t = (xh.T * jnp.concatenate(w_rows, axis=0)).astype(BF16)
        states = jnp.dot(xd_t, bmb, preferred_element_type=F32)
        hout_ref[0, g] = hg * jnp.concatenate(d_rows, axis=0) + states
        if after_group is not None:
            after_group(g)

    gated = y_s[...] * _silu(z[...])
    ms = jnp.mean(gated * gated, axis=-1, keepdims=True)
    return gated * lax.rsqrt(ms + EPS) * nw_ref[...]


def _ssd_core_kernel(xbc_ref, z_ref, dtr_ref, buf_ref, h0_ref, *refs, vl):
    scan_refs = refs[:6]
    out_ref, nbuf_ref, hout_ref, ubuf, z_s, dt_s, y_s = refs[6:]
    L = SSD_CHUNK
    ubuf[HIST - 3:HIST, :] = buf_ref[0]
    ubuf[HIST:HIST + vl, :] = xbc_ref[0]
    ubuf[HIST + vl:HIST + L, :] = jnp.zeros((L - vl, SSD_CONV_DIM), F32)
    z_s[0:vl, :] = z_ref[0]
    z_s[vl:L, :] = jnp.zeros((L - vl, SSD_INNER), F32)
    dt_s[0:vl, :] = dtr_ref[0]
    dt_s[vl:L, :] = jnp.zeros((L - vl, LANES), F32)
    hout_ref[0] = h0_ref[0]
    yn = _ssd_core(ubuf, z_s, dt_s[...], hout_ref, y_s, scan_refs, vl=vl)
    out_ref[0] = yn[0:vl, :]
    nbuf_ref[0] = ubuf[HIST + vl - 3:HIST + vl, :]


def _ssd_layer_kernel(xa_ref, xb_ref, xn_ref, buf_ref, h0_ref, win_ref, *refs, n_steps):
    scan_refs = refs[:6]
    (wout_ref, lng_ref, lnb_ref, out_ref, nbuf_ref, hout_ref, ubuf, z_s, dt_s, y_s) = refs[6:]
    L = SSD_CHUNK
    s = pl.program_id(0)
    first = s % n_steps == 0
    xbc_lo = SSD_INNER
    dt_lo = SSD_INNER + SSD_CONV_DIM

    n_sl = SSD_GROUPS
    zw = SSD_INNER // n_sl
    cwid = SSD_CONV_DIM // n_sl

    def project_slice(xb, slot, i):
        z_s[slot, :, i * zw:(i + 1) * zw] = jnp.dot(
            xb, win_ref[:, i * zw:(i + 1) * zw], preferred_element_type=F32)
        ubuf[slot, HIST:HIST + L, i * cwid:(i + 1) * cwid] = jnp.dot(
            xb, win_ref[:, xbc_lo + i * cwid:xbc_lo + (i + 1) * cwid],
            preferred_element_type=F32)
        if i == 0:
            dt_s[slot] = jnp.dot(xb, win_ref[:, dt_lo:dt_lo + LANES],
                                 preferred_element_type=F32)

    def finish(x_ref, yn):
        f = jnp.dot(yn.astype(BF16), wout_ref[...], preferred_element_type=F32)
        return _layer_norm(DN_ALPHA * x_ref[0] + f, lng_ref[...], lnb_ref[...])

    @pl.when(s == 0)
    def _():
        xb = xa_ref[0].astype(BF16)
        for i in range(n_sl):
            project_slice(xb, 0, i)

    @pl.when(first)
    def _():
        ubuf[0, HIST - 3:HIST, :] = buf_ref[0]
        hout_ref[0] = h0_ref[0]

    @pl.when(jnp.logical_not(first))
    def _():
        ubuf[0, HIST - 3:HIST, :] = ubuf[1, HIST + L - 3:HIST + L, :]

    xb_b = xb_ref[0].astype(BF16)
    yn = _ssd_core(ubuf.at[0], z_s.at[0], dt_s[0], hout_ref, y_s, scan_refs, vl=L,
                   after_group=lambda g: project_slice(xb_b, 1, g))
    out_ref[0, 0:L, :] = finish(xa_ref, yn)
    ubuf[1, HIST - 3:HIST, :] = ubuf[0, HIST + L - 3:HIST + L, :]
    xb_n = xn_ref[0].astype(BF16)
    yn = _ssd_core(ubuf.at[1], z_s.at[1], dt_s[1], hout_ref, y_s, scan_refs, vl=L,
                   after_group=lambda g: project_slice(xb_n, 0, g))
    out_ref[0, L:2 * L, :] = finish(xb_ref, yn)

    @pl.when(s % n_steps == n_steps - 1)
    def _():
        nbuf_ref[0] = ubuf[1, HIST + L - 3:HIST + L, :]


def ssd_layer(x, conv_buf, h0, w, ln_g, ln_b):
    in_w_pad, conv_w, conv_b, dt_bias, a_log, d_skip, norm_w, out_w = w
    bsz, t_len, _ = x.shape
    L = SSD_CHUNK
    gw = SSD_HPG * SSD_HEADDIM
    n_proj = in_w_pad.shape[1]
    pad_lanes = lambda v: jnp.pad(v.reshape(1, SSD_HEADS), ((0, 0), (0, LANES - SSD_HEADS)))
    dexp = jnp.repeat(d_skip, SSD_HEADDIM).reshape(1, SSD_INNER)
    h0g = h0.reshape(bsz, SSD_GROUPS, gw, SSD_STATE)
    scan_args = (conv_w, conv_b.reshape(1, SSD_CONV_DIM), pad_lanes(dt_bias), pad_lanes(a_log),
                 dexp, norm_w.reshape(1, SSD_INNER))
    scan_shapes = [(CONV_W, SSD_CONV_DIM), (1, SSD_CONV_DIM), (1, LANES), (1, LANES),
                   (1, SSD_INNER), (1, SSD_INNER)]
    state_shapes = [jax.ShapeDtypeStruct((bsz, CONV_W - 1, SSD_CONV_DIM), F32),
                    jax.ShapeDtypeStruct((bsz, SSD_GROUPS, gw, SSD_STATE), F32)]
    if t_len % (2 * L) == 0:
        n_steps = t_len // (2 * L)
        n_chunks = bsz * t_len // L
        const = lambda shape: pl.BlockSpec(shape, lambda s: (0,) * len(shape))
        resident = lambda shape: pl.BlockSpec(shape, lambda s: (0,) * len(shape),
                                              pipeline_mode=pl.Buffered(1))
        x_chunk = lambda off: pl.BlockSpec(
            (1, L, D_MODEL), lambda s: (jnp.minimum(2 * s + off, n_chunks - 1), 0, 0))
        state_specs = [pl.BlockSpec((1, CONV_W - 1, SSD_CONV_DIM), lambda s: (s // n_steps, 0, 0)),
                       pl.BlockSpec((1, SSD_GROUPS, gw, SSD_STATE),
                                    lambda s: (s // n_steps, 0, 0, 0))]
        xt = x.reshape(n_chunks, L, D_MODEL)
        out, nbuf, hout = pl.pallas_call(
            functools.partial(_ssd_layer_kernel, n_steps=n_steps),
            grid=(bsz * n_steps,),
            in_specs=([x_chunk(0), x_chunk(1), x_chunk(2)] + state_specs
                      + [resident((D_MODEL, n_proj))] + [const(sh) for sh in scan_shapes]
                      + [resident((SSD_INNER, D_MODEL)), const((1, D_MODEL)), const((1, D_MODEL))]),
            out_specs=[pl.BlockSpec((1, 2 * L, D_MODEL), lambda s: (s, 0, 0))] + state_specs,
            out_shape=[jax.ShapeDtypeStruct((bsz * n_steps, 2 * L, D_MODEL), F32)] + state_shapes,
            scratch_shapes=[pltpu.VMEM((2, HIST + L, SSD_CONV_DIM), F32),
                            pltpu.VMEM((2, L, SSD_INNER), F32),
                            pltpu.VMEM((2, L, LANES), F32),
                            pltpu.VMEM((L, SSD_INNER), F32)],
            compiler_params=_cparams(1),
            name="ssd_layer",
        )(xt, xt, xt, conv_buf, h0g, in_w_pad, *scan_args, out_w,
          ln_g.reshape(1, D_MODEL), ln_b.reshape(1, D_MODEL))
        return (out.reshape(bsz, t_len, D_MODEL), nbuf,
                hout.reshape(bsz, SSD_HEADS, SSD_HEADDIM, SSD_STATE))

    vl = t_len
    assert vl <= L
    x2 = x.reshape(bsz * t_len, D_MODEL)
    splits = ((0, SSD_INNER), (SSD_INNER, SSD_CONV_DIM), (SSD_INNER + SSD_CONV_DIM, LANES))
    z, xbc, dt_raw = proj_split(x2, in_w_pad, splits, _row_tile(bsz * t_len))
    const = lambda shape: pl.BlockSpec(shape, lambda b: (0,) * len(shape))
    rows = lambda width: pl.BlockSpec((1, vl, width), lambda b: (b, 0, 0))
    state_specs = [pl.BlockSpec((1, CONV_W - 1, SSD_CONV_DIM), lambda b: (b, 0, 0)),
                   pl.BlockSpec((1, SSD_GROUPS, gw, SSD_STATE), lambda b: (b, 0, 0, 0))]
    yn, nbuf, hout = pl.pallas_call(
        functools.partial(_ssd_core_kernel, vl=vl),
        grid=(bsz,),
        in_specs=([rows(SSD_CONV_DIM), rows(SSD_INNER), rows(LANES)] + state_specs
                  + [const(sh) for sh in scan_shapes]),
        out_specs=[rows(SSD_INNER)] + state_specs,
        out_shape=[jax.ShapeDtypeStruct((bsz, t_len, SSD_INNER), F32)] + state_shapes,
        scratch_shapes=[pltpu.VMEM((HIST + L, SSD_CONV_DIM), F32),
                        pltpu.VMEM((L, SSD_INNER), F32),
                        pltpu.VMEM((L, LANES), F32),
                        pltpu.VMEM((L, SSD_INNER), F32)],
        compiler_params=_cparams(1),
        name="ssd_core",
    )(xbc.reshape(bsz, t_len, SSD_CONV_DIM), z.reshape(bsz, t_len, SSD_INNER),
      dt_raw.reshape(bsz, t_len, LANES), conv_buf, h0g, *scan_args)
    out = outproj_ln(yn.reshape(bsz * t_len, SSD_INNER), out_w, x2, ln_g, ln_b,
                     _row_tile(bsz * t_len)).reshape(bsz, t_len, D_MODEL)
    return out, nbuf, hout.reshape(bsz, SSD_HEADS, SSD_HEADDIM, SSD_STATE)


def _row_tile(rows):
    return 256 if rows % 256 == 0 else rows


def _out_tile(rows):
    return 1024 if rows % 1024 == 0 else _row_tile(rows)


def _attn_proj_t_kernel(x_ref, w_ref, wkt_ref, q_ref, kt_ref, v_ref, g_ref):
    x = x_ref[0].astype(BF16)
    q_ref[0] = jnp.dot(x, w_ref[:, 0:ATT_W], preferred_element_type=F32)
    kt_ref[0] = _nt(wkt_ref[...], x)
    v_ref[0] = jnp.dot(x, w_ref[:, 2 * ATT_W:3 * ATT_W], preferred_element_type=F32)
    g_ref[0] = jnp.dot(x, w_ref[:, 3 * ATT_W:4 * ATT_W], preferred_element_type=F32)


def attn_project_t(x, in_w, wkt, tm=256):
    bsz, t_len, _ = x.shape
    rows = lambda: pl.BlockSpec((1, tm, ATT_W), lambda b, i: (b, i, 0))
    return pl.pallas_call(
        _attn_proj_t_kernel,
        grid=(bsz, t_len // tm),
        in_specs=[pl.BlockSpec((1, tm, D_MODEL), lambda b, i: (b, i, 0)),
                  pl.BlockSpec((D_MODEL, 4 * ATT_W), lambda b, i: (0, 0)),
                  pl.BlockSpec((ATT_W, D_MODEL), lambda b, i: (0, 0))],
        out_specs=[rows(), pl.BlockSpec((1, ATT_W, tm), lambda b, i: (b, 0, i)), rows(), rows()],
        out_shape=[jax.ShapeDtypeStruct((bsz, t_len, ATT_W), F32),
                   jax.ShapeDtypeStruct((bsz, ATT_W, t_len), F32),
                   jax.ShapeDtypeStruct((bsz, t_len, ATT_W), F32),
                   jax.ShapeDtypeStruct((bsz, t_len, ATT_W), F32)],
        compiler_params=_cparams(2),
        name="attn_project_t",
    )(x, in_w, wkt)


def _attn_weights(in_w_f32):
    return in_w_f32.astype(BF16), in_w_f32[:, ATT_W:2 * ATT_W].T.astype(BF16)


def _attn_project(x, in_w):
    bsz, t_len, _ = x.shape
    x2 = x.reshape(bsz * t_len, D_MODEL)
    tm = _row_tile(bsz * t_len)
    splits = tuple((i * ATT_W, ATT_W) for i in range(4))
    q, k, v, gate = proj_split(x2, in_w, splits, tm)
    shp = (bsz, t_len, ATT_W)
    return x2, tm, q.reshape(shp), k.reshape(shp), v.reshape(shp), gate.reshape(shp)


def kernel(x_prompt, x_sample, cache_k, cache_v, page_table, state_lru_conv, state_lru_h, state_ssd_conv, state_ssd_h, ln_g, ln_b, a_in_w, a_conv_w, a_conv_b, a_gate_r_w, a_gate_r_b, a_gate_i_w, a_gate_i_b, a_lambda, a_out_w, b_in_w, b_lambda, b_subln_w, b_out_w, c_in_w, c_conv_w, c_conv_b, c_dt_bias, c_a_log, c_d, c_norm_w, c_out_w):
    xp, xs = x_prompt, x_sample
    bp, bs = xp.shape[0], xs.shape[0]
    tp, ts = xp.shape[1], xs.shape[1]
    k_p, v_p, k_s, v_s = [], [], [], []
    lc_p, lh_p, lc_s, lh_s = [], [], [], []
    sc_p, sh_p, sc_s, sh_s = [], [], [], []
    for i in range(DEPTH):
        j = i // N_MIXERS
        kind = i % N_MIXERS
        if kind == 0:
            w = _rglru_weights(a_in_w[j], a_conv_w[j], a_conv_b[j], a_gate_r_w[j], a_gate_r_b[j],
                               a_gate_i_w[j], a_gate_i_b[j], a_lambda[j], a_out_w[j])
            zc = jnp.zeros((bp, CONV_W - 1, LRU_W), F32)
            zh = jnp.zeros((bp, LRU_W), F32)
            xp, c1, h1 = rglru_layer(xp, zc, zh, w, ln_g[i], ln_b[i])
            xs, c2, h2 = rglru_layer(xs, state_lru_conv[j], state_lru_h[j], w, ln_g[i], ln_b[i])
            lc_p.append(c1); lh_p.append(h1); lc_s.append(c2); lh_s.append(h2)
        elif kind == 1:
            lam_init = 0.8 - 0.6 * math.exp(-0.3 * i)
            in_w, wkt = _attn_weights(b_in_w[j])
            out_w = b_out_w[j].astype(BF16)
            q, kt, v, g = attn_project_t(xp, in_w, wkt)
            x2s, tms, qs, ks, vs, gs = _attn_project(xs, in_w)
            og, ogs = attn_core(q, kt, v, g, qs, ks, vs, gs, cache_k, cache_v, j, page_table,
                                b_lambda[j], b_subln_w[j], lam_init)
            xp = outproj_ln(og.reshape(bp * tp, ATT_W), out_w, xp.reshape(bp * tp, D_MODEL),
                            ln_g[i], ln_b[i], _out_tile(bp * tp)).reshape(bp, tp, D_MODEL)
            xs = outproj_ln(ogs.reshape(bs * ts, ATT_W), out_w, x2s, ln_g[i], ln_b[i], tms
                            ).reshape(bs, ts, D_MODEL)
            k_p.append(jnp.transpose(kt.reshape(bp, ATT_HEADS, 2, ATT_DH, tp), (0, 4, 1, 2, 3)))
            v_p.append(v.reshape(bp, tp, ATT_HEADS, 2 * ATT_DH))
            k_s.append(ks.reshape(bs, ts, ATT_HEADS, 2, ATT_DH))
            v_s.append(vs.reshape(bs, ts, ATT_HEADS, 2 * ATT_DH))
        else:
            in_w_pad = jnp.pad(c_in_w[j], ((0, 0), (0, LANES - SSD_HEADS))).astype(BF16)
            w = (in_w_pad, c_conv_w[j], c_conv_b[j], c_dt_bias[j], c_a_log[j], c_d[j], c_norm_w[j],
                 c_out_w[j].astype(BF16))
            zc = jnp.zeros((bp, CONV_W - 1, SSD_CONV_DIM), F32)
            zh = jnp.zeros((bp, SSD_HEADS, SSD_HEADDIM, SSD_STATE), F32)
            xp, c1, h1 = ssd_layer(xp, zc, zh, w, ln_g[i], ln_b[i])
            xs, c2, h2 = ssd_layer(xs, state_ssd_conv[j], state_ssd_h[j], w, ln_g[i], ln_b[i])
            sc_p.append(c1); sh_p.append(h1); sc_s.append(c2); sh_s.append(h2)
    return (xp, xs, jnp.stack(k_p), jnp.stack(v_p), jnp.stack(k_s), jnp.stack(v_s),
            jnp.stack(lc_p), jnp.stack(lh_p), jnp.stack(lc_s), jnp.stack(lh_s),
            jnp.stack(sc_p), jnp.stack(sh_p), jnp.stack(sc_s), jnp.stack(sh_s))
```

```python
import functools
import math

import jax
import jax.numpy as jnp
from jax import lax
from jax.experimental import pallas as pl
from jax.experimental.pallas import tpu as pltpu

F32 = jnp.float32
BF16 = jnp.bfloat16

D_MODEL = 1024
DEPTH = 4
PAGE_SIZE = 128
N_MIXERS = 3
DN_ALPHA = (2.0 * DEPTH) ** 0.25
EPS = 1e-5
CONV_W = 4
LRU_W = D_MODEL
LRU_BLOCKS = 16
LRU_BS = LRU_W // LRU_BLOCKS
LRU_C = 8.0
ATT_HEADS = 8
ATT_DH = D_MODEL // (2 * ATT_HEADS)
ATT_W = ATT_HEADS * 2 * ATT_DH
SSD_INNER = 2 * D_MODEL
SSD_HEADDIM = 64
SSD_HEADS = SSD_INNER // SSD_HEADDIM
SSD_GROUPS = 8
SSD_HPG = SSD_HEADS // SSD_GROUPS
SSD_STATE = 128
SSD_CONV_DIM = SSD_INNER + 2 * SSD_GROUPS * SSD_STATE
SSD_CHUNK = 128

LANES = 128
SUBLANES = 8
MXU_DIM = 256
VMEM_LIMIT = 56 * 1024 * 1024
LOG2E = math.log2(math.e)
F32_TINY = 1.1754944e-38
NEG_BIG = -1e30
HIST = SUBLANES

ROW_TILE = MXU_DIM
ATT_TILE = 2 * MXU_DIM
OUT_TILE = 4 * MXU_DIM
SHORT_TILE = 2 * SUBLANES


def _cparams(n_grid):
    return pltpu.CompilerParams(dimension_semantics=("arbitrary",) * n_grid,
                                vmem_limit_bytes=VMEM_LIMIT)


def _nt(a, b):
    return lax.dot_general(a, b, (((1,), (1,)), ((), ())), preferred_element_type=F32)


def _sigmoid(x):
    return 0.5 * jnp.tanh(0.5 * x) + 0.5


def _sqrt_nonneg(x):
    return x * lax.rsqrt(jnp.maximum(x, F32_TINY))


def _layer_norm(v, g, b):
    mu = jnp.mean(v, axis=-1, keepdims=True)
    d = v - mu
    var = jnp.mean(d * d, axis=-1, keepdims=True)
    return d * lax.rsqrt(var + EPS) * g + b


def _silu(x):
    return x * _sigmoid(x)


def _softplus(x):
    return jnp.maximum(x, 0.0) + jnp.log(1.0 + jnp.exp(-jnp.abs(x)))


def _proj_kernel(x_ref, w_ref, *out_refs, splits):
    x = x_ref[...].astype(BF16)
    for (off, width), o_ref in zip(splits, out_refs):
        o_ref[...] = jnp.dot(x, w_ref[:, off:off + width], preferred_element_type=F32)


def proj_split(x2d, w_bf16, splits, tm):
    rows, kdim = x2d.shape
    ndim = w_bf16.shape[1]
    return pl.pallas_call(
        functools.partial(_proj_kernel, splits=splits),
        grid=(rows // tm,),
        in_specs=[pl.BlockSpec((tm, kdim), lambda i: (i, 0)),
                  pl.BlockSpec((kdim, ndim), lambda i: (0, 0))],
        out_specs=[pl.BlockSpec((tm, wd), lambda i: (i, 0)) for _, wd in splits],
        out_shape=[jax.ShapeDtypeStruct((rows, wd), F32) for _, wd in splits],
        compiler_params=_cparams(1),
        name="proj_split",
    )(x2d, w_bf16)


def _outproj_ln_kernel(y_ref, w_ref, x_ref, g_ref, b_ref, o_ref):
    f = jnp.dot(y_ref[...].astype(BF16), w_ref[...], preferred_element_type=F32)
    o_ref[...] = _layer_norm(DN_ALPHA * x_ref[...] + f, g_ref[...], b_ref[...])


def outproj_ln(y2d, w_bf16, x2d, g, b, tm):
    rows, kdim = y2d.shape
    return pl.pallas_call(
        _outproj_ln_kernel,
        grid=(rows // tm,),
        in_specs=[pl.BlockSpec((tm, kdim), lambda i: (i, 0)),
                  pl.BlockSpec((kdim, D_MODEL), lambda i: (0, 0)),
                  pl.BlockSpec((tm, D_MODEL), lambda i: (i, 0)),
                  pl.BlockSpec((1, D_MODEL), lambda i: (0, 0)),
                  pl.BlockSpec((1, D_MODEL), lambda i: (0, 0))],
        out_specs=pl.BlockSpec((tm, D_MODEL), lambda i: (i, 0)),
        out_shape=jax.ShapeDtypeStruct((rows, D_MODEL), F32),
        compiler_params=_cparams(1),
        name="outproj_ln",
    )(y2d, w_bf16, x2d, g.reshape(1, D_MODEL), b.reshape(1, D_MODEL))


def _rglru_core(ub, gate, a_s, b_s, h_in, gate_refs, *, tt, unroll):
    cw_ref, cb_ref, wr_ref, rb_ref, wi_ref, ib_ref, lam_ref = gate_refs
    cw = cw_ref[...]
    ext = ub[...]
    y = cb_ref[...] + cw[3:4, :] * ext[HIST:, :]
    for k in range(1, CONV_W):
        y = y + cw[3 - k:4 - k, :] * pltpu.roll(ext, k, 0)[HIST:, :]

    c = (-0.5 * LRU_C) * _softplus(-lam_ref[...])
    yb = y.astype(BF16)
    for j in range(LRU_W // MXU_DIM):
        sl = slice(j * MXU_DIM, (j + 1) * MXU_DIM)
        ys = yb[:, sl]
        t_r = jnp.tanh(jnp.dot(ys, wr_ref[j], preferred_element_type=F32) + rb_ref[:, sl])
        t_i = jnp.tanh(jnp.dot(ys, wi_ref[j], preferred_element_type=F32) + ib_ref[:, sl])
        log_a = c[:, sl] * t_r + c[:, sl]
        a = jnp.exp(log_a)
        a_s[:, sl] = a
        half_y = 0.5 * y[:, sl]
        b_s[:, sl] = (_sqrt_nonneg(-jnp.tanh(log_a) * (a * a + 1.0))
                      * (half_y * t_i + half_y))

    row = lax.broadcasted_iota(jnp.int32, (SUBLANES, LRU_W), 0)

    def group(r0, hc):
        a = a_s[pl.ds(r0, SUBLANES), :]
        b = b_s[pl.ds(r0, SUBLANES), :]
        for d in (1, 2, 4):
            a_sh = pltpu.roll(a, d, 0)
            b_sh = pltpu.roll(b, d, 0)
            m = row >= d
            b = jnp.where(m, a * b_sh + b, b)
            a = jnp.where(m, a * a_sh, a)
        h = a * hc + b
        b_s[pl.ds(r0, SUBLANES), :] = h
        return h[SUBLANES - 1:SUBLANES, :]

    if unroll:
        hc = h_in
        for g in range(tt // SUBLANES):
            hc = group(g * SUBLANES, hc)
    else:
        lax.fori_loop(0, tt // SUBLANES,
                      lambda g, hc: group(pl.multiple_of(g * SUBLANES, SUBLANES), hc), h_in)
    return b_s[...] * _silu(gate[...])


def _rglru_core_kernel(gate_ref, u_ref, buf_ref, h0_ref, *refs, tt, valid):
    gate_refs = refs[:7]
    out_ref, nbuf_ref, hlast_ref, ubuf, g_s, a_s, b_s = refs[7:]
    ubuf[0:HIST - 3, :] = jnp.zeros((HIST - 3, LRU_W), F32)
    ubuf[HIST - 3:HIST, :] = buf_ref[0]
    ubuf[HIST:HIST + valid, :] = u_ref[0]
    ubuf[HIST + valid:HIST + tt, :] = jnp.zeros((tt - valid, LRU_W), F32)
    g_s[0:valid, :] = gate_ref[0]
    g_s[valid:tt, :] = jnp.zeros((tt - valid, LRU_W), F32)
    hg = _rglru_core(ubuf, g_s, a_s, b_s, h0_ref[0], gate_refs, tt=tt, unroll=False)
    out_ref[0] = hg[0:valid, :]
    nbuf_ref[0] = ubuf[HIST + valid - 3:HIST + valid, :]
    hlast_ref[0] = b_s[valid - 1:valid, :]


def _rglru_layer_kernel(xa_ref, xb_ref, xn_ref, buf_ref, h0_ref, win_ref, *refs, tt, n_steps):
    gate_refs = refs[:7]
    wout_ref, lng_ref, lnb_ref, out_ref, nbuf_ref, hlast_ref, ubuf, g_s, a_s, b_s, hcar = refs[7:]
    s = pl.program_id(0)
    first = s % n_steps == 0

    def project(x_ref, slot):
        xb = x_ref[0].astype(BF16)
        g_s[slot] = jnp.dot(xb, win_ref[:, 0:LRU_W], preferred_element_type=F32)
        ubuf[slot, HIST:HIST + tt, :] = jnp.dot(xb, win_ref[:, LRU_W:2 * LRU_W],
                                                preferred_element_type=F32)

    def finish(x_ref, hg):
        f = jnp.dot(hg.astype(BF16), wout_ref[...], preferred_element_type=F32)
        return _layer_norm(DN_ALPHA * x_ref[0] + f, lng_ref[...], lnb_ref[...])

    @pl.when(s == 0)
    def _():
        for slot in range(2):
            ubuf[slot, 0:HIST - 3, :] = jnp.zeros((HIST - 3, LRU_W), F32)
        project(xa_ref, 0)

    @pl.when(first)
    def _():
        ubuf[0, HIST - 3:HIST, :] = buf_ref[0]
        hcar[...] = h0_ref[0]

    @pl.when(jnp.logical_not(first))
    def _():
        ubuf[0, HIST - 3:HIST, :] = ubuf[1, HIST + tt - 3:HIST + tt, :]

    project(xb_ref, 1)
    hg = _rglru_core(ubuf.at[0], g_s.at[0], a_s, b_s, hcar[...], gate_refs, tt=tt, unroll=True)
    h_a = b_s[tt - 1:tt, :]
    out_ref[0, 0:tt, :] = finish(xa_ref, hg)
    ubuf[1, HIST - 3:HIST, :] = ubuf[0, HIST + tt - 3:HIST + tt, :]
    project(xn_ref, 0)
    hg = _rglru_core(ubuf.at[1], g_s.at[1], a_s, b_s, h_a, gate_refs, tt=tt, unroll=True)
    h_b = b_s[tt - 1:tt, :]
    hcar[...] = h_b
    out_ref[0, tt:2 * tt, :] = finish(xb_ref, hg)

    @pl.when(s % n_steps == n_steps - 1)
    def _():
        nbuf_ref[0] = ubuf[1, HIST + tt - 3:HIST + tt, :]
        hlast_ref[0] = h_b


def rglru_layer(x, conv_buf, h0, w, ln_g, ln_b):
    in_w, conv_w, conv_b, wr_bd, r_b, wi_bd, i_b, lam, out_w = w
    bsz, t_len, _ = x.shape
    row = lambda v: v.reshape(1, LRU_W)
    gate_args = (conv_w, row(conv_b), wr_bd, row(r_b), wi_bd, row(i_b), row(lam))
    h0r = h0.reshape(bsz, 1, LRU_W)
    state_shapes = [jax.ShapeDtypeStruct((bsz, CONV_W - 1, LRU_W), F32),
                    jax.ShapeDtypeStruct((bsz, 1, LRU_W), F32)]
    tt = ROW_TILE
    if t_len % (2 * tt) == 0:
        n_steps = t_len // (2 * tt)
        n_tiles = bsz * t_len // tt
        const = lambda shape: pl.BlockSpec(shape, lambda s: (0,) * len(shape))
        x_tile = lambda off: pl.BlockSpec(
            (1, tt, D_MODEL), lambda s: (jnp.minimum(2 * s + off, n_tiles - 1), 0, 0))
        per_seq = lambda r: pl.BlockSpec((1, r, LRU_W), lambda s: (s // n_steps, 0, 0))
        gate_specs = [const((CONV_W, LRU_W)), const((1, LRU_W)),
                      const((LRU_W // MXU_DIM, MXU_DIM, MXU_DIM)), const((1, LRU_W)),
                      const((LRU_W // MXU_DIM, MXU_DIM, MXU_DIM)), const((1, LRU_W)),
                      const((1, LRU_W))]
        xt = x.reshape(n_tiles, tt, D_MODEL)
        out, nbuf, hlast = pl.pallas_call(
            functools.partial(_rglru_layer_kernel, tt=tt, n_steps=n_steps),
            grid=(bsz * n_steps,),
            in_specs=([x_tile(0), x_tile(1), x_tile(2), per_seq(CONV_W - 1), per_seq(1),
                       const((D_MODEL, 2 * LRU_W))] + gate_specs
                      + [const((LRU_W, D_MODEL)), const((1, D_MODEL)), const((1, D_MODEL))]),
            out_specs=[pl.BlockSpec((1, 2 * tt, D_MODEL), lambda s: (s, 0, 0)),
                       per_seq(CONV_W - 1), per_seq(1)],
            out_shape=[jax.ShapeDtypeStruct((bsz * n_steps, 2 * tt, D_MODEL), F32)] + state_shapes,
            scratch_shapes=[pltpu.VMEM((2, HIST + tt, LRU_W), F32),
                            pltpu.VMEM((2, tt, LRU_W), F32),
                            pltpu.VMEM((tt, LRU_W), F32),
                            pltpu.VMEM((tt, LRU_W), F32),
                            pltpu.VMEM((1, LRU_W), F32)],
            compiler_params=_cparams(1),
            name="rglru_layer",
        )(xt, xt, xt, conv_buf, h0r, in_w, *gate_args, out_w, row(ln_g), row(ln_b))
        return out.reshape(bsz, t_len, D_MODEL), nbuf, hlast.reshape(bsz, LRU_W)

    tt, valid = SHORT_TILE, t_len
    assert t_len <= tt
    x2 = x.reshape(bsz * t_len, D_MODEL)
    gate, u = proj_split(x2, in_w, ((0, LRU_W), (LRU_W, LRU_W)), _row_tile(bsz * t_len))
    vec = lambda: pl.BlockSpec((1, LRU_W), lambda b: (0, 0))
    wspec = lambda: pl.BlockSpec((LRU_W // MXU_DIM, MXU_DIM, MXU_DIM), lambda b: (0, 0, 0))
    per_seq = lambda r: pl.BlockSpec((1, r, LRU_W), lambda b: (b, 0, 0))
    hg, nbuf, hlast = pl.pallas_call(
        functools.partial(_rglru_core_kernel, tt=tt, valid=valid),
        grid=(bsz,),
        in_specs=[per_seq(valid), per_seq(valid), per_seq(CONV_W - 1), per_seq(1),
                  pl.BlockSpec((CONV_W, LRU_W), lambda b: (0, 0)),
                  vec(), wspec(), vec(), wspec(), vec(), vec()],
        out_specs=[per_seq(valid), per_seq(CONV_W - 1), per_seq(1)],
        out_shape=[jax.ShapeDtypeStruct((bsz, t_len, LRU_W), F32)] + state_shapes,
        scratch_shapes=[pltpu.VMEM((HIST + tt, LRU_W), F32),
                        pltpu.VMEM((tt, LRU_W), F32),
                        pltpu.VMEM((tt, LRU_W), F32),
                        pltpu.VMEM((tt, LRU_W), F32)],
        compiler_params=_cparams(1),
        name="rglru_core",
    )(gate.reshape(bsz, t_len, LRU_W), u.reshape(bsz, t_len, LRU_W), conv_buf, h0r, *gate_args)
    out = outproj_ln(hg.reshape(bsz * t_len, LRU_W), out_w, x2, ln_g, ln_b,
                     _row_tile(bsz * t_len)).reshape(bsz, t_len, D_MODEL)
    return out, nbuf, hlast.reshape(bsz, LRU_W)


def _block_diag_tiles(w):
    per = MXU_DIM // LRU_BS
    w4 = w.reshape(LRU_W // MXU_DIM, per, LRU_BS, LRU_BS)
    eye = jnp.eye(per, dtype=w.dtype)
    t = jnp.einsum("jakc,ab->jakbc", w4, eye)
    return t.reshape(LRU_W // MXU_DIM, MXU_DIM, MXU_DIM).astype(BF16)


def _rglru_weights(in_w, conv_w, conv_b, r_w, r_b, i_w, i_b, lam, out_w):
    return (in_w.astype(BF16), conv_w, conv_b, _block_diag_tiles(0.5 * r_w), 0.5 * r_b,
            _block_diag_tiles(0.5 * i_w), 0.5 * i_b, lam, out_w.astype(BF16))


def _diff_lambda(lp, lam_init):
    s1 = jnp.sum(lp[0:1, :] * lp[1:2, :], axis=-1, keepdims=True)
    s2 = jnp.sum(lp[2:3, :] * lp[3:4, :], axis=-1, keepdims=True)
    return jnp.exp(s1) - jnp.exp(s2) + lam_init


def _subln_gate(o, sw, gate, lam_init):
    ms = jnp.mean(o * o, axis=-1, keepdims=True)
    return (o * lax.rsqrt(ms + EPS) * sw * (1.0 - lam_init)) * _silu(gate)


def _attn_prompt_kernel(lp_ref, q_ref, kt_ref, v_ref, gate_ref, sw_ref, o_ref,
                        kb_s, vx_s, m_s, accl_s, s_s, *, tq, lam_init):
    qi = pl.program_id(2)
    hd = 2 * ATT_DH
    t_len = v_ref.shape[1]

    @pl.when(qi == 0)
    def _():
        for i in range(t_len // tq):
            kb_s[i] = kt_ref[0, :, i * tq:(i + 1) * tq].astype(BF16)

        def cvt(i, carry):
            r = pl.multiple_of(i * tq, tq)
            vx_s[pl.ds(r, tq), 0:hd] = v_ref[0, pl.ds(r, tq), :].astype(BF16)
            vx_s[pl.ds(r, tq), hd:2 * hd] = jnp.ones((tq, hd), BF16)
            return carry
        lax.fori_loop(0, t_len // tq, cvt, 0)

    lam = _diff_lambda(lp_ref[...], lam_init)
    q = q_ref[0] * (ATT_DH ** -0.5 * LOG2E)
    lane = lax.broadcasted_iota(jnp.int32, (tq, hd), 1)
    q_maps = (jnp.where(lane < ATT_DH, q, 0.0).astype(BF16),
              jnp.where(lane >= ATT_DH, q, 0.0).astype(BF16))
    m_s[...] = jnp.full(m_s.shape, NEG_BIG, F32)
    accl_s[...] = jnp.zeros(accl_s.shape, F32)
    rowi = lax.broadcasted_iota(jnp.int32, (tq, tq), 0)
    coli = lax.broadcasted_iota(jnp.int32, (tq, tq), 1)

    def scores(ki, slot):
        kb = kb_s[ki]
        for c in range(2):
            s_s[slot, c] = jnp.dot(q_maps[c], kb, preferred_element_type=F32)

    def absorb(ki, slot, masked):
        ks = pl.multiple_of(ki * tq, tq)
        vx = vx_s[pl.ds(ks, tq), :]
        for c in range(2):
            s = s_s[slot, c]
            if masked:
                s = jnp.where(coli <= rowi, s, NEG_BIG)
            m_prev = m_s[c]
            m_new = jnp.maximum(m_prev, jnp.max(s, axis=-1, keepdims=True))
            alpha = jnp.exp2(m_prev - m_new)
            p = jnp.exp2(s - jnp.concatenate([m_new] * (tq // hd), axis=1))
            pv = jnp.dot(p.astype(BF16), vx, preferred_element_type=F32)
            accl_s[c] = jnp.concatenate([alpha, alpha], axis=1) * accl_s[c] + pv
            m_s[c] = m_new

    scores(0, 0)

    def pair(j, carry):
        k0 = 2 * j
        scores(k0 + 1, 1)
        absorb(k0, 0, False)
        scores(k0 + 2, 0)
        absorb(k0 + 1, 1, False)
        return carry

    lax.fori_loop(0, qi // 2, pair, 0)

    @pl.when(qi % 2 == 0)
    def _():
        absorb(qi, 0, True)

    @pl.when(qi % 2 == 1)
    def _():
        scores(qi, 1)
        absorb(qi - 1, 0, False)
        absorb(qi, 1, True)

    a1 = accl_s[0]
    a2 = accl_s[1]
    o = a1[:, 0:hd] / a1[:, hd:2 * hd] - lam * (a2[:, 0:hd] / a2[:, hd:2 * hd])
    o_ref[0] = _subln_gate(o, sw_ref[...], gate_ref[0], lam_init)


def _attn_sample_parts(lp_ref, qbd_ref, *refs, n_pg, t_new, lam_init):
    k_refs = refs[:n_pg]
    v_refs = refs[n_pg:2 * n_pg]
    knew_ref, vnew_ref, gate_ref, sw_ref, o_ref, m_s, l_s, acc_s = refs[2 * n_pg:]
    j = pl.program_id(2)
    n_j = pl.num_programs(2)
    n_rows = ATT_HEADS * 2 * t_new
    hd = 2 * ATT_DH

    def start():
        @pl.when(j == 0)
        def _():
            m_s[...] = jnp.full(m_s.shape, NEG_BIG, F32)
            l_s[...] = jnp.zeros(l_s.shape, F32)
            acc_s[...] = jnp.zeros(acc_s.shape, F32)

    def attend(kts, vbs, mask):
        qbd = qbd_ref[0]
        n = len(kts)
        s = jnp.concatenate([jnp.dot(qbd, kt.astype(BF16), preferred_element_type=F32)
                             for kt in kts], axis=1)
        if mask is not None:
            s = jnp.where(mask, s, NEG_BIG)
        m_prev = m_s[...]
        m_new = jnp.maximum(m_prev, jnp.max(s, axis=-1, keepdims=True))
        alpha = jnp.exp2(m_prev - m_new)
        p = jnp.exp2(s - jnp.concatenate([m_new] * n, axis=1))
        l_s[...] = alpha * l_s[...] + jnp.sum(p, axis=-1, keepdims=True)
        pb = p.astype(BF16)
        pv = jnp.dot(pb[:, 0:PAGE_SIZE], vbs[0].astype(BF16), preferred_element_type=F32)
        for i in range(1, n):
            pv = pv + jnp.dot(pb[:, i * PAGE_SIZE:(i + 1) * PAGE_SIZE], vbs[i].astype(BF16),
                              preferred_element_type=F32)
        acc_s[...] = jnp.concatenate([alpha] * (ATT_W // LANES), axis=1) * acc_s[...] + pv
        m_s[...] = m_new

    def page_v(vr):
        return jnp.concatenate([vr[0, pl.ds(h, PAGE_SIZE, stride=ATT_HEADS), :]
                                for h in range(ATT_HEADS)], axis=1)

    def pages():
        attend([r[0] for r in k_refs], [page_v(r) for r in v_refs], None)

    def finish():
        @pl.when(j == n_j - 1)
        def _():
            rowi = lax.broadcasted_iota(jnp.int32, (n_rows, PAGE_SIZE), 0)
            coli = lax.broadcasted_iota(jnp.int32, (n_rows, PAGE_SIZE), 1)
            attend([knew_ref[0]], [vnew_ref[0]], coli <= (rowi % t_new))
            lam = _diff_lambda(lp_ref[...], lam_init)
            sw = sw_ref[...]
            for h in range(ATT_HEADS):
                r1 = h * 2 * t_new
                r2 = r1 + t_new
                cs = slice(h * hd, (h + 1) * hd)
                o1 = acc_s[r1:r1 + t_new, cs] / l_s[r1:r1 + t_new, :]
                o2 = acc_s[r2:r2 + t_new, cs] / l_s[r2:r2 + t_new, :]
                o_ref[0, :, cs] = _subln_gate(o1 - lam * o2, sw, gate_ref[0, :, cs], lam_init)

    return start, pages, finish


def _attn_kernel(pt_ref, lp_ref, q_ref, kt_ref, v_ref, gate_ref, sw_ref, qbd_ref, *refs,
                 n_pg, tq, t_new, lam_init):
    del pt_ref
    sample_in = refs[:2 * n_pg + 3]
    o_ref, os_ref, kb_s, vx_s, m_s, accl_s, s_s, ms_s, ls_s, accs_s = refs[2 * n_pg + 3:]
    start, pages, finish = _attn_sample_parts(
        lp_ref, qbd_ref, *sample_in, sw_ref, os_ref, ms_s, ls_s, accs_s,
        n_pg=n_pg, t_new=t_new, lam_init=lam_init)
    _attn_prompt_kernel(lp_ref, q_ref, kt_ref, v_ref, gate_ref, sw_ref, o_ref,
                        kb_s, vx_s, m_s, accl_s, s_s, tq=tq, lam_init=lam_init)
    start()
    pages()
    finish()


def attn_core(q, kt, v, gate, qs, ks, vs, gate_s, cache_k, cache_v, layer, page_table, lam_p,
              subln_w, lam_init, tq=ATT_TILE):
    bsz_p, t_len, _ = q.shape
    bsz, t_new, _ = qs.shape
    n_pages = page_table.shape[1]
    n_pool = cache_k.shape[1]
    hd = 2 * ATT_DH
    n_rows = ATT_HEADS * 2 * t_new
    n_q = t_len // tq
    assert bsz == bsz_p * ATT_HEADS and n_pages % n_q == 0
    n_pg = n_pages // n_q
    q4 = (qs * (ATT_DH ** -0.5 * LOG2E)).reshape(bsz, t_new, 2 * ATT_HEADS, ATT_DH)
    eye = jnp.eye(2 * ATT_HEADS, dtype=F32)
    qbd = jnp.einsum("bqhd,hg->bhqgd", q4, eye).reshape(bsz, n_rows, ATT_W).astype(BF16)
    k_new = jnp.pad(jnp.swapaxes(ks, 1, 2), ((0, 0), (0, 0), (0, PAGE_SIZE - t_new)))
    v_new = jnp.pad(vs, ((0, 0), (0, PAGE_SIZE - t_new), (0, 0)))
    ck = jnp.transpose(cache_k, (0, 1, 3, 4, 5, 2)).reshape(-1, ATT_W, PAGE_SIZE)
    cv = cache_v.reshape(-1, PAGE_SIZE * ATT_HEADS, hd)
    pt = page_table.reshape(-1) + layer * n_pool

    sample = lambda b, h: b * ATT_HEADS + h

    def page_spec(i, rows, width):
        return pl.BlockSpec(
            (1, rows, width),
            lambda b, h, j, pt_ref: (pt_ref[sample(b, h) * n_pages + j * n_pg + i], 0, 0))

    const = lambda shape: pl.BlockSpec(shape, lambda b, h, j, pt_ref: (0,) * len(shape))
    qspec = lambda: pl.BlockSpec((1, tq, hd), lambda b, h, j, pt_ref: (b, j, h))
    per_s = lambda r, w: pl.BlockSpec((1, r, w), lambda b, h, j, pt_ref: (sample(b, h), 0, 0))
    grid_spec = pltpu.PrefetchScalarGridSpec(
        num_scalar_prefetch=1,
        grid=(bsz_p, ATT_HEADS, n_q),
        in_specs=([const((4, ATT_DH)), qspec(),
                   pl.BlockSpec((1, hd, t_len), lambda b, h, j, pt_ref: (b, h, 0)),
                   pl.BlockSpec((1, t_len, hd), lambda b, h, j, pt_ref: (b, 0, h)),
                   qspec(), const((1, hd)), per_s(n_rows, ATT_W)]
                  + [page_spec(i, ATT_W, PAGE_SIZE) for i in range(n_pg)]
                  + [page_spec(i, PAGE_SIZE * ATT_HEADS, hd) for i in range(n_pg)]
                  + [per_s(ATT_W, PAGE_SIZE), per_s(PAGE_SIZE, ATT_W), per_s(t_new, ATT_W)]),
        out_specs=[qspec(), per_s(t_new, ATT_W)],
        scratch_shapes=[pltpu.VMEM((n_q, hd, tq), BF16),
                        pltpu.VMEM((t_len, 2 * hd), BF16),
                        pltpu.VMEM((2, tq, hd), F32), pltpu.VMEM((2, tq, 2 * hd), F32),
                        pltpu.VMEM((2, 2, tq, tq), F32),
                        pltpu.VMEM((n_rows, LANES), F32), pltpu.VMEM((n_rows, LANES), F32),
                        pltpu.VMEM((n_rows, ATT_W), F32)],
    )
    return pl.pallas_call(
        functools.partial(_attn_kernel, n_pg=n_pg, tq=tq, t_new=t_new, lam_init=lam_init),
        grid_spec=grid_spec,
        out_shape=[jax.ShapeDtypeStruct((bsz_p, t_len, ATT_W), F32),
                   jax.ShapeDtypeStruct((bsz, t_new, ATT_W), F32)],
        compiler_params=_cparams(3),
        name="attn_core",
    )(pt, lam_p, q, kt, v, gate, subln_w.reshape(1, hd), qbd, *([ck] * n_pg), *([cv] * n_pg),
      k_new, v_new, gate_s)


def _ssd_core(ub, z, dt_raw, hout_ref, y_s, scan_refs, *, vl, L=SSD_CHUNK, after_group=None):
    cw_ref, cb_ref, dtb_ref, alog_ref, dexp_ref, nw_ref = scan_refs
    gw = SSD_HPG * SSD_HEADDIM
    rowL = lax.broadcasted_iota(jnp.int32, (L, LANES), 0)
    laneL = lax.broadcasted_iota(jnp.int32, (L, LANES), 1)
    dt = _softplus(dt_raw + dtb_ref[...])
    dt = jnp.where((rowL < vl) & (laneL < SSD_HEADS), dt, 0.0)
    adt = dt * (-jnp.exp(alog_ref[...]))
    tri = (lax.broadcasted_iota(jnp.int32, (L, L), 0)
           >= lax.broadcasted_iota(jnp.int32, (L, L), 1))
    cs = jnp.dot(tri.astype(F32), adt, preferred_element_type=F32,
                 precision=lax.Precision.HIGHEST)
    cs_last = cs[L - 1:L, :]
    w1 = dt * jnp.exp(cs_last - cs)
    cs_t = cs.T
    dt_t = dt.T
    w1_t = w1.T

    cw = cw_ref[...]
    lane_g = lax.broadcasted_iota(jnp.int32, (L, gw), 1)

    def conv_silu(lo, width):
        sl = slice(lo, lo + width)
        y = cb_ref[:, sl] + cw[3:4, sl] * ub[HIST:HIST + L, sl]
        for k in range(1, CONV_W):
            y = y + cw[3 - k:4 - k, sl] * ub[HIST - k:HIST - k + L, sl]
        return _silu(y)

    for g in range(SSD_GROUPS):
        xh = conv_silu(g * gw, gw)
        bm = conv_silu(SSD_INNER + g * SSD_STATE, SSD_STATE)
        cm = conv_silu(SSD_INNER + (SSD_GROUPS + g) * SSD_STATE, SSD_STATE)
        bmb = bm.astype(BF16)
        cmb = cm.astype(BF16)
        xhb = xh.astype(BF16)
        cbm = _nt(cmb, bmb)
        y_diag = jnp.zeros((L, gw), F32)
        e_cols = []
        w_rows = []
        d_rows = []
        for r in range(SSD_HPG):
            h = g * SSD_HPG + r
            cs_col = cs[:, h:h + 1]
            cs_row = cs_t[h:h + 1, :]
            lm = jnp.exp(jnp.where(tri, cs_col - cs_row, NEG_BIG))
            mat = (cbm * lm * dt_t[h:h + 1, :]).astype(BF16)
            in_head = (lane_g >= r * SSD_HEADDIM) & (lane_g < (r + 1) * SSD_HEADDIM)
            xr = jnp.where(in_head, xhb, jnp.zeros_like(xhb))
            y_diag = y_diag + jnp.dot(mat, xr, preferred_element_type=F32)
            e_cols.append(jnp.broadcast_to(jnp.exp(cs_col), (L, SSD_HEADDIM)))
            w_rows.append(jnp.broadcast_to(w1_t[h:h + 1, :], (SSD_HEADDIM, L)))
            d_rows.append(jnp.broadcast_to(jnp.exp(cs_t[h:h + 1, L - 1:L]),
                                           (SSD_HEADDIM, SSD_STATE)))
        hg = hout_ref[0, g]
        y_off = _nt(cmb, hg.astype(BF16)) * jnp.concatenate(e_cols, axis=1)
        y_s[:, g * gw:(g + 1) * gw] = y_diag + y_off + dexp_ref[:, g * gw:(g + 1) * gw] * xh
        xd_t = (xh.T * jnp.concatenate(w_rows, axis=0)).astype(BF16)
        states = jnp.dot(xd_t, bmb, preferred_element_type=F32)
        hout_ref[0, g] = hg * jnp.concatenate(d_rows, axis=0) + states
        if after_group is not None:
            after_group(g)

    gated = y_s[...] * _silu(z[...])
    ms = jnp.mean(gated * gated, axis=-1, keepdims=True)
    return gated * lax.rsqrt(ms + EPS) * nw_ref[...]


def _ssd_core_kernel(xbc_ref, z_ref, dtr_ref, buf_ref, h0_ref, *refs, vl, L):
    scan_refs = refs[:6]
    out_ref, nbuf_ref, hout_ref, ubuf, z_s, dt_s, y_s = refs[6:]
    ubuf[HIST - 3:HIST, :] = buf_ref[0]
    ubuf[HIST:HIST + vl, :] = xbc_ref[0]
    ubuf[HIST + vl:HIST + L, :] = jnp.zeros((L - vl, SSD_CONV_DIM), F32)
    z_s[0:vl, :] = z_ref[0]
    z_s[vl:L, :] = jnp.zeros((L - vl, SSD_INNER), F32)
    dt_s[0:vl, :] = dtr_ref[0]
    dt_s[vl:L, :] = jnp.zeros((L - vl, LANES), F32)
    hout_ref[0] = h0_ref[0]
    yn = _ssd_core(ubuf, z_s, dt_s[...], hout_ref, y_s, scan_refs, vl=vl, L=L)
    out_ref[0] = yn[0:vl, :]
    nbuf_ref[0] = ubuf[HIST + vl - 3:HIST + vl, :]


def _ssd_layer_kernel(xa_ref, xb_ref, xn_ref, buf_ref, h0_ref, win_ref, *refs, n_steps):
    scan_refs = refs[:6]
    (wout_ref, lng_ref, lnb_ref, out_ref, nbuf_ref, hout_ref, ubuf, z_s, dt_s, y_s) = refs[6:]
    L = SSD_CHUNK
    s = pl.program_id(0)
    first = s % n_steps == 0
    xbc_lo = SSD_INNER
    dt_lo = SSD_INNER + SSD_CONV_DIM

    n_sl = SSD_GROUPS
    zw = SSD_INNER // n_sl
    cwid = SSD_CONV_DIM // n_sl

    def project_slice(xb, slot, i):
        z_s[slot, :, i * zw:(i + 1) * zw] = jnp.dot(
            xb, win_ref[:, i * zw:(i + 1) * zw], preferred_element_type=F32)
        ubuf[slot, HIST:HIST + L, i * cwid:(i + 1) * cwid] = jnp.dot(
            xb, win_ref[:, xbc_lo + i * cwid:xbc_lo + (i + 1) * cwid],
            preferred_element_type=F32)
        if i == 0:
            dt_s[slot] = jnp.dot(xb, win_ref[:, dt_lo:dt_lo + LANES],
                                 preferred_element_type=F32)

    def finish(x_ref, yn):
        f = jnp.dot(yn.astype(BF16), wout_ref[...], preferred_element_type=F32)
        return _layer_norm(DN_ALPHA * x_ref[0] + f, lng_ref[...], lnb_ref[...])

    @pl.when(s == 0)
    def _():
        xb = xa_ref[0].astype(BF16)
        for i in range(n_sl):
            project_slice(xb, 0, i)

    @pl.when(first)
    def _():
        ubuf[0, HIST - 3:HIST, :] = buf_ref[0]
        hout_ref[0] = h0_ref[0]

    @pl.when(jnp.logical_not(first))
    def _():
        ubuf[0, HIST - 3:HIST, :] = ubuf[1, HIST + L - 3:HIST + L, :]

    xb_b = xb_ref[0].astype(BF16)
    yn = _ssd_core(ubuf.at[0], z_s.at[0], dt_s[0], hout_ref, y_s, scan_refs, vl=L,
                   after_group=lambda g: project_slice(xb_b, 1, g))
    out_ref[0, 0:L, :] = finish(xa_ref, yn)
    ubuf[1, HIST - 3:HIST, :] = ubuf[0, HIST + L - 3:HIST + L, :]
    xb_n = xn_ref[0].astype(BF16)
    yn = _ssd_core(ubuf.at[1], z_s.at[1], dt_s[1], hout_ref, y_s, scan_refs, vl=L,
                   after_group=lambda g: project_slice(xb_n, 0, g))
    out_ref[0, L:2 * L, :] = finish(xb_ref, yn)

    @pl.when(s % n_steps == n_steps - 1)
    def _():
        nbuf_ref[0] = ubuf[1, HIST + L - 3:HIST + L, :]


def ssd_layer(x, conv_buf, h0, w, ln_g, ln_b):
    in_w_pad, conv_w, conv_b, dt_bias, a_log, d_skip, norm_w, out_w = w
    bsz, t_len, _ = x.shape
    L = SSD_CHUNK
    gw = SSD_HPG * SSD_HEADDIM
    n_proj = in_w_pad.shape[1]
    pad_lanes = lambda v: jnp.pad(v.reshape(1, SSD_HEADS), ((0, 0), (0, LANES - SSD_HEADS)))
    dexp = jnp.repeat(d_skip, SSD_HEADDIM).reshape(1, SSD_INNER)
    h0g = h0.reshape(bsz, SSD_GROUPS, gw, SSD_STATE)
    scan_args = (conv_w, conv_b.reshape(1, SSD_CONV_DIM), pad_lanes(dt_bias), pad_lanes(a_log),
                 dexp, norm_w.reshape(1, SSD_INNER))
    scan_shapes = [(CONV_W, SSD_CONV_DIM), (1, SSD_CONV_DIM), (1, LANES), (1, LANES),
                   (1, SSD_INNER), (1, SSD_INNER)]
    state_shapes = [jax.ShapeDtypeStruct((bsz, CONV_W - 1, SSD_CONV_DIM), F32),
                    jax.ShapeDtypeStruct((bsz, SSD_GROUPS, gw, SSD_STATE), F32)]
    if t_len % (2 * L) == 0:
        n_steps = t_len // (2 * L)
        n_chunks = bsz * t_len // L
        const = lambda shape: pl.BlockSpec(shape, lambda s: (0,) * len(shape))
        resident = lambda shape: pl.BlockSpec(shape, lambda s: (0,) * len(shape),
                                              pipeline_mode=pl.Buffered(1))
        x_chunk = lambda off: pl.BlockSpec(
            (1, L, D_MODEL), lambda s: (jnp.minimum(2 * s + off, n_chunks - 1), 0, 0))
        state_specs = [pl.BlockSpec((1, CONV_W - 1, SSD_CONV_DIM), lambda s: (s // n_steps, 0, 0)),
                       pl.BlockSpec((1, SSD_GROUPS, gw, SSD_STATE),
                                    lambda s: (s // n_steps, 0, 0, 0))]
        xt = x.reshape(n_chunks, L, D_MODEL)
        out, nbuf, hout = pl.pallas_call(
            functools.partial(_ssd_layer_kernel, n_steps=n_steps),
            grid=(bsz * n_steps,),
            in_specs=([x_chunk(0), x_chunk(1), x_chunk(2)] + state_specs
                      + [resident((D_MODEL, n_proj))] + [const(sh) for sh in scan_shapes]
                      + [resident((SSD_INNER, D_MODEL)), const((1, D_MODEL)), const((1, D_MODEL))]),
            out_specs=[pl.BlockSpec((1, 2 * L, D_MODEL), lambda s: (s, 0, 0))] + state_specs,
            out_shape=[jax.ShapeDtypeStruct((bsz * n_steps, 2 * L, D_MODEL), F32)] + state_shapes,
            scratch_shapes=[pltpu.VMEM((2, HIST + L, SSD_CONV_DIM), F32),
                            pltpu.VMEM((2, L, SSD_INNER), F32),
                            pltpu.VMEM((2, L, LANES), F32),
                            pltpu.VMEM((L, SSD_INNER), F32)],
            compiler_params=_cparams(1),
            name="ssd_layer",
        )(xt, xt, xt, conv_buf, h0g, in_w_pad, *scan_args, out_w,
          ln_g.reshape(1, D_MODEL), ln_b.reshape(1, D_MODEL))
        return (out.reshape(bsz, t_len, D_MODEL), nbuf,
                hout.reshape(bsz, SSD_HEADS, SSD_HEADDIM, SSD_STATE))

    vl, L = t_len, SHORT_TILE
    assert vl <= L
    x2 = x.reshape(bsz * t_len, D_MODEL)
    splits = ((0, SSD_INNER), (SSD_INNER, SSD_CONV_DIM), (SSD_INNER + SSD_CONV_DIM, LANES))
    z, xbc, dt_raw = proj_split(x2, in_w_pad, splits, _row_tile(bsz * t_len))
    const = lambda shape: pl.BlockSpec(shape, lambda b: (0,) * len(shape))
    rows = lambda width: pl.BlockSpec((1, vl, width), lambda b: (b, 0, 0))
    state_specs = [pl.BlockSpec((1, CONV_W - 1, SSD_CONV_DIM), lambda b: (b, 0, 0)),
                   pl.BlockSpec((1, SSD_GROUPS, gw, SSD_STATE), lambda b: (b, 0, 0, 0))]
    yn, nbuf, hout = pl.pallas_call(
        functools.partial(_ssd_core_kernel, vl=vl, L=L),
        grid=(bsz,),
        in_specs=([rows(SSD_CONV_DIM), rows(SSD_INNER), rows(LANES)] + state_specs
                  + [const(sh) for sh in scan_shapes]),
        out_specs=[rows(SSD_INNER)] + state_specs,
        out_shape=[jax.ShapeDtypeStruct((bsz, t_len, SSD_INNER), F32)] + state_shapes,
        scratch_shapes=[pltpu.VMEM((HIST + L, SSD_CONV_DIM), F32),
                        pltpu.VMEM((L, SSD_INNER), F32),
                        pltpu.VMEM((L, LANES), F32),
                        pltpu.VMEM((L, SSD_INNER), F32)],
        compiler_params=_cparams(1),
        name="ssd_core",
    )(xbc.reshape(bsz, t_len, SSD_CONV_DIM), z.reshape(bsz, t_len, SSD_INNER),
      dt_raw.reshape(bsz, t_len, LANES), conv_buf, h0g, *scan_args)
    out = outproj_ln(yn.reshape(bsz * t_len, SSD_INNER), out_w, x2, ln_g, ln_b,
                     _row_tile(bsz * t_len)).reshape(bsz, t_len, D_MODEL)
    return out, nbuf, hout.reshape(bsz, SSD_HEADS, SSD_HEADDIM, SSD_STATE)


def _row_tile(rows):
    return ROW_TILE if rows % ROW_TILE == 0 else rows


def _out_tile(rows):
    return OUT_TILE if rows % OUT_TILE == 0 else _row_tile(rows)


def _attn_proj_t_kernel(x_ref, w_ref, wkt_ref, q_ref, kt_ref, v_ref, g_ref):
    x = x_ref[0].astype(BF16)
    q_ref[0] = jnp.dot(x, w_ref[:, 0:ATT_W], preferred_element_type=F32)
    kt_ref[0] = _nt(wkt_ref[...], x)
    v_ref[0] = jnp.dot(x, w_ref[:, 2 * ATT_W:3 * ATT_W], preferred_element_type=F32)
    g_ref[0] = jnp.dot(x, w_ref[:, 3 * ATT_W:4 * ATT_W], preferred_element_type=F32)


def attn_project_t(x, in_w, wkt, tm=ROW_TILE):
    bsz, t_len, _ = x.shape
    rows = lambda: pl.BlockSpec((1, tm, ATT_W), lambda b, i: (b, i, 0))
    return pl.pallas_call(
        _attn_proj_t_kernel,
        grid=(bsz, t_len // tm),
        in_specs=[pl.BlockSpec((1, tm, D_MODEL), lambda b, i: (b, i, 0)),
                  pl.BlockSpec((D_MODEL, 4 * ATT_W), lambda b, i: (0, 0)),
                  pl.BlockSpec((ATT_W, D_MODEL), lambda b, i: (0, 0))],
        out_specs=[rows(), pl.BlockSpec((1, ATT_W, tm), lambda b, i: (b, 0, i)), rows(), rows()],
        out_shape=[jax.ShapeDtypeStruct((bsz, t_len, ATT_W), F32),
                   jax.ShapeDtypeStruct((bsz, ATT_W, t_len), F32),
                   jax.ShapeDtypeStruct((bsz, t_len, ATT_W), F32),
                   jax.ShapeDtypeStruct((bsz, t_len, ATT_W), F32)],
        compiler_params=_cparams(2),
        name="attn_project_t",
    )(x, in_w, wkt)


def _attn_weights(in_w_f32):
    return in_w_f32.astype(BF16), in_w_f32[:, ATT_W:2 * ATT_W].T.astype(BF16)


def _attn_project(x, in_w):
    bsz, t_len, _ = x.shape
    x2 = x.reshape(bsz * t_len, D_MODEL)
    tm = _row_tile(bsz * t_len)
    splits = tuple((i * ATT_W, ATT_W) for i in range(4))
    q, k, v, gate = proj_split(x2, in_w, splits, tm)
    shp = (bsz, t_len, ATT_W)
    return x2, tm, q.reshape(shp), k.reshape(shp), v.reshape(shp), gate.reshape(shp)


def kernel(x_prompt, x_sample, cache_k, cache_v, page_table, state_lru_conv, state_lru_h, state_ssd_conv, state_ssd_h, ln_g, ln_b, a_in_w, a_conv_w, a_conv_b, a_gate_r_w, a_gate_r_b, a_gate_i_w, a_gate_i_b, a_lambda, a_out_w, b_in_w, b_lambda, b_subln_w, b_out_w, c_in_w, c_conv_w, c_conv_b, c_dt_bias, c_a_log, c_d, c_norm_w, c_out_w):
    xp, xs = x_prompt, x_sample
    bp, bs = xp.shape[0], xs.shape[0]
    tp, ts = xp.shape[1], xs.shape[1]
    k_p, v_p, k_s, v_s = [], [], [], []
    lc_p, lh_p, lc_s, lh_s = [], [], [], []
    sc_p, sh_p, sc_s, sh_s = [], [], [], []
    for i in range(DEPTH):
        j = i // N_MIXERS
        kind = i % N_MIXERS
        if kind == 0:
            w = _rglru_weights(a_in_w[j], a_conv_w[j], a_conv_b[j], a_gate_r_w[j], a_gate_r_b[j],
                               a_gate_i_w[j], a_gate_i_b[j], a_lambda[j], a_out_w[j])
            zc = jnp.zeros((bp, CONV_W - 1, LRU_W), F32)
            zh = jnp.zeros((bp, LRU_W), F32)
            xp, c1, h1 = rglru_layer(xp, zc, zh, w, ln_g[i], ln_b[i])
            xs, c2, h2 = rglru_layer(xs, state_lru_conv[j], state_lru_h[j], w, ln_g[i], ln_b[i])
            lc_p.append(c1); lh_p.append(h1); lc_s.append(c2); lh_s.append(h2)
        elif kind == 1:
            lam_init = 0.8 - 0.6 * math.exp(-0.3 * i)
            in_w, wkt = _attn_weights(b_in_w[j])
            out_w = b_out_w[j].astype(BF16)
            q, kt, v, g = attn_project_t(xp, in_w, wkt)
            x2s, tms, qs, ks, vs, gs = _attn_project(xs, in_w)
            og, ogs = attn_core(q, kt, v, g, qs, ks, vs, gs, cache_k, cache_v, j, page_table,
                                b_lambda[j], b_subln_w[j], lam_init)
            xp = outproj_ln(og.reshape(bp * tp, ATT_W), out_w, xp.reshape(bp * tp, D_MODEL),
                            ln_g[i], ln_b[i], _out_tile(bp * tp)).reshape(bp, tp, D_MODEL)
            xs = outproj_ln(ogs.reshape(bs * ts, ATT_W), out_w, x2s, ln_g[i], ln_b[i], tms
                            ).reshape(bs, ts, D_MODEL)
            k_p.append(jnp.transpose(kt.reshape(bp, ATT_HEADS, 2, ATT_DH, tp), (0, 4, 1, 2, 3)))
            v_p.append(v.reshape(bp, tp, ATT_HEADS, 2 * ATT_DH))
            k_s.append(ks.reshape(bs, ts, ATT_HEADS, 2, ATT_DH))
            v_s.append(vs.reshape(bs, ts, ATT_HEADS, 2 * ATT_DH))
        else:
            in_w_pad = jnp.pad(c_in_w[j], ((0, 0), (0, LANES - SSD_HEADS))).astype(BF16)
            w = (in_w_pad, c_conv_w[j], c_conv_b[j], c_dt_bias[j], c_a_log[j], c_d[j], c_norm_w[j],
                 c_out_w[j].astype(BF16))
            zc = jnp.zeros((bp, CONV_W - 1, SSD_CONV_DIM), F32)
            zh = jnp.zeros((bp, SSD_HEADS, SSD_HEADDIM, SSD_STATE), F32)
            xp, c1, h1 = ssd_layer(xp, zc, zh, w, ln_g[i], ln_b[i])
            xs, c2, h2 = ssd_layer(xs, state_ssd_conv[j], state_ssd_h[j], w, ln_g[i], ln_b[i])
            sc_p.append(c1); sh_p.append(h1); sc_s.append(c2); sh_s.append(h2)
    return (xp, xs, jnp.stack(k_p), jnp.stack(v_p), jnp.stack(k_s), jnp.stack(v_s),
            jnp.stack(lc_p), jnp.stack(lh_p), jnp.stack(lc_s), jnp.stack(lh_s),
            jnp.stack(sc_p), jnp.stack(sh_p), jnp.stack(sc_s), jnp.stack(sh_s))
```

```python
import functools
import math

import jax
import jax.numpy as jnp
from jax import lax
from jax.experimental import pallas as pl
from jax.experimental.pallas import tpu as pltpu

F32 = jnp.float32
BF16 = jnp.bfloat16

D_MODEL = 1024
DEPTH = 4
PAGE_SIZE = 128
N_MIXERS = 3
DN_ALPHA = (2.0 * DEPTH) ** 0.25
EPS = 1e-5
CONV_W = 4
LRU_W = D_MODEL
LRU_BLOCKS = 16
LRU_BS = LRU_W // LRU_BLOCKS
LRU_C = 8.0
ATT_HEADS = 8
ATT_DH = D_MODEL // (2 * ATT_HEADS)
ATT_W = ATT_HEADS * 2 * ATT_DH
SSD_INNER = 2 * D_MODEL
SSD_HEADDIM = 64
SSD_HEADS = SSD_INNER // SSD_HEADDIM
SSD_GROUPS = 8
SSD_HPG = SSD_HEADS // SSD_GROUPS
SSD_STATE = 128
SSD_CONV_DIM = SSD_INNER + 2 * SSD_GROUPS * SSD_STATE
SSD_CHUNK = 128

LANES = 128
SUBLANES = 8
MXU_DIM = 256
VMEM_LIMIT = 56 * 1024 * 1024
LOG2E = math.log2(math.e)
F32_TINY = 1.1754944e-38
NEG_BIG = -1e30
HIST = SUBLANES

ROW_TILE = MXU_DIM
ATT_TILE = 2 * MXU_DIM
OUT_TILE = 4 * MXU_DIM
SHORT_TILE = 2 * SUBLANES


def _cparams(n_grid):
    return pltpu.CompilerParams(dimension_semantics=("arbitrary",) * n_grid,
                                vmem_limit_bytes=VMEM_LIMIT)


def _nt(a, b):
    return lax.dot_general(a, b, (((1,), (1,)), ((), ())), preferred_element_type=F32)


def _sqrt_nonneg(x):
    return x * lax.rsqrt(jnp.maximum(x, F32_TINY))


def _layer_norm(v, g, b):
    mu = jnp.mean(v, axis=-1, keepdims=True)
    d = v - mu
    var = jnp.mean(d * d, axis=-1, keepdims=True)
    return d * lax.rsqrt(var + EPS) * g + b


def _silu(x):
    h = 0.5 * x
    return h * jnp.tanh(h) + h


def _softplus(x):
    return jnp.maximum(x, 0.0) + jnp.log(1.0 + jnp.exp(-jnp.abs(x)))


def _proj_kernel(x_ref, w_ref, *out_refs, splits):
    x = x_ref[...].astype(BF16)
    for (off, width), o_ref in zip(splits, out_refs):
        o_ref[...] = jnp.dot(x, w_ref[:, off:off + width], preferred_element_type=F32)


def proj_split(x2d, w_bf16, splits, tm):
    rows, kdim = x2d.shape
    ndim = w_bf16.shape[1]
    return pl.pallas_call(
        functools.partial(_proj_kernel, splits=splits),
        grid=(rows // tm,),
        in_specs=[pl.BlockSpec((tm, kdim), lambda i: (i, 0)),
                  pl.BlockSpec((kdim, ndim), lambda i: (0, 0))],
        out_specs=[pl.BlockSpec((tm, wd), lambda i: (i, 0)) for _, wd in splits],
        out_shape=[jax.ShapeDtypeStruct((rows, wd), F32) for _, wd in splits],
        compiler_params=_cparams(1),
        name="proj_split",
    )(x2d, w_bf16)


def _outproj_ln_kernel(y_ref, w_ref, x_ref, g_ref, b_ref, o_ref):
    f = jnp.dot(y_ref[...].astype(BF16), w_ref[...], preferred_element_type=F32)
    o_ref[...] = _layer_norm(DN_ALPHA * x_ref[...] + f, g_ref[...], b_ref[...])


def outproj_ln(y2d, w_bf16, x2d, g, b, tm):
    rows, kdim = y2d.shape
    return pl.pallas_call(
        _outproj_ln_kernel,
        grid=(rows // tm,),
        in_specs=[pl.BlockSpec((tm, kdim), lambda i: (i, 0)),
                  pl.BlockSpec((kdim, D_MODEL), lambda i: (0, 0)),
                  pl.BlockSpec((tm, D_MODEL), lambda i: (i, 0)),
                  pl.BlockSpec((1, D_MODEL), lambda i: (0, 0)),
                  pl.BlockSpec((1, D_MODEL), lambda i: (0, 0))],
        out_specs=pl.BlockSpec((tm, D_MODEL), lambda i: (i, 0)),
        out_shape=jax.ShapeDtypeStruct((rows, D_MODEL), F32),
        compiler_params=_cparams(1),
        name="outproj_ln",
    )(y2d, w_bf16, x2d, g.reshape(1, D_MODEL), b.reshape(1, D_MODEL))


def _rglru_core(ub, gate, a_s, b_s, h_in, gate_refs, *, tt, unroll):
    cw_ref, cb_ref, wr_ref, rb_ref, wi_ref, ib_ref, lam_ref = gate_refs
    cw = cw_ref[...]
    ext = ub[...]
    y = cb_ref[...] + cw[3:4, :] * ext[HIST:, :]
    for k in range(1, CONV_W):
        y = y + cw[3 - k:4 - k, :] * pltpu.roll(ext, k, 0)[HIST:, :]

    c = (-0.5 * LRU_C) * _softplus(-lam_ref[...])
    yb = y.astype(BF16)
    for j in range(LRU_W // MXU_DIM):
        sl = slice(j * MXU_DIM, (j + 1) * MXU_DIM)
        ys = yb[:, sl]
        t_r = jnp.tanh(jnp.dot(ys, wr_ref[j], preferred_element_type=F32) + rb_ref[:, sl])
        t_i = jnp.tanh(jnp.dot(ys, wi_ref[j], preferred_element_type=F32) + ib_ref[:, sl])
        log_a = c[:, sl] * t_r + c[:, sl]
        a = jnp.exp(log_a)
        a_s[:, sl] = a
        half_y = 0.5 * y[:, sl]
        b_s[:, sl] = (_sqrt_nonneg(-jnp.tanh(log_a) * (a * a + 1.0))
                      * (half_y * t_i + half_y))

    row = lax.broadcasted_iota(jnp.int32, (SUBLANES, LRU_W), 0)

    def group(r0, hc):
        a = a_s[pl.ds(r0, SUBLANES), :]
        b = b_s[pl.ds(r0, SUBLANES), :]
        for d in (1, 2, 4):
            a_sh = pltpu.roll(a, d, 0)
            b_sh = pltpu.roll(b, d, 0)
            m = row >= d
            b = jnp.where(m, a * b_sh + b, b)
            a = jnp.where(m, a * a_sh, a)
        h = a * hc + b
        b_s[pl.ds(r0, SUBLANES), :] = h
        return h[SUBLANES - 1:SUBLANES, :]

    if unroll:
        hc = h_in
        for g in range(tt // SUBLANES):
            hc = group(g * SUBLANES, hc)
    else:
        lax.fori_loop(0, tt // SUBLANES,
                      lambda g, hc: group(pl.multiple_of(g * SUBLANES, SUBLANES), hc), h_in)
    return b_s[...] * _silu(gate[...])


def _rglru_core_kernel(gate_ref, u_ref, buf_ref, h0_ref, *refs, tt, valid):
    gate_refs = refs[:7]
    out_ref, nbuf_ref, hlast_ref, ubuf, g_s, a_s, b_s = refs[7:]
    ubuf[0:HIST - 3, :] = jnp.zeros((HIST - 3, LRU_W), F32)
    ubuf[HIST - 3:HIST, :] = buf_ref[0]
    ubuf[HIST:HIST + valid, :] = u_ref[0]
    ubuf[HIST + valid:HIST + tt, :] = jnp.zeros((tt - valid, LRU_W), F32)
    g_s[0:valid, :] = gate_ref[0]
    g_s[valid:tt, :] = jnp.zeros((tt - valid, LRU_W), F32)
    hg = _rglru_core(ubuf, g_s, a_s, b_s, h0_ref[0], gate_refs, tt=tt, unroll=False)
    out_ref[0] = hg[0:valid, :]
    nbuf_ref[0] = ubuf[HIST + valid - 3:HIST + valid, :]
    hlast_ref[0] = b_s[valid - 1:valid, :]


def _rglru_layer_kernel(xa_ref, xb_ref, xn_ref, buf_ref, h0_ref, win_ref, *refs, tt, n_steps):
    gate_refs = refs[:7]
    wout_ref, lng_ref, lnb_ref, out_ref, nbuf_ref, hlast_ref, ubuf, g_s, a_s, b_s, hcar = refs[7:]
    s = pl.program_id(0)
    first = s % n_steps == 0

    def project(x_ref, slot):
        xb = x_ref[0].astype(BF16)
        g_s[slot] = jnp.dot(xb, win_ref[:, 0:LRU_W], preferred_element_type=F32)
        ubuf[slot, HIST:HIST + tt, :] = jnp.dot(xb, win_ref[:, LRU_W:2 * LRU_W],
                                                preferred_element_type=F32)

    def finish(x_ref, hg):
        f = jnp.dot(hg.astype(BF16), wout_ref[...], preferred_element_type=F32)
        return _layer_norm(DN_ALPHA * x_ref[0] + f, lng_ref[...], lnb_ref[...])

    @pl.when(s == 0)
    def _():
        for slot in range(2):
            ubuf[slot, 0:HIST - 3, :] = jnp.zeros((HIST - 3, LRU_W), F32)
        project(xa_ref, 0)

    @pl.when(first)
    def _():
        ubuf[0, HIST - 3:HIST, :] = buf_ref[0]
        hcar[...] = h0_ref[0]

    @pl.when(jnp.logical_not(first))
    def _():
        ubuf[0, HIST - 3:HIST, :] = ubuf[1, HIST + tt - 3:HIST + tt, :]

    project(xb_ref, 1)
    hg = _rglru_core(ubuf.at[0], g_s.at[0], a_s, b_s, hcar[...], gate_refs, tt=tt, unroll=True)
    h_a = b_s[tt - 1:tt, :]
    out_ref[0, 0:tt, :] = finish(xa_ref, hg)
    ubuf[1, HIST - 3:HIST, :] = ubuf[0, HIST + tt - 3:HIST + tt, :]
    project(xn_ref, 0)
    hg = _rglru_core(ubuf.at[1], g_s.at[1], a_s, b_s, h_a, gate_refs, tt=tt, unroll=True)
    h_b = b_s[tt - 1:tt, :]
    hcar[...] = h_b
    out_ref[0, tt:2 * tt, :] = finish(xb_ref, hg)

    @pl.when(s % n_steps == n_steps - 1)
    def _():
        nbuf_ref[0] = ubuf[1, HIST + tt - 3:HIST + tt, :]
        hlast_ref[0] = h_b


def rglru_layer(x, conv_buf, h0, w, ln_g, ln_b):
    in_w, conv_w, conv_b, wr_bd, r_b, wi_bd, i_b, lam, out_w = w
    bsz, t_len, _ = x.shape
    row = lambda v: v.reshape(1, LRU_W)
    gate_args = (conv_w, row(conv_b), wr_bd, row(r_b), wi_bd, row(i_b), row(lam))
    h0r = h0.reshape(bsz, 1, LRU_W)
    state_shapes = [jax.ShapeDtypeStruct((bsz, CONV_W - 1, LRU_W), F32),
                    jax.ShapeDtypeStruct((bsz, 1, LRU_W), F32)]
    tt = ROW_TILE
    if t_len % (2 * tt) == 0:
        n_steps = t_len // (2 * tt)
        n_tiles = bsz * t_len // tt
        const = lambda shape: pl.BlockSpec(shape, lambda s: (0,) * len(shape))
        x_tile = lambda off: pl.BlockSpec(
            (1, tt, D_MODEL), lambda s: (jnp.minimum(2 * s + off, n_tiles - 1), 0, 0))
        per_seq = lambda r: pl.BlockSpec((1, r, LRU_W), lambda s: (s // n_steps, 0, 0))
        gate_specs = [const((CONV_W, LRU_W)), const((1, LRU_W)),
                      const((LRU_W // MXU_DIM, MXU_DIM, MXU_DIM)), const((1, LRU_W)),
                      const((LRU_W // MXU_DIM, MXU_DIM, MXU_DIM)), const((1, LRU_W)),
                      const((1, LRU_W))]
        xt = x.reshape(n_tiles, tt, D_MODEL)
        out, nbuf, hlast = pl.pallas_call(
            functools.partial(_rglru_layer_kernel, tt=tt, n_steps=n_steps),
            grid=(bsz * n_steps,),
            in_specs=([x_tile(0), x_tile(1), x_tile(2), per_seq(CONV_W - 1), per_seq(1),
                       const((D_MODEL, 2 * LRU_W))] + gate_specs
                      + [const((LRU_W, D_MODEL)), const((1, D_MODEL)), const((1, D_MODEL))]),
            out_specs=[pl.BlockSpec((1, 2 * tt, D_MODEL), lambda s: (s, 0, 0)),
                       per_seq(CONV_W - 1), per_seq(1)],
            out_shape=[jax.ShapeDtypeStruct((bsz * n_steps, 2 * tt, D_MODEL), F32)] + state_shapes,
            scratch_shapes=[pltpu.VMEM((2, HIST + tt, LRU_W), F32),
                            pltpu.VMEM((2, tt, LRU_W), F32),
                            pltpu.VMEM((tt, LRU_W), F32),
                            pltpu.VMEM((tt, LRU_W), F32),
                            pltpu.VMEM((1, LRU_W), F32)],
            compiler_params=_cparams(1),
            name="rglru_layer",
        )(xt, xt, xt, conv_buf, h0r, in_w, *gate_args, out_w, row(ln_g), row(ln_b))
        return out.reshape(bsz, t_len, D_MODEL), nbuf, hlast.reshape(bsz, LRU_W)

    tt, valid = SHORT_TILE, t_len
    assert t_len <= tt
    x2 = x.reshape(bsz * t_len, D_MODEL)
    gate, u = proj_split(x2, in_w, ((0, LRU_W), (LRU_W, LRU_W)), _row_tile(bsz * t_len))
    vec = lambda: pl.BlockSpec((1, LRU_W), lambda b: (0, 0))
    wspec = lambda: pl.BlockSpec((LRU_W // MXU_DIM, MXU_DIM, MXU_DIM), lambda b: (0, 0, 0))
    per_seq = lambda r: pl.BlockSpec((1, r, LRU_W), lambda b: (b, 0, 0))
    hg, nbuf, hlast = pl.pallas_call(
        functools.partial(_rglru_core_kernel, tt=tt, valid=valid),
        grid=(bsz,),
        in_specs=[per_seq(valid), per_seq(valid), per_seq(CONV_W - 1), per_seq(1),
                  pl.BlockSpec((CONV_W, LRU_W), lambda b: (0, 0)),
                  vec(), wspec(), vec(), wspec(), vec(), vec()],
        out_specs=[per_seq(valid), per_seq(CONV_W - 1), per_seq(1)],
        out_shape=[jax.ShapeDtypeStruct((bsz, t_len, LRU_W), F32)] + state_shapes,
        scratch_shapes=[pltpu.VMEM((HIST + tt, LRU_W), F32),
                        pltpu.VMEM((tt, LRU_W), F32),
                        pltpu.VMEM((tt, LRU_W), F32),
                        pltpu.VMEM((tt, LRU_W), F32)],
        compiler_params=_cparams(1),
        name="rglru_core",
    )(gate.reshape(bsz, t_len, LRU_W), u.reshape(bsz, t_len, LRU_W), conv_buf, h0r, *gate_args)
    out = outproj_ln(hg.reshape(bsz * t_len, LRU_W), out_w, x2, ln_g, ln_b,
                     _row_tile(bsz * t_len)).reshape(bsz, t_len, D_MODEL)
    return out, nbuf, hlast.reshape(bsz, LRU_W)


def _block_diag_tiles(w):
    per = MXU_DIM // LRU_BS
    w4 = w.reshape(LRU_W // MXU_DIM, per, LRU_BS, LRU_BS)
    eye = jnp.eye(per, dtype=w.dtype)
    t = jnp.einsum("jakc,ab->jakbc", w4, eye)
    return t.reshape(LRU_W // MXU_DIM, MXU_DIM, MXU_DIM).astype(BF16)


def _rglru_weights(in_w, conv_w, conv_b, r_w, r_b, i_w, i_b, lam, out_w):
    return (in_w.astype(BF16), conv_w, conv_b, _block_diag_tiles(0.5 * r_w), 0.5 * r_b,
            _block_diag_tiles(0.5 * i_w), 0.5 * i_b, lam, out_w.astype(BF16))


def _diff_lambda(lp, lam_init):
    s1 = jnp.sum(lp[0:1, :] * lp[1:2, :], axis=-1, keepdims=True)
    s2 = jnp.sum(lp[2:3, :] * lp[3:4, :], axis=-1, keepdims=True)
    return jnp.exp(s1) - jnp.exp(s2) + lam_init


def _subln_gate(o, sw, gate, lam_init):
    ms = jnp.mean(o * o, axis=-1, keepdims=True)
    return (o * lax.rsqrt(ms + EPS) * sw * (1.0 - lam_init)) * _silu(gate)


def _attn_prompt_kernel(lp_ref, q_ref, kt_ref, v_ref, gate_ref, sw_ref, o_ref,
                        kb_s, vx_s, m_s, accl_s, s_s, *, tq, lam_init):
    qi = pl.program_id(2)
    hd = 2 * ATT_DH
    t_len = v_ref.shape[1]

    @pl.when(qi == 0)
    def _():
        for i in range(t_len // tq):
            kb_s[i] = kt_ref[0, :, i * tq:(i + 1) * tq].astype(BF16)

        def cvt(i, carry):
            r = pl.multiple_of(i * tq, tq)
            vx_s[pl.ds(r, tq), 0:hd] = v_ref[0, pl.ds(r, tq), :].astype(BF16)
            vx_s[pl.ds(r, tq), hd:2 * hd] = jnp.ones((tq, hd), BF16)
            return carry
        lax.fori_loop(0, t_len // tq, cvt, 0)

    lam = _diff_lambda(lp_ref[...], lam_init)
    q = q_ref[0] * (ATT_DH ** -0.5 * LOG2E)
    lane = lax.broadcasted_iota(jnp.int32, (tq, hd), 1)
    q_maps = (jnp.where(lane < ATT_DH, q, 0.0).astype(BF16),
              jnp.where(lane >= ATT_DH, q, 0.0).astype(BF16))
    m_s[...] = jnp.full(m_s.shape, NEG_BIG, F32)
    accl_s[...] = jnp.zeros(accl_s.shape, F32)
    rowi = lax.broadcasted_iota(jnp.int32, (tq, tq), 0)
    coli = lax.broadcasted_iota(jnp.int32, (tq, tq), 1)

    def scores(ki, slot):
        kb = kb_s[ki]
        for c in range(2):
            s_s[slot, c] = jnp.dot(q_maps[c], kb, preferred_element_type=F32)

    def absorb(ki, slot, masked):
        ks = pl.multiple_of(ki * tq, tq)
        vx = vx_s[pl.ds(ks, tq), :]
        for c in range(2):
            s = s_s[slot, c]
            if masked:
                s = jnp.where(coli <= rowi, s, NEG_BIG)
            m_prev = m_s[c]
            m_new = jnp.maximum(m_prev, jnp.max(s, axis=-1, keepdims=True))
            alpha = jnp.exp2(m_prev - m_new)
            p = jnp.exp2(s - jnp.concatenate([m_new] * (tq // hd), axis=1))
            pv = jnp.dot(p.astype(BF16), vx, preferred_element_type=F32)
            accl_s[c] = jnp.concatenate([alpha, alpha], axis=1) * accl_s[c] + pv
            m_s[c] = m_new

    scores(0, 0)

    def pair(j, carry):
        k0 = 2 * j
        scores(k0 + 1, 1)
        absorb(k0, 0, False)
        scores(k0 + 2, 0)
        absorb(k0 + 1, 1, False)
        return carry

    lax.fori_loop(0, qi // 2, pair, 0)

    @pl.when(qi % 2 == 0)
    def _():
        absorb(qi, 0, True)

    @pl.when(qi % 2 == 1)
    def _():
        scores(qi, 1)
        absorb(qi - 1, 0, False)
        absorb(qi, 1, True)

    a1 = accl_s[0]
    a2 = accl_s[1]
    o = a1[:, 0:hd] / a1[:, hd:2 * hd] - lam * (a2[:, 0:hd] / a2[:, hd:2 * hd])
    o_ref[0] = _subln_gate(o, sw_ref[...], gate_ref[0], lam_init)


def _attn_sample_parts(lp_ref, qbd_ref, *refs, n_pg, t_new, lam_init):
    k_refs = refs[:n_pg]
    v_refs = refs[n_pg:2 * n_pg]
    knew_ref, vnew_ref, gate_ref, sw_ref, o_ref, m_s, l_s, acc_s = refs[2 * n_pg:]
    j = pl.program_id(2)
    n_j = pl.num_programs(2)
    n_rows = ATT_HEADS * 2 * t_new
    hd = 2 * ATT_DH

    def start():
        @pl.when(j == 0)
        def _():
            m_s[...] = jnp.full(m_s.shape, NEG_BIG, F32)
            l_s[...] = jnp.zeros(l_s.shape, F32)
            acc_s[...] = jnp.zeros(acc_s.shape, F32)

    def attend(kts, vbs, mask):
        qbd = qbd_ref[0]
        n = len(kts)
        s = jnp.concatenate([jnp.dot(qbd, kt.astype(BF16), preferred_element_type=F32)
                             for kt in kts], axis=1)
        if mask is not None:
            s = jnp.where(mask, s, NEG_BIG)
        m_prev = m_s[...]
        m_new = jnp.maximum(m_prev, jnp.max(s, axis=-1, keepdims=True))
        alpha = jnp.exp2(m_prev - m_new)
        p = jnp.exp2(s - jnp.concatenate([m_new] * n, axis=1))
        l_s[...] = alpha * l_s[...] + jnp.sum(p, axis=-1, keepdims=True)
        pb = p.astype(BF16)
        pv = jnp.dot(pb[:, 0:PAGE_SIZE], vbs[0].astype(BF16), preferred_element_type=F32)
        for i in range(1, n):
            pv = pv + jnp.dot(pb[:, i * PAGE_SIZE:(i + 1) * PAGE_SIZE], vbs[i].astype(BF16),
                              preferred_element_type=F32)
        acc_s[...] = jnp.concatenate([alpha] * (ATT_W // LANES), axis=1) * acc_s[...] + pv
        m_s[...] = m_new

    def page_v(vr):
        return jnp.concatenate([vr[0, pl.ds(h, PAGE_SIZE, stride=ATT_HEADS), :]
                                for h in range(ATT_HEADS)], axis=1)

    def pages():
        attend([r[0] for r in k_refs], [page_v(r) for r in v_refs], None)

    def finish():
        @pl.when(j == n_j - 1)
        def _():
            rowi = lax.broadcasted_iota(jnp.int32, (n_rows, PAGE_SIZE), 0)
            coli = lax.broadcasted_iota(jnp.int32, (n_rows, PAGE_SIZE), 1)
            attend([knew_ref[0]], [vnew_ref[0]], coli <= (rowi % t_new))
            lam = _diff_lambda(lp_ref[...], lam_init)
            sw = sw_ref[...]
            for h in range(ATT_HEADS):
                r1 = h * 2 * t_new
                r2 = r1 + t_new
                cs = slice(h * hd, (h + 1) * hd)
                o1 = acc_s[r1:r1 + t_new, cs] / l_s[r1:r1 + t_new, :]
                o2 = acc_s[r2:r2 + t_new, cs] / l_s[r2:r2 + t_new, :]
                o_ref[0, :, cs] = _subln_gate(o1 - lam * o2, sw, gate_ref[0, :, cs], lam_init)

    return start, pages, finish


def _attn_kernel(pt_ref, lp_ref, q_ref, kt_ref, v_ref, gate_ref, sw_ref, qbd_ref, *refs,
                 n_pg, tq, t_new, lam_init):
    del pt_ref
    sample_in = refs[:2 * n_pg + 3]
    o_ref, os_ref, kb_s, vx_s, m_s, accl_s, s_s, ms_s, ls_s, accs_s = refs[2 * n_pg + 3:]
    start, pages, finish = _attn_sample_parts(
        lp_ref, qbd_ref, *sample_in, sw_ref, os_ref, ms_s, ls_s, accs_s,
        n_pg=n_pg, t_new=t_new, lam_init=lam_init)
    _attn_prompt_kernel(lp_ref, q_ref, kt_ref, v_ref, gate_ref, sw_ref, o_ref,
                        kb_s, vx_s, m_s, accl_s, s_s, tq=tq, lam_init=lam_init)
    start()
    pages()
    finish()


def attn_core(q, kt, v, gate, qs, ks, vs, gate_s, cache_k, cache_v, layer, page_table, lam_p,
              subln_w, lam_init, tq=ATT_TILE):
    bsz_p, t_len, _ = q.shape
    bsz, t_new, _ = qs.shape
    n_pages = page_table.shape[1]
    n_pool = cache_k.shape[1]
    hd = 2 * ATT_DH
    n_rows = ATT_HEADS * 2 * t_new
    n_q = t_len // tq
    assert bsz == bsz_p * ATT_HEADS and n_pages % n_q == 0
    n_pg = n_pages // n_q
    q4 = (qs * (ATT_DH ** -0.5 * LOG2E)).reshape(bsz, t_new, 2 * ATT_HEADS, ATT_DH)
    eye = jnp.eye(2 * ATT_HEADS, dtype=F32)
    qbd = jnp.einsum("bqhd,hg->bhqgd", q4, eye).reshape(bsz, n_rows, ATT_W).astype(BF16)
    k_new = jnp.pad(jnp.swapaxes(ks, 1, 2), ((0, 0), (0, 0), (0, PAGE_SIZE - t_new)))
    v_new = jnp.pad(vs, ((0, 0), (0, PAGE_SIZE - t_new), (0, 0)))
    ck = jnp.transpose(cache_k, (0, 1, 3, 4, 5, 2)).reshape(-1, ATT_W, PAGE_SIZE)
    cv = cache_v.reshape(-1, PAGE_SIZE * ATT_HEADS, hd)
    pt = page_table.reshape(-1) + layer * n_pool

    sample = lambda b, h: b * ATT_HEADS + h

    def page_spec(i, rows, width):
        return pl.BlockSpec(
            (1, rows, width),
            lambda b, h, j, pt_ref: (pt_ref[sample(b, h) * n_pages + j * n_pg + i], 0, 0))

    const = lambda shape: pl.BlockSpec(shape, lambda b, h, j, pt_ref: (0,) * len(shape))
    qspec = lambda: pl.BlockSpec((1, tq, hd), lambda b, h, j, pt_ref: (b, j, h))
    per_s = lambda r, w: pl.BlockSpec((1, r, w), lambda b, h, j, pt_ref: (sample(b, h), 0, 0))
    grid_spec = pltpu.PrefetchScalarGridSpec(
        num_scalar_prefetch=1,
        grid=(bsz_p, ATT_HEADS, n_q),
        in_specs=([const((4, ATT_DH)), qspec(),
                   pl.BlockSpec((1, hd, t_len), lambda b, h, j, pt_ref: (b, h, 0)),
                   pl.BlockSpec((1, t_len, hd), lambda b, h, j, pt_ref: (b, 0, h)),
                   qspec(), const((1, hd)), per_s(n_rows, ATT_W)]
                  + [page_spec(i, ATT_W, PAGE_SIZE) for i in range(n_pg)]
                  + [page_spec(i, PAGE_SIZE * ATT_HEADS, hd) for i in range(n_pg)]
                  + [per_s(ATT_W, PAGE_SIZE), per_s(PAGE_SIZE, ATT_W), per_s(t_new, ATT_W)]),
        out_specs=[qspec(), per_s(t_new, ATT_W)],
        scratch_shapes=[pltpu.VMEM((n_q, hd, tq), BF16),
                        pltpu.VMEM((t_len, 2 * hd), BF16),
                        pltpu.VMEM((2, tq, hd), F32), pltpu.VMEM((2, tq, 2 * hd), F32),
                        pltpu.VMEM((2, 2, tq, tq), F32),
                        pltpu.VMEM((n_rows, LANES), F32), pltpu.VMEM((n_rows, LANES), F32),
                        pltpu.VMEM((n_rows, ATT_W), F32)],
    )
    return pl.pallas_call(
        functools.partial(_attn_kernel, n_pg=n_pg, tq=tq, t_new=t_new, lam_init=lam_init),
        grid_spec=grid_spec,
        out_shape=[jax.ShapeDtypeStruct((bsz_p, t_len, ATT_W), F32),
                   jax.ShapeDtypeStruct((bsz, t_new, ATT_W), F32)],
        compiler_params=_cparams(3),
        name="attn_core",
    )(pt, lam_p, q, kt, v, gate, subln_w.reshape(1, hd), qbd, *([ck] * n_pg), *([cv] * n_pg),
      k_new, v_new, gate_s)


def _ssd_core(ub, z, dt_raw, hout_ref, y_s, scan_refs, *, vl, L=SSD_CHUNK, after_group=None):
    cw_ref, cb_ref, dtb_ref, alog_ref, dexp_ref, nw_ref = scan_refs
    gw = SSD_HPG * SSD_HEADDIM
    rowL = lax.broadcasted_iota(jnp.int32, (L, LANES), 0)
    laneL = lax.broadcasted_iota(jnp.int32, (L, LANES), 1)
    dt = _softplus(dt_raw + dtb_ref[...])
    dt = jnp.where((rowL < vl) & (laneL < SSD_HEADS), dt, 0.0)
    adt = dt * (-jnp.exp(alog_ref[...]))
    tri = (lax.broadcasted_iota(jnp.int32, (L, L), 0)
           >= lax.broadcasted_iota(jnp.int32, (L, L), 1))
    cs = jnp.dot(tri.astype(F32), adt, preferred_element_type=F32,
                 precision=lax.Precision.HIGHEST)
    cs_last = cs[L - 1:L, :]
    w1 = dt * jnp.exp(cs_last - cs)
    cs_t = cs.T
    dt_t = dt.T
    w1_t = w1.T

    cw = cw_ref[...]
    lane_g = lax.broadcasted_iota(jnp.int32, (L, gw), 1)

    def conv_silu(lo, width):
        sl = slice(lo, lo + width)
        y = cb_ref[:, sl] + cw[3:4, sl] * ub[HIST:HIST + L, sl]
        for k in range(1, CONV_W):
            y = y + cw[3 - k:4 - k, sl] * ub[HIST - k:HIST - k + L, sl]
        return _silu(y)

    for g in range(SSD_GROUPS):
        xh = conv_silu(g * gw, gw)
        bm = conv_silu(SSD_INNER + g * SSD_STATE, SSD_STATE)
        cm = conv_silu(SSD_INNER + (SSD_GROUPS + g) * SSD_STATE, SSD_STATE)
        bmb = bm.astype(BF16)
        cmb = cm.astype(BF16)
        xhb = xh.astype(BF16)
        cbm = _nt(cmb, bmb)
        y_diag = jnp.zeros((L, gw), F32)
        e_cols = []
        w_rows = []
        d_rows = []
        for r in range(SSD_HPG):
            h = g * SSD_HPG + r
            cs_col = cs[:, h:h + 1]
            cs_row = cs_t[h:h + 1, :]
            lm = jnp.exp(jnp.where(tri, cs_col - cs_row, NEG_BIG))
            mat = (cbm * lm * dt_t[h:h + 1, :]).astype(BF16)
            in_head = (lane_g >= r * SSD_HEADDIM) & (lane_g < (r + 1) * SSD_HEADDIM)
            xr = jnp.where(in_head, xhb, jnp.zeros_like(xhb))
            y_diag = y_diag + jnp.dot(mat, xr, preferred_element_type=F32)
            e_cols.append(jnp.broadcast_to(jnp.exp(cs_col), (L, SSD_HEADDIM)))
            w_rows.append(jnp.broadcast_to(w1_t[h:h + 1, :], (SSD_HEADDIM, L)))
            d_rows.append(jnp.broadcast_to(jnp.exp(cs_t[h:h + 1, L - 1:L]),
                                           (SSD_HEADDIM, SSD_STATE)))
        hg = hout_ref[0, g]
        y_off = _nt(cmb, hg.astype(BF16)) * jnp.concatenate(e_cols, axis=1)
        y_s[:, g * gw:(g + 1) * gw] = y_diag + y_off + dexp_ref[:, g * gw:(g + 1) * gw] * xh
        xd_t = (xh.T * jnp.concatenate(w_rows, axis=0)).astype(BF16)
        states = jnp.dot(xd_t, bmb, preferred_element_type=F32)
        hout_ref[0, g] = hg * jnp.concatenate(d_rows, axis=0) + states
        if after_group is not None:
            after_group(g)

    gated = y_s[...] * _silu(z[...])
    ms = jnp.mean(gated * gated, axis=-1, keepdims=True)
    return gated * lax.rsqrt(ms + EPS) * nw_ref[...]


def _ssd_core_kernel(xbc_ref, z_ref, dtr_ref, buf_ref, h0_ref, *refs, vl, L):
    scan_refs = refs[:6]
    out_ref, nbuf_ref, hout_ref, ubuf, z_s, dt_s, y_s = refs[6:]
    ubuf[HIST - 3:HIST, :] = buf_ref[0]
    ubuf[HIST:HIST + vl, :] = xbc_ref[0]
    ubuf[HIST + vl:HIST + L, :] = jnp.zeros((L - vl, SSD_CONV_DIM), F32)
    z_s[0:vl, :] = z_ref[0]
    z_s[vl:L, :] = jnp.zeros((L - vl, SSD_INNER), F32)
    dt_s[0:vl, :] = dtr_ref[0]
    dt_s[vl:L, :] = jnp.zeros((L - vl, LANES), F32)
    hout_ref[0] = h0_ref[0]
    yn = _ssd_core(ubuf, z_s, dt_s[...], hout_ref, y_s, scan_refs, vl=vl, L=L)
    out_ref[0] = yn[0:vl, :]
    nbuf_ref[0] = ubuf[HIST + vl - 3:HIST + vl, :]


def _ssd_layer_kernel(xa_ref, xb_ref, xn_ref, buf_ref, h0_ref, win_ref, *refs, n_steps):
    scan_refs = refs[:6]
    (wout_ref, lng_ref, lnb_ref, out_ref, nbuf_ref, hout_ref, ubuf, z_s, dt_s, y_s) = refs[6:]
    L = SSD_CHUNK
    s = pl.program_id(0)
    first = s % n_steps == 0
    xbc_lo = SSD_INNER
    dt_lo = SSD_INNER + SSD_CONV_DIM

    n_sl = SSD_GROUPS
    zw = SSD_INNER // n_sl
    cwid = SSD_CONV_DIM // n_sl

    def project_slice(xb, slot, i):
        z_s[slot, :, i * zw:(i + 1) * zw] = jnp.dot(
            xb, win_ref[:, i * zw:(i + 1) * zw], preferred_element_type=F32)
        ubuf[slot, HIST:HIST + L, i * cwid:(i + 1) * cwid] = jnp.dot(
            xb, win_ref[:, xbc_lo + i * cwid:xbc_lo + (i + 1) * cwid],
            preferred_element_type=F32)
        if i == 0:
            dt_s[slot] = jnp.dot(xb, win_ref[:, dt_lo:dt_lo + LANES],
                                 preferred_element_type=F32)

    def finish(x_ref, yn):
        f = jnp.dot(yn.astype(BF16), wout_ref[...], preferred_element_type=F32)
        return _layer_norm(DN_ALPHA * x_ref[0] + f, lng_ref[...], lnb_ref[...])

    @pl.when(s == 0)
    def _():
        xb = xa_ref[0].astype(BF16)
        for i in range(n_sl):
            project_slice(xb, 0, i)

    @pl.when(first)
    def _():
        ubuf[0, HIST - 3:HIST, :] = buf_ref[0]
        hout_ref[0] = h0_ref[0]

    @pl.when(jnp.logical_not(first))
    def _():
        ubuf[0, HIST - 3:HIST, :] = ubuf[1, HIST + L - 3:HIST + L, :]

    xb_b = xb_ref[0].astype(BF16)
    yn = _ssd_core(ubuf.at[0], z_s.at[0], dt_s[0], hout_ref, y_s, scan_refs, vl=L,
                   after_group=lambda g: project_slice(xb_b, 1, g))
    out_ref[0, 0:L, :] = finish(xa_ref, yn)
    ubuf[1, HIST - 3:HIST, :] = ubuf[0, HIST + L - 3:HIST + L, :]
    xb_n = xn_ref[0].astype(BF16)
    yn = _ssd_core(ubuf.at[1], z_s.at[1], dt_s[1], hout_ref, y_s, scan_refs, vl=L,
                   after_group=lambda g: project_slice(xb_n, 0, g))
    out_ref[0, L:2 * L, :] = finish(xb_ref, yn)

    @pl.when(s % n_steps == n_steps - 1)
    def _():
        nbuf_ref[0] = ubuf[1, HIST + L - 3:HIST + L, :]


def ssd_layer(x, conv_buf, h0, w, ln_g, ln_b):
    in_w_pad, conv_w, conv_b, dt_bias, a_log, d_skip, norm_w, out_w = w
    bsz, t_len, _ = x.shape
    L = SSD_CHUNK
    gw = SSD_HPG * SSD_HEADDIM
    n_proj = in_w_pad.shape[1]
    pad_lanes = lambda v: jnp.pad(v.reshape(1, SSD_HEADS), ((0, 0), (0, LANES - SSD_HEADS)))
    dexp = jnp.repeat(d_skip, SSD_HEADDIM).reshape(1, SSD_INNER)
    h0g = h0.reshape(bsz, SSD_GROUPS, gw, SSD_STATE)
    scan_args = (conv_w, conv_b.reshape(1, SSD_CONV_DIM), pad_lanes(dt_bias), pad_lanes(a_log),
                 dexp, norm_w.reshape(1, SSD_INNER))
    scan_shapes = [(CONV_W, SSD_CONV_DIM), (1, SSD_CONV_DIM), (1, LANES), (1, LANES),
                   (1, SSD_INNER), (1, SSD_INNER)]
    state_shapes = [jax.ShapeDtypeStruct((bsz, CONV_W - 1, SSD_CONV_DIM), F32),
                    jax.ShapeDtypeStruct((bsz, SSD_GROUPS, gw, SSD_STATE), F32)]
    if t_len % (2 * L) == 0:
        n_steps = t_len // (2 * L)
        n_chunks = bsz * t_len // L
        const = lambda shape: pl.BlockSpec(shape, lambda s: (0,) * len(shape))
        resident = lambda shape: pl.BlockSpec(shape, lambda s: (0,) * len(shape),
                                              pipeline_mode=pl.Buffered(1))
        x_chunk = lambda off: pl.BlockSpec(
            (1, L, D_MODEL), lambda s: (jnp.minimum(2 * s + off, n_chunks - 1), 0, 0))
        state_specs = [pl.BlockSpec((1, CONV_W - 1, SSD_CONV_DIM), lambda s: (s // n_steps, 0, 0)),
                       pl.BlockSpec((1, SSD_GROUPS, gw, SSD_STATE),
                                    lambda s: (s // n_steps, 0, 0, 0))]
        xt = x.reshape(n_chunks, L, D_MODEL)
        out, nbuf, hout = pl.pallas_call(
            functools.partial(_ssd_layer_kernel, n_steps=n_steps),
            grid=(bsz * n_steps,),
            in_specs=([x_chunk(0), x_chunk(1), x_chunk(2)] + state_specs
                      + [resident((D_MODEL, n_proj))] + [const(sh) for sh in scan_shapes]
                      + [resident((SSD_INNER, D_MODEL)), const((1, D_MODEL)), const((1, D_MODEL))]),
            out_specs=[pl.BlockSpec((1, 2 * L, D_MODEL), lambda s: (s, 0, 0))] + state_specs,
            out_shape=[jax.ShapeDtypeStruct((bsz * n_steps, 2 * L, D_MODEL), F32)] + state_shapes,
            scratch_shapes=[pltpu.VMEM((2, HIST + L, SSD_CONV_DIM), F32),
                            pltpu.VMEM((2, L, SSD_INNER), F32),
                            pltpu.VMEM((2, L, LANES), F32),
                            pltpu.VMEM((L, SSD_INNER), F32)],
            compiler_params=_cparams(1),
            name="ssd_layer",
        )(xt, xt, xt, conv_buf, h0g, in_w_pad, *scan_args, out_w,
          ln_g.reshape(1, D_MODEL), ln_b.reshape(1, D_MODEL))
        return (out.reshape(bsz, t_len, D_MODEL), nbuf,
                hout.reshape(bsz, SSD_HEADS, SSD_HEADDIM, SSD_STATE))

    vl, L = t_len, SHORT_TILE
    assert vl <= L
    x2 = x.reshape(bsz * t_len, D_MODEL)
    splits = ((0, SSD_INNER), (SSD_INNER, SSD_CONV_DIM), (SSD_INNER + SSD_CONV_DIM, LANES))
    z, xbc, dt_raw = proj_split(x2, in_w_pad, splits, _row_tile(bsz * t_len))
    const = lambda shape: pl.BlockSpec(shape, lambda b: (0,) * len(shape))
    rows = lambda width: pl.BlockSpec((1, vl, width), lambda b: (b, 0, 0))
    state_specs = [pl.BlockSpec((1, CONV_W - 1, SSD_CONV_DIM), lambda b: (b, 0, 0)),
                   pl.BlockSpec((1, SSD_GROUPS, gw, SSD_STATE), lambda b: (b, 0, 0, 0))]
    yn, nbuf, hout = pl.pallas_call(
        functools.partial(_ssd_core_kernel, vl=vl, L=L),
        grid=(bsz,),
        in_specs=([rows(SSD_CONV_DIM), rows(SSD_INNER), rows(LANES)] + state_specs
                  + [const(sh) for sh in scan_shapes]),
        out_specs=[rows(SSD_INNER)] + state_specs,
        out_shape=[jax.ShapeDtypeStruct((bsz, t_len, SSD_INNER), F32)] + state_shapes,
        scratch_shapes=[pltpu.VMEM((HIST + L, SSD_CONV_DIM), F32),
                        pltpu.VMEM((L, SSD_INNER), F32),
                        pltpu.VMEM((L, LANES), F32),
                        pltpu.VMEM((L, SSD_INNER), F32)],
        compiler_params=_cparams(1),
        name="ssd_core",
    )(xbc.reshape(bsz, t_len, SSD_CONV_DIM), z.reshape(bsz, t_len, SSD_INNER),
      dt_raw.reshape(bsz, t_len, LANES), conv_buf, h0g, *scan_args)
    out = outproj_ln(yn.reshape(bsz * t_len, SSD_INNER), out_w, x2, ln_g, ln_b,
                     _row_tile(bsz * t_len)).reshape(bsz, t_len, D_MODEL)
    return out, nbuf, hout.reshape(bsz, SSD_HEADS, SSD_HEADDIM, SSD_STATE)


def _row_tile(rows):
    return ROW_TILE if rows % ROW_TILE == 0 else rows


def _out_tile(rows):
    return OUT_TILE if rows % OUT_TILE == 0 else _row_tile(rows)


def _attn_proj_t_kernel(x_ref, w_ref, wkt_ref, q_ref, kt_ref, v_ref, g_ref):
    x = x_ref[0].astype(BF16)
    q_ref[0] = jnp.dot(x, w_ref[:, 0:ATT_W], preferred_element_type=F32)
    kt_ref[0] = _nt(wkt_ref[...], x)
    v_ref[0] = jnp.dot(x, w_ref[:, 2 * ATT_W:3 * ATT_W], preferred_element_type=F32)
    g_ref[0] = jnp.dot(x, w_ref[:, 3 * ATT_W:4 * ATT_W], preferred_element_type=F32)


def attn_project_t(x, in_w, wkt, tm=ATT_TILE):
    bsz, t_len, _ = x.shape
    rows = lambda: pl.BlockSpec((1, tm, ATT_W), lambda b, i: (b, i, 0))
    return pl.pallas_call(
        _attn_proj_t_kernel,
        grid=(bsz, t_len // tm),
        in_specs=[pl.BlockSpec((1, tm, D_MODEL), lambda b, i: (b, i, 0)),
                  pl.BlockSpec((D_MODEL, 4 * ATT_W), lambda b, i: (0, 0)),
                  pl.BlockSpec((ATT_W, D_MODEL), lambda b, i: (0, 0))],
        out_specs=[rows(), pl.BlockSpec((1, ATT_W, tm), lambda b, i: (b, 0, i)), rows(), rows()],
        out_shape=[jax.ShapeDtypeStruct((bsz, t_len, ATT_W), F32),
                   jax.ShapeDtypeStruct((bsz, ATT_W, t_len), F32),
                   jax.ShapeDtypeStruct((bsz, t_len, ATT_W), F32),
                   jax.ShapeDtypeStruct((bsz, t_len, ATT_W), F32)],
        compiler_params=_cparams(2),
        name="attn_project_t",
    )(x, in_w, wkt)


def _attn_weights(in_w_f32):
    return in_w_f32.astype(BF16), in_w_f32[:, ATT_W:2 * ATT_W].T.astype(BF16)


def _attn_project(x, in_w):
    bsz, t_len, _ = x.shape
    x2 = x.reshape(bsz * t_len, D_MODEL)
    tm = _row_tile(bsz * t_len)
    splits = tuple((i * ATT_W, ATT_W) for i in range(4))
    q, k, v, gate = proj_split(x2, in_w, splits, tm)
    shp = (bsz, t_len, ATT_W)
    return x2, tm, q.reshape(shp), k.reshape(shp), v.reshape(shp), gate.reshape(shp)


def kernel(x_prompt, x_sample, cache_k, cache_v, page_table, state_lru_conv, state_lru_h, state_ssd_conv, state_ssd_h, ln_g, ln_b, a_in_w, a_conv_w, a_conv_b, a_gate_r_w, a_gate_r_b, a_gate_i_w, a_gate_i_b, a_lambda, a_out_w, b_in_w, b_lambda, b_subln_w, b_out_w, c_in_w, c_conv_w, c_conv_b, c_dt_bias, c_a_log, c_d, c_norm_w, c_out_w):
    xp, xs = x_prompt, x_sample
    bp, bs = xp.shape[0], xs.shape[0]
    tp, ts = xp.shape[1], xs.shape[1]
    k_p, v_p, k_s, v_s = [], [], [], []
    lc_p, lh_p, lc_s, lh_s = [], [], [], []
    sc_p, sh_p, sc_s, sh_s = [], [], [], []
    for i in range(DEPTH):
        j = i // N_MIXERS
        kind = i % N_MIXERS
        if kind == 0:
            w = _rglru_weights(a_in_w[j], a_conv_w[j], a_conv_b[j], a_gate_r_w[j], a_gate_r_b[j],
                               a_gate_i_w[j], a_gate_i_b[j], a_lambda[j], a_out_w[j])
            zc = jnp.zeros((bp, CONV_W - 1, LRU_W), F32)
            zh = jnp.zeros((bp, LRU_W), F32)
            xp, c1, h1 = rglru_layer(xp, zc, zh, w, ln_g[i], ln_b[i])
            xs, c2, h2 = rglru_layer(xs, state_lru_conv[j], state_lru_h[j], w, ln_g[i], ln_b[i])
            lc_p.append(c1); lh_p.append(h1); lc_s.append(c2); lh_s.append(h2)
        elif kind == 1:
            lam_init = 0.8 - 0.6 * math.exp(-0.3 * i)
            in_w, wkt = _attn_weights(b_in_w[j])
            out_w = b_out_w[j].astype(BF16)
            q, kt, v, g = attn_project_t(xp, in_w, wkt)
            x2s, tms, qs, ks, vs, gs = _attn_project(xs, in_w)
            og, ogs = attn_core(q, kt, v, g, qs, ks, vs, gs, cache_k, cache_v, j, page_table,
                                b_lambda[j], b_subln_w[j], lam_init)
            xp = outproj_ln(og.reshape(bp * tp, ATT_W), out_w, xp.reshape(bp * tp, D_MODEL),
                            ln_g[i], ln_b[i], _out_tile(bp * tp)).reshape(bp, tp, D_MODEL)
            xs = outproj_ln(ogs.reshape(bs * ts, ATT_W), out_w, x2s, ln_g[i], ln_b[i], tms
                            ).reshape(bs, ts, D_MODEL)
            k_p.append(jnp.transpose(kt.reshape(bp, ATT_HEADS, 2, ATT_DH, tp), (0, 4, 1, 2, 3)))
            v_p.append(v.reshape(bp, tp, ATT_HEADS, 2 * ATT_DH))
            k_s.append(ks.reshape(bs, ts, ATT_HEADS, 2, ATT_DH))
            v_s.append(vs.reshape(bs, ts, ATT_HEADS, 2 * ATT_DH))
        else:
            in_w_pad = jnp.pad(c_in_w[j], ((0, 0), (0, LANES - SSD_HEADS))).astype(BF16)
            w = (in_w_pad, c_conv_w[j], c_conv_b[j], c_dt_bias[j], c_a_log[j], c_d[j], c_norm_w[j],
                 c_out_w[j].astype(BF16))
            zc = jnp.zeros((bp, CONV_W - 1, SSD_CONV_DIM), F32)
            zh = jnp.zeros((bp, SSD_HEADS, SSD_HEADDIM, SSD_STATE), F32)
            xp, c1, h1 = ssd_layer(xp, zc, zh, w, ln_g[i], ln_b[i])
            xs, c2, h2 = ssd_layer(xs, state_ssd_conv[j], state_ssd_h[j], w, ln_g[i], ln_b[i])
            sc_p.append(c1); sh_p.append(h1); sc_s.append(c2); sh_s.append(h2)
    return (xp, xs, jnp.stack(k_p), jnp.stack(v_p), jnp.stack(k_s), jnp.stack(v_s),
            jnp.stack(lc_p), jnp.stack(lh_p), jnp.stack(lc_s), jnp.stack(lh_s),
            jnp.stack(sc_p), jnp.stack(sh_p), jnp.stack(sc_s), jnp.stack(sh_s))
```
